```python
import math
import jax, jax.numpy as jnp
from jax import lax
import numpy as np

D_MODEL = 1024
BATCH = 8
SEQ = 2048
DEPTH = 1
DEC_BATCH = 128
DEC_SEQ = 8
PAST_LEN = 16384
PAGE_SIZE = 128

N_META = 16
D_LRU = 1024
LRU_BLOCKS = 16
LRU_BW = D_LRU // LRU_BLOCKS
LRU_CONV = 4
LRU_C = 8.0
ML_HEADS = 4
D_ML = 1024
ML_DH = D_ML // ML_HEADS
MLSTM_CHUNK = 64
D_FF = 2816
FFN_CONV = 3
N_IN = D_LRU + 4 * D_ML + 2 * ML_HEADS + 2 * D_MODEL
EPS = 1e-6

kernel_name = "hawk_mlstm_parallel_gated_convffn_step"


def rms_norm(x, g):
    xf = x.astype(jnp.float32)
    out = xf * lax.rsqrt(jnp.mean(xf * xf, axis=-1, keepdims=True) + EPS)
    return (out * g.astype(jnp.float32)).astype(x.dtype)


def causal_dwconv(u, buf, w, b):
    L = u.shape[1]
    W = w.shape[0]
    full = jnp.concatenate([buf.astype(u.dtype), u], axis=1)
    out = b + sum(full[:, j:j + L] * w[j] for j in range(W))
    return out, full[:, L:]


def rg_lru(x, h0, w_r, b_r, w_i, b_i, lam):
    B, L, C = x.shape
    xb = x.reshape(B, L, LRU_BLOCKS, LRU_BW)
    r = jax.nn.sigmoid(jnp.einsum('blnc,ncd->blnd', xb, w_r).reshape(B, L, C) + b_r)
    i = jax.nn.sigmoid(jnp.einsum('blnc,ncd->blnd', xb, w_i).reshape(B, L, C) + b_i)
    log_a = -LRU_C * r.astype(jnp.float32) * jax.nn.softplus(-lam.astype(jnp.float32))
    a = jnp.exp(log_a)
    inp = jnp.sqrt(-jnp.expm1(2.0 * log_a)) * (i * x).astype(jnp.float32)

    def step(h, t):
        a_t, u_t = t
        h = a_t * h + u_t
        return h, h

    h_last, hs = lax.scan(step, h0.astype(jnp.float32),
                          (jnp.swapaxes(a, 0, 1), jnp.swapaxes(inp, 0, 1)))
    return jnp.swapaxes(hs, 0, 1).astype(x.dtype), h_last


def mlstm_chunk(carry, inp):
    C, n, m = carry
    q, k, v, ig, lf = inp
    L = q.shape[2]
    b = jnp.cumsum(lf, axis=-1)
    causal = jnp.tril(jnp.ones((L, L), dtype=bool))
    dmat = jnp.where(causal, b[..., :, None] - b[..., None, :] + ig[..., None, :], -jnp.inf)
    inter = b + m[..., None]
    m_t = jnp.maximum(jnp.max(dmat, axis=-1), inter)
    w = jnp.exp(dmat - m_t[..., None])
    s = jnp.einsum('bhtd,bhsd->bhts', q, k) * w
    e_int = jnp.exp(inter - m_t)
    num = jnp.einsum('bhts,bhsv->bhtv', s, v) + e_int[..., None] * jnp.einsum('bhtd,bhdv->bhtv', q, C)
    den = jnp.sum(s, axis=-1) + e_int * jnp.einsum('bhtd,bhd->bht', q, n)
    h = num / jnp.maximum(jnp.abs(den), jnp.exp(-m_t))[..., None]
    bL = b[..., -1]
    g = bL[..., None] - b + ig
    m_new = jnp.maximum(bL + m, jnp.max(g, axis=-1))
    decay = jnp.exp(bL + m - m_new)
    wk = jnp.exp(g - m_new[..., None])
    C_new = decay[..., None, None] * C + jnp.einsum('bhs,bhsd,bhsv->bhdv', wk, k, v)
    n_new = decay[..., None] * n + jnp.einsum('bhs,bhsd->bhd', wk, k)
    return (C_new, n_new, m_new), h


def mlstm_seq(q, k, v, ig, lf, C, n, m, lead):
    B, H, Ltot, _ = q.shape
    carry = (C, n, m)
    outs = []
    if lead > 0:
        carry, h0 = mlstm_chunk(carry, (q[:, :, :lead], k[:, :, :lead], v[:, :, :lead],
                                        ig[:, :, :lead], lf[:, :, :lead]))
        outs.append(h0)
    T = Ltot - lead
    cs = math.gcd(T, MLSTM_CHUNK)
    nc = T // cs

    def to_chunks(a):
        a = a[:, :, lead:]
        a = a.reshape(a.shape[:2] + (nc, cs) + a.shape[3:])
        return jnp.moveaxis(a, 2, 0)

    carry, hs = lax.scan(mlstm_chunk, carry,
                         (to_chunks(q), to_chunks(k), to_chunks(v), to_chunks(ig), to_chunks(lf)))
    hs = jnp.moveaxis(hs, 0, 2).reshape(B, H, T, v.shape[-1])
    outs.append(hs)
    return jnp.concatenate(outs, axis=2), carry


def conv_ffn(xn, buf, w_up, cw, cb, w_down):
    up = xn @ w_up
    upc, new_buf = causal_dwconv(up, buf, cw, cb)
    val, gate = jnp.split(upc, 2, axis=-1)
    return (jax.nn.gelu(gate) * val) @ w_down, new_buf


def block(x, lead, st, lp):
    lru_conv_buf, lru_h, C, n, m, ffn_buf = st
    sdt = lru_h.dtype
    B, L, _ = x.shape
    xn = rms_norm(x, lp['norm1_g'])
    z = xn @ lp['w_in'] + lp['b_in']
    sizes = [D_LRU, D_ML, D_ML, D_ML, D_ML, ML_HEADS, ML_HEADS, D_MODEL]
    u, q, k, v, o, ipre, fpre, ga, gb = jnp.split(z, np.cumsum(sizes).tolist(), axis=-1)
    uc, new_lru_conv = causal_dwconv(u, lru_conv_buf, lp['lru_conv_w'], lp['lru_conv_b'])
    ha, new_lru_h = rg_lru(uc, lru_h, lp['lru_w_r'], lp['lru_b_r'], lp['lru_w_i'], lp['lru_b_i'], lp['lru_lambda'])

    def heads(a):
        return a.reshape(B, L, ML_HEADS, ML_DH).transpose(0, 2, 1, 3).astype(jnp.float32)

    qh = heads(q) * (ML_DH ** -0.5)
    kh = heads(k)
    vh = heads(v)
    ig = ipre.astype(jnp.float32).transpose(0, 2, 1)
    lf = jax.nn.log_sigmoid(fpre.astype(jnp.float32)).transpose(0, 2, 1)
    hb, (Cn, nn_, mn) = mlstm_seq(qh, kh, vh, ig, lf, C.astype(jnp.float32), n.astype(jnp.float32),
                                  m.astype(jnp.float32), lead)
    hb = hb.transpose(0, 2, 1, 3)
    hb = hb * lax.rsqrt(jnp.mean(hb * hb, axis=-1, keepdims=True) + EPS)
    hb = (hb.reshape(B, L, D_ML) * lp['mlstm_head_g'].astype(jnp.float32)).astype(x.dtype) * jax.nn.sigmoid(o)
    merged = jax.nn.sigmoid(ga) * (ha @ lp['w_branch_a']) + jax.nn.sigmoid(gb) * (hb @ lp['w_branch_b'])
    x = x + merged @ lp['w_out']
    f, new_ffn = conv_ffn(rms_norm(x, lp['norm2_g']), ffn_buf, lp['w_up'], lp['ffn_conv_w'],
                          lp['ffn_conv_b'], lp['w_down'])
    x = x + f
    return x, (new_lru_conv.astype(sdt), new_lru_h.astype(sdt), Cn.astype(sdt), nn_.astype(sdt),
               mn.astype(sdt), new_ffn.astype(sdt))


def setup_inputs(seed: int = 0) -> dict:
    key = jax.random.key(seed)
    ks = jax.random.split(key, 32)

    def nrm(k, shape, s):
        return jax.random.normal(k, shape, jnp.float32) * s

    f_off = D_LRU + 4 * D_ML + ML_HEADS
    b_in = nrm(ks[11], (DEPTH, N_IN), 0.01)
    b_in = b_in.at[:, f_off:f_off + ML_HEADS].add(jnp.linspace(3.0, 6.0, ML_HEADS))
    a0 = jax.random.uniform(ks[18], (DEPTH, D_LRU), jnp.float32, minval=0.9, maxval=0.999)
    s0 = a0 ** (1.0 / LRU_C)
    lru_lambda = jnp.log(s0) - jnp.log1p(-s0)
    return {
        "x_prompt": nrm(ks[0], (BATCH, SEQ, D_MODEL), 1.0),
        "x_sample": nrm(ks[1], (DEC_BATCH, DEC_SEQ, D_MODEL), 1.0),
        "state_lru_conv": nrm(ks[2], (DEPTH, DEC_BATCH, LRU_CONV - 1, D_LRU), 1.0),
        "state_lru_h": nrm(ks[3], (DEPTH, DEC_BATCH, D_LRU), 0.5),
        "state_mlstm_C": nrm(ks[4], (DEPTH, DEC_BATCH, ML_HEADS, ML_DH, ML_DH), 0.05),
        "state_mlstm_n": nrm(ks[5], (DEPTH, DEC_BATCH, ML_HEADS, ML_DH), 0.05),
        "state_mlstm_m": nrm(ks[6], (DEPTH, DEC_BATCH, ML_HEADS), 1.0),
        "state_ffn_conv": nrm(ks[7], (DEPTH, DEC_BATCH, FFN_CONV - 1, 2 * D_FF), 1.0),
        "meta_tokens": nrm(ks[8], (N_META, D_MODEL), 1.0),
        "norm1_g": 1.0 + nrm(ks[9], (DEPTH, D_MODEL), 0.02),
        "w_in": nrm(ks[10], (DEPTH, D_MODEL, N_IN), D_MODEL ** -0.5),
        "b_in": b_in,
        "lru_conv_w": nrm(ks[12], (DEPTH, LRU_CONV, D_LRU), LRU_CONV ** -0.5),
        "lru_conv_b": nrm(ks[13], (DEPTH, D_LRU), 0.01),
        "lru_w_r": nrm(ks[14], (DEPTH, LRU_BLOCKS, LRU_BW, LRU_BW), LRU_BW ** -0.5),
        "lru_b_r": nrm(ks[15], (DEPTH, D_LRU), 0.01),
        "lru_w_i": nrm(ks[16], (DEPTH, LRU_BLOCKS, LRU_BW, LRU_BW), LRU_BW ** -0.5),
        "lru_b_i": nrm(ks[17], (DEPTH, D_LRU), 0.01),
        "lru_lambda": lru_lambda,
        "mlstm_head_g": 1.0 + nrm(ks[19], (DEPTH, D_ML), 0.02),
        "w_branch_a": nrm(ks[20], (DEPTH, D_LRU, D_MODEL), D_LRU ** -0.5),
        "w_branch_b": nrm(ks[21], (DEPTH, D_ML, D_MODEL), D_ML ** -0.5),
        "w_out": nrm(ks[22], (DEPTH, D_MODEL, D_MODEL), D_MODEL ** -0.5),
        "norm2_g": 1.0 + nrm(ks[23], (DEPTH, D_MODEL), 0.02),
        "w_up": nrm(ks[24], (DEPTH, D_MODEL, 2 * D_FF), D_MODEL ** -0.5),
        "ffn_conv_w": nrm(ks[25], (DEPTH, FFN_CONV, 2 * D_FF), FFN_CONV ** -0.5),
        "ffn_conv_b": nrm(ks[26], (DEPTH, 2 * D_FF), 0.01),
        "w_down": nrm(ks[27], (DEPTH, D_FF, D_MODEL), D_FF ** -0.5),
        "final_g": 1.0 + nrm(ks[28], (D_MODEL,), 0.02),
    }


def reference(x_prompt, x_sample, state_lru_conv, state_lru_h, state_mlstm_C, state_mlstm_n,
              state_mlstm_m, state_ffn_conv, meta_tokens, norm1_g, w_in, b_in, lru_conv_w, lru_conv_b,
              lru_w_r, lru_b_r, lru_w_i, lru_b_i, lru_lambda, mlstm_head_g, w_branch_a, w_branch_b,
              w_out, norm2_g, w_up, ffn_conv_w, ffn_conv_b, w_down, final_g):
    dt = x_prompt.dtype
    bp = x_prompt.shape[0]
    meta = jnp.broadcast_to(meta_tokens[None].astype(dt), (bp, N_META, D_MODEL))
    xp = jnp.concatenate([meta, x_prompt], axis=1)
    xs = x_sample
    new_p = []
    new_s = []
    for l in range(DEPTH):
        lp = {
            'norm1_g': norm1_g[l], 'w_in': w_in[l], 'b_in': b_in[l],
            'lru_conv_w': lru_conv_w[l], 'lru_conv_b': lru_conv_b[l],
            'lru_w_r': lru_w_r[l], 'lru_b_r': lru_b_r[l], 'lru_w_i': lru_w_i[l], 'lru_b_i': lru_b_i[l],
            'lru_lambda': lru_lambda[l], 'mlstm_head_g': mlstm_head_g[l],
            'w_branch_a': w_branch_a[l], 'w_branch_b': w_branch_b[l], 'w_out': w_out[l],
            'norm2_g': norm2_g[l], 'w_up': w_up[l], 'ffn_conv_w': ffn_conv_w[l],
            'ffn_conv_b': ffn_conv_b[l], 'w_down': w_down[l],
        }
        zero = (jnp.zeros((bp, LRU_CONV - 1, D_LRU), dt), jnp.zeros((bp, D_LRU), dt),
                jnp.zeros((bp, ML_HEADS, ML_DH, ML_DH), dt), jnp.zeros((bp, ML_HEADS, ML_DH), dt),
                jnp.zeros((bp, ML_HEADS), dt), jnp.zeros((bp, FFN_CONV - 1, 2 * D_FF), dt))
        xp, sp = block(xp, N_META, zero, lp)
        xs, ss = block(xs, 0, (state_lru_conv[l], state_lru_h[l], state_mlstm_C[l], state_mlstm_n[l],
                               state_mlstm_m[l], state_ffn_conv[l]), lp)
        new_p.append(sp)
        new_s.append(ss)

    def stk(lst, i):
        return jnp.stack([t[i] for t in lst], axis=0)

    y_prompt = rms_norm(xp[:, N_META:], final_g)
    y_sample = rms_norm(xs, final_g)
    return (y_prompt, y_sample,
            stk(new_p, 0), stk(new_p, 1), stk(new_p, 2), stk(new_p, 3), stk(new_p, 4), stk(new_p, 5),
            stk(new_s, 0), stk(new_s, 1), stk(new_s, 2), stk(new_s, 3), stk(new_s, 4), stk(new_s, 5))
```

```python
import functools

import jax
import jax.numpy as jnp
from jax import lax
from jax.experimental import pallas as pl
from jax.experimental.pallas import tpu as pltpu

F32 = jnp.float32
BF16 = jnp.bfloat16

D_MODEL = 1024
D_LRU = 1024
LRU_BLOCKS = 16
LRU_CONV = 4
LRU_C = 8.0
N_HEADS = 4
D_HEAD = 256
D_FF = 2816
FFN_CONV = 3
N_META = 16
EPS = 1e-6

V7X_LANES = 128
V7X_SUBLANES = 8
V7X_MXU_DIM = 256
NEG_BIG = -1e30

N_MAIN = 7 * D_MODEL
GATE_W = 2 * V7X_LANES


def _resident(shape):
    return pl.BlockSpec(shape, lambda *_: (0,) * len(shape), pipeline_mode=pl.Buffered(1))


def _params(n_grid, vmem_mb):
    return pltpu.CompilerParams(
        dimension_semantics=("arbitrary",) * n_grid,
        vmem_limit_bytes=vmem_mb * 1024 * 1024,
    )


def _rms(x, g):
    ms = jnp.mean(x * x, axis=-1, keepdims=True)
    return x * lax.rsqrt(ms + EPS) * g


def _proj_kernel(x_ref, g_ref, w_ref, b_ref, wg_ref, bg_ref,
                 u_ref, q_ref, k_ref, v_ref, o_ref, ga_ref, gb_ref, gt_ref):
    xn = _rms(x_ref[...], g_ref[...]).astype(BF16)

    def mm(j):
        sl = slice(j * D_MODEL, (j + 1) * D_MODEL)
        return jnp.dot(xn, w_ref[:, sl], preferred_element_type=F32) + b_ref[:, sl]

    u_ref[...] = mm(0)
    q_ref[...] = (mm(1) * (D_HEAD ** -0.5)).astype(BF16)
    k_ref[...] = mm(2).astype(BF16)
    v_ref[...] = mm(3).astype(BF16)
    o_ref[...] = mm(4)
    ga_ref[...] = mm(5)
    gb_ref[...] = mm(6)
    gt_ref[...] = jnp.dot(xn, wg_ref[...], preferred_element_type=F32) + bg_ref[...]


def _proj(x2, P, tm):
    M = x2.shape[0]
    row = lambda w: pl.BlockSpec((tm, w), lambda i: (i, 0))
    f32o = jax.ShapeDtypeStruct((M, D_MODEL), F32)
    bf16o = jax.ShapeDtypeStruct((M, D_MODEL), BF16)
    return pl.pallas_call(
        _proj_kernel,
        grid=(M // tm,),
        in_specs=[row(D_MODEL), _resident((1, D_MODEL)), _resident((D_MODEL, N_MAIN)),
                  _resident((1, N_MAIN)), _resident((D_MODEL, GATE_W)), _resident((1, GATE_W))],
        out_specs=[row(D_MODEL)] * 7 + [row(GATE_W)],
        out_shape=[f32o, bf16o, bf16o, bf16o, f32o, f32o, f32o,
                   jax.ShapeDtypeStruct((M, GATE_W), F32)],
        compiler_params=_params(1, 48),
        name="proj",
    )(x2, P["norm1_g"], P["w_main"], P["b_main"], P["w_gate"], P["b_gate"])


def _lru_kernel(NS, TS, u_ref, conv0_ref, h0_ref, cw_ref, cb_ref, wr_ref, br_ref, wi_ref,
                bi_ref, lam_ref, ha_ref, convo_ref, ho_ref, ext_ref, h_ref, a_ref, hs_ref):
    ti = pl.program_id(1)
    R = NS * TS
    C = D_LRU
    PAD = V7X_SUBLANES

    @pl.when(ti == 0)
    def _():
        ext_ref[:, PAD - 3:PAD, :] = conv0_ref[...]
        h_ref[...] = h0_ref[...]

    u = u_ref[...].reshape(NS, TS, C)
    ext_ref[:, PAD:, :] = u
    cw = cw_ref[...]
    uc = (cb_ref[...] + cw[0:1] * ext_ref[:, PAD - 3:PAD - 3 + TS, :]
          + cw[1:2] * ext_ref[:, PAD - 2:PAD - 2 + TS, :]
          + cw[2:3] * ext_ref[:, PAD - 1:PAD - 1 + TS, :] + cw[3:4] * u)
    new_conv = ext_ref[:, PAD + TS - 3:PAD + TS, :]
    ext_ref[:, PAD - 3:PAD, :] = new_conv
    convo_ref[...] = new_conv

    uc2 = uc.reshape(R, C)
    ucb = uc2.astype(BF16)

    def block_diag(w_ref):
        W = V7X_MXU_DIM
        return jnp.concatenate(
            [jnp.dot(ucb[:, g * W:(g + 1) * W], w_ref[g], preferred_element_type=F32)
             for g in range(C // W)], axis=1)

    r = jax.nn.sigmoid(block_diag(wr_ref) + br_ref[...])
    i = jax.nn.sigmoid(block_diag(wi_ref) + bi_ref[...])
    log_a = -LRU_C * r * jax.nn.softplus(-lam_ref[...])
    a = jnp.exp(log_a)
    inp = jnp.sqrt(-jnp.tanh(log_a) * (a * a + 1.0)) * (i * uc2)

    sub = lax.broadcasted_iota(jnp.int32, (R, C), 0) & (V7X_SUBLANES - 1)
    hh = inp
    for d in (1, 2, 4):
        keep = sub >= d
        a_sh = pltpu.roll(a, d, axis=0)
        h_sh = pltpu.roll(hh, d, axis=0)
        hh = hh + a * jnp.where(keep, h_sh, 0.0)
        a = a * jnp.where(keep, a_sh, 1.0)
    a_ref[...] = a.reshape(NS, TS, C)
    hs_ref[...] = hh.reshape(NS, TS, C)

    def carry(j, h):
        r0 = pl.multiple_of(j * V7X_SUBLANES, V7X_SUBLANES)
        rows = pl.ds(r0, V7X_SUBLANES)
        hj = hs_ref[:, rows, :] + a_ref[:, rows, :] * h
        hs_ref[:, rows, :] = hj
        return jnp.broadcast_to(hj[:, V7X_SUBLANES - 1:, :], hj.shape)

    h_in = jnp.broadcast_to(h_ref[...], (NS, V7X_SUBLANES, C))
    h_out = lax.fori_loop(0, TS // V7X_SUBLANES, carry, h_in)
    h_ref[...] = h_out[:, 0:1, :]
    ho_ref[...] = h_out[:, 0:1, :]
    ha_ref[...] = hs_ref[...].reshape(R, C).astype(BF16)


def _lru(u, conv0, h0, P, NSEQ, L, NS, TS):
    R = NS * TS
    C = D_LRU
    NT = L // TS
    rows = pl.BlockSpec((R, C), lambda s, t: (s * NT + t, 0))
    nb = C // V7X_MXU_DIM
    return pl.pallas_call(
        functools.partial(_lru_kernel, NS, TS),
        grid=(NSEQ // NS, NT),
        in_specs=[rows,
                  pl.BlockSpec((NS, LRU_CONV - 1, C), lambda s, t: (s, 0, 0)),
                  pl.BlockSpec((NS, 1, C), lambda s, t: (s, 0, 0)),
                  _resident((LRU_CONV, C)), _resident((1, C)),
                  _resident((nb, V7X_MXU_DIM, V7X_MXU_DIM)), _resident((1, C)),
                  _resident((nb, V7X_MXU_DIM, V7X_MXU_DIM)), _resident((1, C)),
                  _resident((1, C))],
        out_specs=[rows,
                   pl.BlockSpec((NS, LRU_CONV - 1, C), lambda s, t: (s, 0, 0)),
                   pl.BlockSpec((NS, 1, C), lambda s, t: (s, 0, 0))],
        out_shape=[jax.ShapeDtypeStruct((NSEQ * L, C), BF16),
                   jax.ShapeDtypeStruct((NSEQ, LRU_CONV - 1, C), F32),
                   jax.ShapeDtypeStruct((NSEQ, 1, C), F32)],
        scratch_shapes=[pltpu.VMEM((NS, V7X_SUBLANES + TS, C), F32),
                        pltpu.VMEM((NS, 1, C), F32),
                        pltpu.VMEM((NS, TS, C), F32),
                        pltpu.VMEM((NS, TS, C), F32)],
        compiler_params=_params(2, 48),
        name="lru",
    )(u, conv0, h0, P["lru_conv_w"], P["lru_conv_b"], P["w_r"], P["lru_b_r"], P["w_i"],
      P["lru_b_i"], P["lru_lambda"])


def _seg_scan(x, tpos, TS, op, ident):
    d = 1
    while d < TS:
        sh = pltpu.roll(x, d, axis=0)
        x = op(x, jnp.where(tpos >= d, sh, ident))
        d *= 2
    return x


def _pad_rows(x, rows):
    if x.shape[0] >= rows:
        return x
    return jnp.concatenate([x, jnp.zeros((rows - x.shape[0],) + x.shape[1:], x.dtype)], axis=0)


def _mlstm_kernel(NS, TS, q_ref, k_ref, v_ref, gt_ref, o_ref, hg_ref, c0_ref, n0_ref, m0_ref,
                  hb_ref, co_ref, no_ref, mo_ref, c_s, n_s, m_s):
    ti = pl.program_id(1)
    R = NS * TS
    RC = max(R, V7X_LANES)
    LN = V7X_LANES
    shift = TS.bit_length() - 1

    @pl.when(ti == 0)
    def _():
        c_s[...] = c0_ref[...]
        n_s[...] = n0_ref[...]
        m_s[...] = m0_ref[...]

    gt = gt_ref[...]
    ig4 = gt[:, :LN]
    lf4 = jax.nn.log_sigmoid(gt[:, LN:])
    tpos = lax.broadcasted_iota(jnp.int32, (R, LN), 0) & (TS - 1)
    b4 = _seg_scan(lf4, tpos, TS, jnp.add, 0.0)
    c4 = ig4 - b4
    cmax4 = _seg_scan(c4, tpos, TS, jnp.maximum, -jnp.inf)
    m_prev = [m_s[j] for j in range(NS)]
    m_rows = jnp.concatenate([jnp.broadcast_to(m, (TS, LN)) for m in m_prev], axis=0)
    big_m4 = jnp.maximum(cmax4, m_rows)
    e4 = jnp.exp(m_rows - big_m4)
    dinv4 = jnp.exp(-(b4 + big_m4))

    decay4, wk_parts, m_new = [], [], []
    for j in range(NS):
        b_last = b4[(j + 1) * TS - 1:(j + 1) * TS, :]
        g4 = b_last + c4[j * TS:(j + 1) * TS, :]
        mn = jnp.maximum(b_last + m_prev[j], jnp.max(g4, axis=0, keepdims=True))
        decay4.append(jnp.exp(b_last + m_prev[j] - mn))
        wk_parts.append(jnp.exp(g4 - mn))
        m_new.append(mn)
    wk4 = jnp.concatenate(wk_parts, axis=0)

    ri = lax.broadcasted_iota(jnp.int32, (R, RC), 0)
    ci = lax.broadcasted_iota(jnp.int32, (R, RC), 1)
    eye = ri == ci
    if NS == 1:
        causal = ci <= ri
    else:
        causal = (ci <= ri) & ((ri >> shift) == (ci >> shift))
    seq_of_row = lax.broadcasted_iota(jnp.int32, (R, D_HEAD), 0) >> shift

    for h in range(N_HEADS):
        sl = slice(h * D_HEAD, (h + 1) * D_HEAD)
        qh = q_ref[:, sl]
        kh = k_ref[:, sl]
        vh = v_ref[:, sl]
        kh_p = _pad_rows(kh, RC)
        vh_p = _pad_rows(vh, RC)
        c_c = c4[:, h:h + 1]
        big_m_c = big_m4[:, h:h + 1]
        e_c = e4[:, h:h + 1]
        dinv_c = dinv4[:, h:h + 1]
        wk_c = wk4[:, h:h + 1]

        qk = lax.dot_general(qh, kh_p, (((1,), (1,)), ((), ())), preferred_element_type=F32)
        c_r = jnp.sum(jnp.where(eye, c_c, 0.0), axis=0, keepdims=True)
        w = jnp.exp(jnp.where(causal, c_r - big_m_c, NEG_BIG))
        s = qk * w
        den = jnp.sum(s, axis=1, keepdims=True)
        num = jnp.dot(s.astype(BF16), vh_p, preferred_element_type=F32)

        qf = qh.astype(F32)
        if NS == 1:
            q_c = jnp.dot(qh, c_s[0, h].astype(BF16), preferred_element_type=F32)
            n_rows = n_s[0, h:h + 1, :]
        else:
            q_c = jnp.zeros((R, D_HEAD), F32)
            n_rows = jnp.zeros((R, D_HEAD), F32)
            for j in range(NS):
                mine = seq_of_row == j
                q_c = jnp.where(mine, jnp.dot(qh, c_s[j, h].astype(BF16),
                                              preferred_element_type=F32), q_c)
                n_rows = jnp.where(mine, n_s[j, h:h + 1, :], n_rows)
        q_n = jnp.sum(qf * n_rows, axis=1, keepdims=True)
        num = num + e_c * q_c
        den = den + e_c * q_n
        hh = num * (1.0 / jnp.maximum(jnp.abs(den), dinv_c))
        hh = hh * lax.rsqrt(jnp.mean(hh * hh, axis=1, keepdims=True) + EPS)
        out = (hh * hg_ref[:, sl]) * jax.nn.sigmoid(o_ref[:, sl])
        hb_ref[:, sl] = out.astype(BF16)

        kw = kh.astype(F32) * wk_c
        for j in range(NS):
            kwj = kw if NS == 1 else jnp.where(seq_of_row == j, kw, 0.0)
            upd = lax.dot_general(_pad_rows(kwj, RC).astype(BF16), vh_p,
                                  (((0,), (0,)), ((), ())), preferred_element_type=F32)
            dec = decay4[j][:, h:h + 1]
            c_s[j, h] = dec * c_s[j, h] + upd
            n_s[j, h:h + 1, :] = dec * n_s[j, h:h + 1, :] + jnp.sum(kwj, axis=0, keepdims=True)

    for j in range(NS):
        m_s[j] = m_new[j]

    @pl.when(ti == pl.num_programs(1) - 1)
    def _():
        co_ref[...] = c_s[...]
        no_ref[...] = n_s[...]
        mo_ref[...] = m_s[...]


def _mlstm(q, k, v, gt, o, c0, n0, m0, P, NSEQ, L, NS, TS):
    R = NS * TS
    NT = L // TS
    rows = lambda w: pl.BlockSpec((R, w), lambda s, t: (s * NT + t, 0))
    cspec = pl.BlockSpec((NS, N_HEADS, D_HEAD, D_HEAD), lambda s, t: (s, 0, 0, 0))
    nspec = pl.BlockSpec((NS, N_HEADS, D_HEAD), lambda s, t: (s, 0, 0))
    mspec = pl.BlockSpec((NS, 1, V7X_LANES), lambda s, t: (s, 0, 0))
    return pl.pallas_call(
        functools.partial(_mlstm_kernel, NS, TS),
        grid=(NSEQ // NS, NT),
        in_specs=[rows(D_MODEL), rows(D_MODEL), rows(D_MODEL), rows(GATE_W), rows(D_MODEL),
                  _resident((1, D_MODEL)), cspec, nspec, mspec],
        out_specs=[rows(D_MODEL), cspec, nspec, mspec],
        out_shape=[jax.ShapeDtypeStruct((NSEQ * L, D_MODEL), BF16),
                   jax.ShapeDtypeStruct((NSEQ, N_HEADS, D_HEAD, D_HEAD), F32),
                   jax.ShapeDtypeStruct((NSEQ, N_HEADS, D_HEAD), F32),
                   jax.ShapeDtypeStruct((NSEQ, 1, V7X_LANES), F32)],
        scratch_shapes=[pltpu.VMEM((NS, N_HEADS, D_HEAD, D_HEAD), F32),
                        pltpu.VMEM((NS, N_HEADS, D_HEAD), F32),
                        pltpu.VMEM((NS, 1, V7X_LANES), F32)],
        compiler_params=_params(2, 48),
        name="mlstm",
    )(q, k, v, gt, o, P["mlstm_head_g"], c0, n0, m0)


def _post_kernel(x_ref, ha_ref, hb_ref, ga_ref, gb_ref, wa_ref, wb_ref, wo_ref, x1_ref):
    pa = jnp.dot(ha_ref[...], wa_ref[...], preferred_element_type=F32)
    pb = jnp.dot(hb_ref[...], wb_ref[...], preferred_element_type=F32)
    merged = jax.nn.sigmoid(ga_ref[...]) * pa + jax.nn.sigmoid(gb_ref[...]) * pb
    x1_ref[...] = x_ref[...] + jnp.dot(merged.astype(BF16), wo_ref[...],
                                       preferred_element_type=F32)


def _post(x2, ha, hb, ga, gb, P, tm):
    M = x2.shape[0]
    row = pl.BlockSpec((tm, D_MODEL), lambda i: (i, 0))
    wsp = _resident((D_MODEL, D_MODEL))
    return pl.pallas_call(
        _post_kernel,
        grid=(M // tm,),
        in_specs=[row] * 5 + [wsp] * 3,
        out_specs=row,
        out_shape=jax.ShapeDtypeStruct((M, D_MODEL), F32),
        compiler_params=_params(1, 48),
        name="post",
    )(x2, ha, hb, ga, gb, P["w_branch_a"], P["w_branch_b"], P["w_out"])


def _ffn_kernel(NS, TS, x1_ref, st0_ref, g2_ref, wup_ref, cw_ref, cb_ref, wdn_ref, gf_ref,
                y_ref, sto_ref, ext_ref):
    ti = pl.program_id(1)
    R = NS * TS
    W = 2 * D_FF
    PAD = V7X_SUBLANES

    @pl.when(ti == 0)
    def _():
        ext_ref[:, PAD - 2:PAD, :] = st0_ref[...]

    x1 = x1_ref[...]
    xn = _rms(x1, g2_ref[...]).astype(BF16)
    up = jnp.dot(xn, wup_ref[...], preferred_element_type=F32).reshape(NS, TS, W)
    ext_ref[:, PAD:, :] = up
    cw = cw_ref[...]
    upc = (cb_ref[...] + cw[0:1] * ext_ref[:, PAD - 2:PAD - 2 + TS, :]
           + cw[1:2] * ext_ref[:, PAD - 1:PAD - 1 + TS, :] + cw[2:3] * up)
    new_st = ext_ref[:, PAD + TS - 2:PAD + TS, :]
    ext_ref[:, PAD - 2:PAD, :] = new_st
    sto_ref[...] = new_st

    upc2 = upc.reshape(R, W)
    act = (jax.nn.gelu(upc2[:, D_FF:]) * upc2[:, :D_FF]).astype(BF16)
    x2 = x1 + jnp.dot(act, wdn_ref[...], preferred_element_type=F32)
    y_ref[...] = _rms(x2, gf_ref[...])


def _ffn(x1, st0, P, NSEQ, L, NS, TS):
    R = NS * TS
    NT = L // TS
    W = 2 * D_FF
    rows = pl.BlockSpec((R, D_MODEL), lambda s, t: (s * NT + t, 0))
    stspec = pl.BlockSpec((NS, FFN_CONV - 1, W), lambda s, t: (s, 0, 0))
    return pl.pallas_call(
        functools.partial(_ffn_kernel, NS, TS),
        grid=(NSEQ // NS, NT),
        in_specs=[rows, stspec, _resident((1, D_MODEL)), _resident((D_MODEL, W)),
                  _resident((FFN_CONV, W)), _resident((1, W)), _resident((D_FF, D_MODEL)),
                  _resident((1, D_MODEL))],
        out_specs=[rows, stspec],
        out_shape=[jax.ShapeDtypeStruct((NSEQ * L, D_MODEL), F32),
                   jax.ShapeDtypeStruct((NSEQ, FFN_CONV - 1, W), F32)],
        scratch_shapes=[pltpu.VMEM((NS, V7X_SUBLANES + TS, W), F32)],
        compiler_params=_params(2, 56),
        name="ffn",
    )(x1, st0, P["norm2_g"], P["w_up"], P["ffn_conv_w"], P["ffn_conv_b"], P["w_down"],
      P["final_g"])


def _block_diag(w):
    per = V7X_MXU_DIM // w.shape[1]
    nb = w.shape[0] // per
    bw = w.shape[1]
    out = jnp.zeros((nb, V7X_MXU_DIM, V7X_MXU_DIM), w.dtype)
    for p in range(per):
        out = out.at[:, p * bw:(p + 1) * bw, p * bw:(p + 1) * bw].set(w[p::per])
    return out


def _run_group(x3, state, P, tiles):
    NSEQ, L, _ = x3.shape
    conv0, h0, c0, n0, m0, ffn0 = state
    x2 = x3.reshape(NSEQ * L, D_MODEL)
    u, q, k, v, o, ga, gb, gt = _proj(x2, P, tiles["tm"])
    ha, conv1, h1 = _lru(u, conv0, h0, P, NSEQ, L, tiles["lru_ns"], tiles["ts"])
    hb, c1, n1, m1 = _mlstm(q, k, v, gt, o, c0, n0, m0, P, NSEQ, L, tiles["ml_ns"], tiles["ts"])
    x1 = _post(x2, ha, hb, ga, gb, P, tiles["tm"])
    y, ffn1 = _ffn(x1, ffn0, P, NSEQ, L, tiles["lru_ns"], tiles["ts"])
    return y.reshape(NSEQ, L, D_MODEL), (conv1, h1, c1, n1, m1, ffn1)


def kernel(x_prompt, x_sample, state_lru_conv, state_lru_h, state_mlstm_C, state_mlstm_n,
           state_mlstm_m, state_ffn_conv, meta_tokens, norm1_g, w_in, b_in, lru_conv_w,
           lru_conv_b, lru_w_r, lru_b_r, lru_w_i, lru_b_i, lru_lambda, mlstm_head_g,
           w_branch_a, w_branch_b, w_out, norm2_g, w_up, ffn_conv_w, ffn_conv_b, w_down, final_g):
    assert w_in.shape[0] == 1, "single-layer trunk"
    n_qkvo = D_LRU + 4 * D_MODEL
    w0, b0 = w_in[0], b_in[0]
    gate_cols = lambda a: (jnp.zeros(a.shape[:-1] + (GATE_W,), a.dtype)
                           .at[..., :N_HEADS].set(a[..., n_qkvo:n_qkvo + N_HEADS])
                           .at[..., V7X_LANES:V7X_LANES + N_HEADS]
                           .set(a[..., n_qkvo + N_HEADS:n_qkvo + 2 * N_HEADS]))
    main_cols = lambda a: jnp.concatenate([a[..., :n_qkvo], a[..., n_qkvo + 2 * N_HEADS:]], axis=-1)
    row = lambda a: a.reshape(1, -1).astype(F32)
    P = {
        "norm1_g": row(norm1_g[0]),
        "w_main": main_cols(w0).astype(BF16),
        "b_main": row(main_cols(b0)),
        "w_gate": gate_cols(w0).astype(BF16),
        "b_gate": row(gate_cols(b0)),
        "lru_conv_w": lru_conv_w[0],
        "lru_conv_b": row(lru_conv_b[0]),
        "w_r": _block_diag(lru_w_r[0]).astype(BF16),
        "lru_b_r": row(lru_b_r[0]),
        "w_i": _block_diag(lru_w_i[0]).astype(BF16),
        "lru_b_i": row(lru_b_i[0]),
        "lru_lambda": row(lru_lambda[0]),
        "mlstm_head_g": row(mlstm_head_g[0]),
        "w_branch_a": w_branch_a[0].astype(BF16),
        "w_branch_b": w_branch_b[0].astype(BF16),
        "w_out": w_out[0].astype(BF16),
        "norm2_g": row(norm2_g[0]),
        "w_up": w_up[0].astype(BF16),
        "ffn_conv_w": ffn_conv_w[0],
        "ffn_conv_b": row(ffn_conv_b[0]),
        "w_down": w_down[0].astype(BF16),
        "final_g": row(final_g),
    }

    def pack_state(conv, h, c, n, m, ffn):
        nseq = h.shape[0]
        m_pad = jnp.zeros((nseq, 1, V7X_LANES), F32).at[:, 0, :N_HEADS].set(m.astype(F32))
        return (conv.astype(F32), h.astype(F32).reshape(nseq, 1, D_LRU), c.astype(F32),
                n.astype(F32), m_pad, ffn.astype(F32))

    def unpack_state(st):
        conv, h, c, n, m, ffn = st
        return (conv[None], h.reshape(1, -1, D_LRU), c[None], n[None],
                m[:, 0, :N_HEADS][None], ffn[None])

    n_prompt = x_prompt.shape[0]
    zero = pack_state(jnp.zeros((1, LRU_CONV - 1, D_LRU), F32), jnp.zeros((1, D_LRU), F32),
                      jnp.zeros((1, N_HEADS, D_HEAD, D_HEAD), F32),
                      jnp.zeros((1, N_HEADS, D_HEAD), F32), jnp.zeros((1, N_HEADS), F32),
                      jnp.zeros((1, FFN_CONV - 1, 2 * D_FF), F32))
    _, meta_state = _run_group(meta_tokens[None].astype(F32), zero, P,
                               dict(tm=N_META, ts=N_META, lru_ns=1, ml_ns=1))
    prompt_state0 = tuple(jnp.broadcast_to(s, (n_prompt,) + s.shape[1:]) for s in meta_state)
    y_prompt, prompt_state = _run_group(x_prompt, prompt_state0, P,
                                        dict(tm=256, ts=256, lru_ns=1, ml_ns=1))
    sample_state0 = pack_state(state_lru_conv[0], state_lru_h[0], state_mlstm_C[0],
                               state_mlstm_n[0], state_mlstm_m[0], state_ffn_conv[0])
    y_sample, sample_state = _run_group(x_sample, sample_state0, P,
                                        dict(tm=256, ts=x_sample.shape[1], lru_ns=32, ml_ns=4))
    return (y_prompt, y_sample) + unpack_state(prompt_state) + unpack_state(sample_state)
```

```python
import functools

import jax
import jax.numpy as jnp
from jax import lax
from jax.experimental import pallas as pl
from jax.experimental.pallas import tpu as pltpu

F32 = jnp.float32
BF16 = jnp.bfloat16

D_MODEL = 1024
D_LRU = 1024
LRU_CONV = 4
LRU_C = 8.0
N_HEADS = 4
D_HEAD = 256
D_FF = 2816
FFN_CONV = 3
N_META = 16
EPS = 1e-6

V7X_LANES = 128
V7X_SUBLANES = 8
V7X_MXU_DIM = 256
NEG_BIG = -1e30

N_MAIN = 7 * D_MODEL
GATE_W = 2 * V7X_LANES
COL_U, COL_Q, COL_K, COL_V, COL_O, COL_GA, COL_GB = (j * D_MODEL for j in range(7))


def _resident(shape):
    return pl.BlockSpec(shape, lambda *_: (0,) * len(shape), pipeline_mode=pl.Buffered(1))


def _params(n_grid, vmem_mb):
    return pltpu.CompilerParams(
        dimension_semantics=("arbitrary",) * n_grid,
        vmem_limit_bytes=vmem_mb * 1024 * 1024,
    )


def _rms(x, g):
    ms = jnp.mean(x * x, axis=-1, keepdims=True)
    return x * lax.rsqrt(ms + EPS) * g


def _in_proj(xn, w_ref, b_ref, col, width):
    sl = slice(col, col + width)
    return jnp.dot(xn, w_ref[:, sl], preferred_element_type=F32) + b_ref[:, sl]


def _conv_init(ext_ref, hist0_ref, taps):
    pad, hist = V7X_SUBLANES, taps - 1
    for g in range(ext_ref.shape[1]):
        ls = slice(g * V7X_LANES, (g + 1) * V7X_LANES)
        ext_ref[:, g, pad - hist:pad, :] = hist0_ref[:, :, ls]


def _causal_conv(ext_ref, x3, cw, cb, hist_out_ref):
    taps = cw.shape[0]
    ts = x3.shape[1]
    pad, hist = V7X_SUBLANES, taps - 1
    outs = []
    for g in range(ext_ref.shape[1]):
        ls = slice(g * V7X_LANES, (g + 1) * V7X_LANES)
        xg = x3[:, :, ls]
        ext_ref[:, g, pad:, :] = xg
        acc = cb[:, ls] + cw[taps - 1:taps, ls] * xg
        for j in range(hist):
            acc = acc + cw[j:j + 1, ls] * ext_ref[:, g, pad - hist + j:pad - hist + j + ts, :]
        outs.append(acc)
        new_hist = ext_ref[:, g, pad + ts - hist:pad + ts, :]
        ext_ref[:, g, pad - hist:pad, :] = new_hist
        hist_out_ref[:, :, ls] = new_hist
    return jnp.concatenate(outs, axis=-1)


def _lru_body(NS, TS, u2, cw, cb, wr_ref, br, wi_ref, bi, lam, ext_ref, h_ref, convo_ref, ho_ref):
    R = NS * TS
    C = D_LRU
    SB = V7X_SUBLANES
    uc2 = _causal_conv(ext_ref, u2.reshape(NS, TS, C), cw, cb, convo_ref).reshape(R, C)
    ucb = uc2.astype(BF16)

    def block_diag(w_ref):
        W = V7X_MXU_DIM
        return jnp.concatenate(
            [jnp.dot(ucb[:, g * W:(g + 1) * W], w_ref[g], preferred_element_type=F32)
             for g in range(C // W)], axis=1)

    r = jax.nn.sigmoid(block_diag(wr_ref) + br)
    i = jax.nn.sigmoid(block_diag(wi_ref) + bi)
    log_a = -LRU_C * r * jax.nn.softplus(-lam)
    a = jnp.exp(log_a)
    hh = jnp.sqrt(-jnp.tanh(log_a) * (a * a + 1.0)) * (i * uc2)

    a = a.reshape(R // SB, SB, C)
    hh = hh.reshape(R // SB, SB, C)
    sub = lax.broadcasted_iota(jnp.int32, (1, SB, C), 1)
    for d in (1, 2, 4):
        keep = sub >= d
        a_sh = pltpu.roll(a, d, axis=1)
        h_sh = pltpu.roll(hh, d, axis=1)
        hh = hh + a * jnp.where(keep, h_sh, 0.0)
        a = a * jnp.where(keep, a_sh, 1.0)

    nb = TS // SB
    a = a.reshape(NS, nb, SB, C)
    hh = hh.reshape(NS, nb, SB, C)
    h = jnp.broadcast_to(h_ref[...], (NS, SB, C))
    blocks = []
    for j in range(nb):
        hj = hh[:, j] + a[:, j] * h
        blocks.append(hj)
        h = jnp.broadcast_to(hj[:, SB - 1:, :], (NS, SB, C))
    h_ref[...] = h[:, 0:1, :]
    ho_ref[...] = h[:, 0:1, :]
    return jnp.concatenate(blocks, axis=1).reshape(R, C)


def _seg_scan(x, tpos, TS, op, ident):
    d = 1
    while d < TS:
        sh = pltpu.roll(x, d, axis=0)
        x = op(x, jnp.where(tpos >= d, sh, ident))
        d *= 2
    return x


def _pad_rows(x, rows):
    if x.shape[0] >= rows:
        return x
    return jnp.concatenate([x, jnp.zeros((rows - x.shape[0],) + x.shape[1:], x.dtype)], axis=0)


def _mlstm_body(NS, TS, get_qkvo, gt, hg_ref, c_s, n_s, m_s):
    R = NS * TS
    RC = max(R, V7X_LANES)
    LN = V7X_LANES
    shift = TS.bit_length() - 1

    ig4 = gt[:, :LN]
    lf4 = jax.nn.log_sigmoid(gt[:, LN:])
    tpos = lax.broadcasted_iota(jnp.int32, (R, LN), 0) & (TS - 1)
    b4 = _seg_scan(lf4, tpos, TS, jnp.add, 0.0)
    c4 = ig4 - b4
    cmax4 = _seg_scan(c4, tpos, TS, jnp.maximum, -jnp.inf)
    m_prev = [m_s[j] for j in range(NS)]
    m_rows = jnp.concatenate([jnp.broadcast_to(m, (TS, LN)) for m in m_prev], axis=0)
    big_m4 = jnp.maximum(cmax4, m_rows)
    e4 = jnp.exp(m_rows - big_m4)
    dinv4 = jnp.exp(-(b4 + big_m4))

    decay4, wk_parts = [], []
    for j in range(NS):
        b_last = b4[(j + 1) * TS - 1:(j + 1) * TS, :]
        g4 = b_last + c4[j * TS:(j + 1) * TS, :]
        mn = jnp.maximum(b_last + m_prev[j], jnp.max(g4, axis=0, keepdims=True))
        decay4.append(jnp.exp(b_last + m_prev[j] - mn))
        wk_parts.append(jnp.exp(g4 - mn))
        m_s[j] = mn
    wk4 = jnp.concatenate(wk_parts, axis=0)

    ri = lax.broadcasted_iota(jnp.int32, (R, RC), 0)
    ci = lax.broadcasted_iota(jnp.int32, (R, RC), 1)
    eye = ri == ci
    if NS == 1:
        causal = ci <= ri
    else:
        causal = (ci <= ri) & ((ri >> shift) == (ci >> shift))
    seq_of_row = lax.broadcasted_iota(jnp.int32, (R, D_HEAD), 0) >> shift

    outs = []
    for h in range(N_HEADS):
        sl = slice(h * D_HEAD, (h + 1) * D_HEAD)
        qh, kh, vh, oh = get_qkvo(h)
        kh_p = _pad_rows(kh, RC)
        vh_p = _pad_rows(vh, RC)
        c_c = c4[:, h:h + 1]
        big_m_c = big_m4[:, h:h + 1]
        e_c = e4[:, h:h + 1]
        dinv_c = dinv4[:, h:h + 1]
        wk_c = wk4[:, h:h + 1]

        qk = lax.dot_general(qh, kh_p, (((1,), (1,)), ((), ())), preferred_element_type=F32)
        c_r = jnp.sum(jnp.where(eye, c_c, 0.0), axis=0, keepdims=True)
        w = jnp.exp(jnp.where(causal, c_r - big_m_c, NEG_BIG))
        s = qk * w
        den = jnp.sum(s, axis=1, keepdims=True)
        num = jnp.dot(s.astype(BF16), vh_p, preferred_element_type=F32)

        qf = qh.astype(F32)
        if NS == 1:
            q_c = jnp.dot(qh, c_s[0, h].astype(BF16), preferred_element_type=F32)
            n_rows = n_s[0, h:h + 1, :]
        else:
            q_c = jnp.zeros((R, D_HEAD), F32)
            n_rows = jnp.zeros((R, D_HEAD), F32)
            for j in range(NS):
                mine = seq_of_row == j
                q_c = jnp.where(mine, jnp.dot(qh, c_s[j, h].astype(BF16),
                                              preferred_element_type=F32), q_c)
                n_rows = jnp.where(mine, n_s[j, h:h + 1, :], n_rows)
        q_n = jnp.sum(qf * n_rows, axis=1, keepdims=True)
        num = num + e_c * q_c
        den = den + e_c * q_n
        hh = num * (1.0 / jnp.maximum(jnp.abs(den), dinv_c))
        hh = hh * lax.rsqrt(jnp.mean(hh * hh, axis=1, keepdims=True) + EPS)
        outs.append(((hh * hg_ref[:, sl]) * jax.nn.sigmoid(oh)).astype(BF16))

        kw = kh.astype(F32) * wk_c
        for j in range(NS):
            kwj = kw if NS == 1 else jnp.where(seq_of_row == j, kw, 0.0)
            upd = lax.dot_general(_pad_rows(kwj, RC).astype(BF16), vh_p,
                                  (((0,), (0,)), ((), ())), preferred_element_type=F32)
            dec = decay4[j][:, h:h + 1]
            c_s[j, h] = dec * c_s[j, h] + upd
            n_s[j, h:h + 1, :] = dec * n_s[j, h:h + 1, :] + jnp.sum(kwj, axis=0, keepdims=True)
    return outs


def _merge_out(x, ha, hb, ga, gb, wa_ref, wb_ref, wo_ref):
    pa = jnp.dot(ha, wa_ref[...], preferred_element_type=F32)
    pb = jnp.dot(hb, wb_ref[...], preferred_element_type=F32)
    merged = jax.nn.sigmoid(ga) * pa + jax.nn.sigmoid(gb) * pb
    return x + jnp.dot(merged.astype(BF16), wo_ref[...], preferred_element_type=F32)


def _mixer_kernel(TS, x_ref, conv0_ref, h0_ref, c0_ref, n0_ref, m0_ref,
                  g_ref, w_ref, b_ref, wg_ref, bg_ref, cw_ref, cb_ref, wr_ref, br_ref, wi_ref,
                  bi_ref, lam_ref, hg_ref, wa_ref, wb_ref, wo_ref,
                  x1_ref, convo_ref, ho_ref, co_ref, no_ref, mo_ref,
                  ext_ref, h_s, c_s, n_s, m_s):
    ti = pl.program_id(1)

    @pl.when(ti == 0)
    def _():
        _conv_init(ext_ref, conv0_ref, LRU_CONV)
        h_s[...] = h0_ref[...]
        c_s[...] = c0_ref[...]
        n_s[...] = n0_ref[...]
        m_s[...] = m0_ref[...]

    x = x_ref[...]
    xn = _rms(x, g_ref[...]).astype(BF16)
    proj = functools.partial(_in_proj, xn, w_ref, b_ref)

    hs = _lru_body(1, TS, proj(COL_U, D_LRU), cw_ref[...], cb_ref[...], wr_ref, br_ref[...],
                   wi_ref, bi_ref[...], lam_ref[...], ext_ref, h_s, convo_ref, ho_ref)

    def get_qkvo(h):
        off = h * D_HEAD
        q = (proj(COL_Q + off, D_HEAD) * (D_HEAD ** -0.5)).astype(BF16)
        return (q, proj(COL_K + off, D_HEAD).astype(BF16), proj(COL_V + off, D_HEAD).astype(BF16),
                proj(COL_O + off, D_HEAD))

    gt = jnp.dot(xn, wg_ref[...], preferred_element_type=F32) + bg_ref[...]
    hb = jnp.concatenate(_mlstm_body(1, TS, get_qkvo, gt, hg_ref, c_s, n_s, m_s), axis=1)
    x1_ref[...] = _merge_out(x, hs.astype(BF16), hb, proj(COL_GA, D_MODEL), proj(COL_GB, D_MODEL),
                             wa_ref, wb_ref, wo_ref)

    @pl.when(ti == pl.num_programs(1) - 1)
    def _():
        co_ref[...] = c_s[...]
        no_ref[...] = n_s[...]
        mo_ref[...] = m_s[...]


def _lru_weight_specs():
    nb = D_LRU // V7X_MXU_DIM
    return [_resident((LRU_CONV, D_LRU)), _resident((1, D_LRU)),
            _resident((nb, V7X_MXU_DIM, V7X_MXU_DIM)), _resident((1, D_LRU)),
            _resident((nb, V7X_MXU_DIM, V7X_MXU_DIM)), _resident((1, D_LRU)),
            _resident((1, D_LRU))]


def _lru_weights(P):
    return (P["lru_conv_w"], P["lru_conv_b"], P["w_r"], P["lru_b_r"], P["w_i"], P["lru_b_i"],
            P["lru_lambda"])


def _state_specs(NS):
    return [pl.BlockSpec((NS, LRU_CONV - 1, D_LRU), lambda s, t: (s, 0, 0)),
            pl.BlockSpec((NS, 1, D_LRU), lambda s, t: (s, 0, 0)),
            pl.BlockSpec((NS, N_HEADS, D_HEAD, D_HEAD), lambda s, t: (s, 0, 0, 0)),
            pl.BlockSpec((NS, N_HEADS, D_HEAD), lambda s, t: (s, 0, 0)),
            pl.BlockSpec((NS, 1, V7X_LANES), lambda s, t: (s, 0, 0))]


def _state_shapes(NSEQ):
    return [jax.ShapeDtypeStruct((NSEQ, LRU_CONV - 1, D_LRU), F32),
            jax.ShapeDtypeStruct((NSEQ, 1, D_LRU), F32),
            jax.ShapeDtypeStruct((NSEQ, N_HEADS, D_HEAD, D_HEAD), F32),
            jax.ShapeDtypeStruct((NSEQ, N_HEADS, D_HEAD), F32),
            jax.ShapeDtypeStruct((NSEQ, 1, V7X_LANES), F32)]


def _mixer(x2, state, P, NSEQ, L, TS):
    NT = L // TS
    rows = pl.BlockSpec((TS, D_MODEL), lambda s, t: (s * NT + t, 0))
    wsq = _resident((D_MODEL, D_MODEL))
    return pl.pallas_call(
        functools.partial(_mixer_kernel, TS),
        grid=(NSEQ, NT),
        in_specs=[rows] + _state_specs(1)
        + [_resident((1, D_MODEL)), _resident((D_MODEL, N_MAIN)), _resident((1, N_MAIN)),
           _resident((D_MODEL, GATE_W)), _resident((1, GATE_W))]
        + _lru_weight_specs() + [_resident((1, D_MODEL)), wsq, wsq, wsq],
        out_specs=[rows] + _state_specs(1),
        out_shape=[jax.ShapeDtypeStruct((NSEQ * L, D_MODEL), F32)] + _state_shapes(NSEQ),
        scratch_shapes=[pltpu.VMEM((1, D_LRU // V7X_LANES, V7X_SUBLANES + TS, V7X_LANES), F32),
                        pltpu.VMEM((1, 1, D_LRU), F32),
                        pltpu.VMEM((1, N_HEADS, D_HEAD, D_HEAD), F32),
                        pltpu.VMEM((1, N_HEADS, D_HEAD), F32),
                        pltpu.VMEM((1, 1, V7X_LANES), F32)],
        compiler_params=_params(2, 56),
        name="mixer",
    )(x2, *state, P["norm1_g"], P["w_main"], P["b_main"], P["w_gate"], P["b_gate"],
      *_lru_weights(P), P["mlstm_head_g"], P["w_branch_a"], P["w_branch_b"], P["w_out"])


def _proj_kernel(x_ref, g_ref, w_ref, b_ref, wg_ref, bg_ref,
                 u_ref, q_ref, k_ref, v_ref, o_ref, ga_ref, gb_ref, gt_ref):
    xn = _rms(x_ref[...], g_ref[...]).astype(BF16)
    proj = functools.partial(_in_proj, xn, w_ref, b_ref)
    u_ref[...] = proj(COL_U, D_MODEL)
    q_ref[...] = (proj(COL_Q, D_MODEL) * (D_HEAD ** -0.5)).astype(BF16)
    k_ref[...] = proj(COL_K, D_MODEL).astype(BF16)
    v_ref[...] = proj(COL_V, D_MODEL).astype(BF16)
    o_ref[...] = proj(COL_O, D_MODEL)
    ga_ref[...] = proj(COL_GA, D_MODEL)
    gb_ref[...] = proj(COL_GB, D_MODEL)
    gt_ref[...] = jnp.dot(xn, wg_ref[...], preferred_element_type=F32) + bg_ref[...]


def _proj(x2, P, tm):
    M = x2.shape[0]
    row = lambda w: pl.BlockSpec((tm, w), lambda i: (i, 0))
    f32o = jax.ShapeDtypeStruct((M, D_MODEL), F32)
    bf16o = jax.ShapeDtypeStruct((M, D_MODEL), BF16)
    return pl.pallas_call(
        _proj_kernel,
        grid=(M // tm,),
        in_specs=[row(D_MODEL), _resident((1, D_MODEL)), _resident((D_MODEL, N_MAIN)),
                  _resident((1, N_MAIN)), _resident((D_MODEL, GATE_W)), _resident((1, GATE_W))],
        out_specs=[row(D_MODEL)] * 7 + [row(GATE_W)],
        out_shape=[f32o, bf16o, bf16o, bf16o, f32o, f32o, f32o,
                   jax.ShapeDtypeStruct((M, GATE_W), F32)],
        compiler_params=_params(1, 48),
        name="proj",
    )(x2, P["norm1_g"], P["w_main"], P["b_main"], P["w_gate"], P["b_gate"])


def _lru_kernel(NS, TS, u_ref, conv0_ref, h0_ref, cw_ref, cb_ref, wr_ref, br_ref, wi_ref,
                bi_ref, lam_ref, ha_ref, convo_ref, ho_ref, ext_ref, h_s):
    @pl.when(pl.program_id(1) == 0)
    def _():
        _conv_init(ext_ref, conv0_ref, LRU_CONV)
        h_s[...] = h0_ref[...]

    hs = _lru_body(NS, TS, u_ref[...], cw_ref[...], cb_ref[...], wr_ref, br_ref[...], wi_ref,
                   bi_ref[...], lam_ref[...], ext_ref, h_s, convo_ref, ho_ref)
    ha_ref[...] = hs.astype(BF16)


def _lru(u, conv0, h0, P, NSEQ, L, NS, TS):
    R = NS * TS
    NT = L // TS
    rows = pl.BlockSpec((R, D_LRU), lambda s, t: (s * NT + t, 0))
    st = _state_specs(NS)[:2]
    return pl.pallas_call(
        functools.partial(_lru_kernel, NS, TS),
        grid=(NSEQ // NS, NT),
        in_specs=[rows] + st + _lru_weight_specs(),
        out_specs=[rows] + st,
        out_shape=[jax.ShapeDtypeStruct((NSEQ * L, D_LRU), BF16)] + _state_shapes(NSEQ)[:2],
        scratch_shapes=[pltpu.VMEM((NS, D_LRU // V7X_LANES, V7X_SUBLANES + TS, V7X_LANES), F32),
                        pltpu.VMEM((NS, 1, D_LRU), F32)],
        compiler_params=_params(2, 48),
        name="lru",
    )(u, conv0, h0, *_lru_weights(P))


def _mlstm_kernel(NS, TS, q_ref, k_ref, v_ref, gt_ref, o_ref, hg_ref, c0_ref, n0_ref, m0_ref,
                  hb_ref, co_ref, no_ref, mo_ref, c_s, n_s, m_s):
    ti = pl.program_id(1)

    @pl.when(ti == 0)
    def _():
        c_s[...] = c0_ref[...]
        n_s[...] = n0_ref[...]
        m_s[...] = m0_ref[...]

    def get_qkvo(h):
        sl = slice(h * D_HEAD, (h + 1) * D_HEAD)
        return q_ref[:, sl], k_ref[:, sl], v_ref[:, sl], o_ref[:, sl]

    outs = _mlstm_body(NS, TS, get_qkvo, gt_ref[...], hg_ref, c_s, n_s, m_s)
    for h, out in enumerate(outs):
        hb_ref[:, h * D_HEAD:(h + 1) * D_HEAD] = out

    @pl.when(ti == pl.num_programs(1) - 1)
    def _():
        co_ref[...] = c_s[...]
        no_ref[...] = n_s[...]
        mo_ref[...] = m_s[...]


def _mlstm(q, k, v, gt, o, c0, n0, m0, P, NSEQ, L, NS, TS):
    R = NS * TS
    NT = L // TS
    rows = lambda w: pl.BlockSpec((R, w), lambda s, t: (s * NT + t, 0))
    st = _state_specs(NS)[2:]
    return pl.pallas_call(
        functools.partial(_mlstm_kernel, NS, TS),
        grid=(NSEQ // NS, NT),
        in_specs=[rows(D_MODEL), rows(D_MODEL), rows(D_MODEL), rows(GATE_W), rows(D_MODEL),
                  _resident((1, D_MODEL))] + st,
        out_specs=[rows(D_MODEL)] + st,
        out_shape=[jax.ShapeDtypeStruct((NSEQ * L, D_MODEL), BF16)] + _state_shapes(NSEQ)[2:],
        scratch_shapes=[pltpu.VMEM((NS, N_HEADS, D_HEAD, D_HEAD), F32),
                        pltpu.VMEM((NS, N_HEADS, D_HEAD), F32),
                        pltpu.VMEM((NS, 1, V7X_LANES), F32)],
        compiler_params=_params(2, 48),
        name="mlstm",
    )(q, k, v, gt, o, P["mlstm_head_g"], c0, n0, m0)


def _post_kernel(x_ref, ha_ref, hb_ref, ga_ref, gb_ref, wa_ref, wb_ref, wo_ref, x1_ref):
    x1_ref[...] = _merge_out(x_ref[...], ha_ref[...], hb_ref[...], ga_ref[...], gb_ref[...],
                             wa_ref, wb_ref, wo_ref)


def _post(x2, ha, hb, ga, gb, P, tm):
    M = x2.shape[0]
    row = pl.BlockSpec((tm, D_MODEL), lambda i: (i, 0))
    wsp = _resident((D_MODEL, D_MODEL))
    return pl.pallas_call(
        _post_kernel,
        grid=(M // tm,),
        in_specs=[row] * 5 + [wsp] * 3,
        out_specs=row,
        out_shape=jax.ShapeDtypeStruct((M, D_MODEL), F32),
        compiler_params=_params(1, 48),
        name="post",
    )(x2, ha, hb, ga, gb, P["w_branch_a"], P["w_branch_b"], P["w_out"])


def _ffn_kernel(NS, TS, x1_ref, st0_ref, g2_ref, wup_ref, cw_ref, cb_ref, wdn_ref, gf_ref,
                y_ref, sto_ref, ext_ref):
    R = NS * TS
    W = 2 * D_FF

    @pl.when(pl.program_id(1) == 0)
    def _():
        _conv_init(ext_ref, st0_ref, FFN_CONV)

    x1 = x1_ref[...]
    xn = _rms(x1, g2_ref[...]).astype(BF16)
    up = jnp.dot(xn, wup_ref[...], preferred_element_type=F32).reshape(NS, TS, W)
    upc = _causal_conv(ext_ref, up, cw_ref[...], cb_ref[...], sto_ref).reshape(R, W)
    act = (jax.nn.gelu(upc[:, D_FF:]) * upc[:, :D_FF]).astype(BF16)
    x2 = x1 + jnp.dot(act, wdn_ref[...], preferred_element_type=F32)
    y_ref[...] = _rms(x2, gf_ref[...])


def _ffn(x1, st0, P, NSEQ, L, NS, TS):
    R = NS * TS
    NT = L // TS
    W = 2 * D_FF
    rows = pl.BlockSpec((R, D_MODEL), lambda s, t: (s * NT + t, 0))
    stspec = pl.BlockSpec((NS, FFN_CONV - 1, W), lambda s, t: (s, 0, 0))
    return pl.pallas_call(
        functools.partial(_ffn_kernel, NS, TS),
        grid=(NSEQ // NS, NT),
        in_specs=[rows, stspec, _resident((1, D_MODEL)), _resident((D_MODEL, W)),
                  _resident((FFN_CONV, W)), _resident((1, W)), _resident((D_FF, D_MODEL)),
                  _resident((1, D_MODEL))],
        out_specs=[rows, stspec],
        out_shape=[jax.ShapeDtypeStruct((NSEQ * L, D_MODEL), F32),
                   jax.ShapeDtypeStruct((NSEQ, FFN_CONV - 1, W), F32)],
        scratch_shapes=[pltpu.VMEM((NS, W // V7X_LANES, V7X_SUBLANES + TS, V7X_LANES), F32)],
        compiler_params=_params(2, 56),
        name="ffn",
    )(x1, st0, P["norm2_g"], P["w_up"], P["ffn_conv_w"], P["ffn_conv_b"], P["w_down"],
      P["final_g"])


def _block_diag(w):
    per = V7X_MXU_DIM // w.shape[1]
    nb = w.shape[0] // per
    bw = w.shape[1]
    out = jnp.zeros((nb, V7X_MXU_DIM, V7X_MXU_DIM), w.dtype)
    for p in range(per):
        out = out.at[:, p * bw:(p + 1) * bw, p * bw:(p + 1) * bw].set(w[p::per])
    return out


def _run_group_fused(x3, state, P, TS):
    NSEQ, L, _ = x3.shape
    x2 = x3.reshape(NSEQ * L, D_MODEL)
    x1, conv1, h1, c1, n1, m1 = _mixer(x2, state[:5], P, NSEQ, L, TS)
    y, ffn1 = _ffn(x1, state[5], P, NSEQ, L, 1, TS)
    return y.reshape(NSEQ, L, D_MODEL), (conv1, h1, c1, n1, m1, ffn1)


def _run_group_staged(x3, state, P, tiles):
    NSEQ, L, _ = x3.shape
    conv0, h0, c0, n0, m0, ffn0 = state
    x2 = x3.reshape(NSEQ * L, D_MODEL)
    u, q, k, v, o, ga, gb, gt = _proj(x2, P, tiles["tm"])
    ha, conv1, h1 = _lru(u, conv0, h0, P, NSEQ, L, tiles["lru_ns"], tiles["ts"])
    hb, c1, n1, m1 = _mlstm(q, k, v, gt, o, c0, n0, m0, P, NSEQ, L, tiles["ml_ns"], tiles["ts"])
    x1 = _post(x2, ha, hb, ga, gb, P, tiles["tm"])
    y, ffn1 = _ffn(x1, ffn0, P, NSEQ, L, tiles["lru_ns"], tiles["ts"])
    return y.reshape(NSEQ, L, D_MODEL), (conv1, h1, c1, n1, m1, ffn1)


def kernel(x_prompt, x_sample, state_lru_conv, state_lru_h, state_mlstm_C, state_mlstm_n,
           state_mlstm_m, state_ffn_conv, meta_tokens, norm1_g, w_in, b_in, lru_conv_w,
           lru_conv_b, lru_w_r, lru_b_r, lru_w_i, lru_b_i, lru_lambda, mlstm_head_g,
           w_branch_a, w_branch_b, w_out, norm2_g, w_up, ffn_conv_w, ffn_conv_b, w_down, final_g):
    assert w_in.shape[0] == 1, "single-layer trunk"
    n_qkvo = D_LRU + 4 * D_MODEL
    w0, b0 = w_in[0], b_in[0]
    gate_cols = lambda a: (jnp.zeros(a.shape[:-1] + (GATE_W,), a.dtype)
                           .at[..., :N_HEADS].set(a[..., n_qkvo:n_qkvo + N_HEADS])
                           .at[..., V7X_LANES:V7X_LANES + N_HEADS]
                           .set(a[..., n_qkvo + N_HEADS:n_qkvo + 2 * N_HEADS]))
    main_cols = lambda a: jnp.concatenate([a[..., :n_qkvo], a[..., n_qkvo + 2 * N_HEADS:]], axis=-1)
    row = lambda a: a.reshape(1, -1).astype(F32)
    P = {
        "norm1_g": row(norm1_g[0]),
        "w_main": main_cols(w0).astype(BF16),
        "b_main": row(main_cols(b0)),
        "w_gate": gate_cols(w0).astype(BF16),
        "b_gate": row(gate_cols(b0)),
        "lru_conv_w": lru_conv_w[0],
        "lru_conv_b": row(lru_conv_b[0]),
        "w_r": _block_diag(lru_w_r[0]).astype(BF16),
        "lru_b_r": row(lru_b_r[0]),
        "w_i": _block_diag(lru_w_i[0]).astype(BF16),
        "lru_b_i": row(lru_b_i[0]),
        "lru_lambda": row(lru_lambda[0]),
        "mlstm_head_g": row(mlstm_head_g[0]),
        "w_branch_a": w_branch_a[0].astype(BF16),
        "w_branch_b": w_branch_b[0].astype(BF16),
        "w_out": w_out[0].astype(BF16),
        "norm2_g": row(norm2_g[0]),
        "w_up": w_up[0].astype(BF16),
        "ffn_conv_w": ffn_conv_w[0],
        "ffn_conv_b": row(ffn_conv_b[0]),
        "w_down": w_down[0].astype(BF16),
        "final_g": row(final_g),
    }

    def pack_state(conv, h, c, n, m, ffn):
        nseq = h.shape[0]
        m_pad = jnp.zeros((nseq, 1, V7X_LANES), F32).at[:, 0, :N_HEADS].set(m.astype(F32))
        return (conv.astype(F32), h.astype(F32).reshape(nseq, 1, D_LRU), c.astype(F32),
                n.astype(F32), m_pad, ffn.astype(F32))

    def unpack_state(st):
        conv, h, c, n, m, ffn = st
        return (conv[None], h.reshape(1, -1, D_LRU), c[None], n[None],
                m[:, 0, :N_HEADS][None], ffn[None])

    n_prompt = x_prompt.shape[0]
    zero = pack_state(jnp.zeros((1, LRU_CONV - 1, D_LRU), F32), jnp.zeros((1, D_LRU), F32),
                      jnp.zeros((1, N_HEADS, D_HEAD, D_HEAD), F32),
                      jnp.zeros((1, N_HEADS, D_HEAD), F32), jnp.zeros((1, N_HEADS), F32),
                      jnp.zeros((1, FFN_CONV - 1, 2 * D_FF), F32))
    _, meta_state = _run_group_staged(meta_tokens[None].astype(F32), zero, P,
                                      dict(tm=N_META, ts=N_META, lru_ns=1, ml_ns=1))
    prompt_state0 = tuple(jnp.broadcast_to(s, (n_prompt,) + s.shape[1:]) for s in meta_state)
    y_prompt, prompt_state = _run_group_fused(x_prompt, prompt_state0, P, 256)
    sample_state0 = pack_state(state_lru_conv[0], state_lru_h[0], state_mlstm_C[0],
                               state_mlstm_n[0], state_mlstm_m[0], state_ffn_conv[0])
    y_sample, sample_state = _run_group_staged(
        x_sample, sample_state0, P, dict(tm=256, ts=x_sample.shape[1], lru_ns=32, ml_ns=4))
    return (y_prompt, y_sample) + unpack_state(prompt_state) + unpack_state(sample_state)
```

```python
import functools

import jax
import jax.numpy as jnp
from jax import lax
from jax.experimental import pallas as pl
from jax.experimental.pallas import tpu as pltpu

F32 = jnp.float32
BF16 = jnp.bfloat16

D_MODEL = 1024
D_LRU = 1024
LRU_CONV = 4
LRU_C = 8.0
N_HEADS = 4
D_HEAD = 256
D_FF = 2816
FFN_CONV = 3
N_META = 16
EPS = 1e-6

V7X_LANES = 128
V7X_SUBLANES = 8
V7X_MXU_DIM = 256
NEG_BIG = -1e30

N_MAIN = 7 * D_MODEL
N_W5 = 5 * D_MODEL
N_W2 = 2 * D_MODEL
GATE_W = 2 * V7X_LANES
COL_U, COL_Q, COL_K, COL_V, COL_O, COL_GA, COL_GB = (j * D_MODEL for j in range(7))


def _resident(shape):
    return pl.BlockSpec(shape, lambda *_: (0,) * len(shape), pipeline_mode=pl.Buffered(1))


def _params(n_grid, vmem_mb):
    return pltpu.CompilerParams(
        dimension_semantics=("arbitrary",) * n_grid,
        vmem_limit_bytes=vmem_mb * 1024 * 1024,
    )


def _rms(x, g):
    ms = jnp.mean(x * x, axis=-1, keepdims=True)
    return x * lax.rsqrt(ms + EPS) * g


def _in_proj(xn, w5_ref, w2_ref, b_ref, col, width):
    if col < N_W5:
        w = w5_ref[:, col:col + width]
    else:
        w = w2_ref[:, col - N_W5:col - N_W5 + width]
    return jnp.dot(xn, w, preferred_element_type=F32) + b_ref[:, col:col + width]


def _in_proj_specs():
    return [_resident((1, D_MODEL)), _resident((D_MODEL, N_W5)), _resident((D_MODEL, N_W2)),
            _resident((1, N_MAIN)), _resident((D_MODEL, GATE_W)), _resident((1, GATE_W))]


def _in_proj_weights(P):
    return (P["norm1_g"], P["w5"], P["w2"], P["b_main"], P["w_gate"], P["b_gate"])


def _conv_init(ext_ref, hist0_ref, taps):
    pad, hist = V7X_SUBLANES, taps - 1
    for g in range(ext_ref.shape[1]):
        ls = slice(g * V7X_LANES, (g + 1) * V7X_LANES)
        ext_ref[:, g, pad - hist:pad, :] = hist0_ref[:, :, ls]


def _causal_conv(ext_ref, x3, cw, cb, hist_out_ref):
    taps = cw.shape[0]
    ts = x3.shape[1]
    pad, hist = V7X_SUBLANES, taps - 1
    outs = []
    for g in range(ext_ref.shape[1]):
        ls = slice(g * V7X_LANES, (g + 1) * V7X_LANES)
        xg = x3[:, :, ls]
        ext_ref[:, g, pad:, :] = xg
        acc = cb[:, ls] + cw[taps - 1:taps, ls] * xg
        for j in range(hist):
            acc = acc + cw[j:j + 1, ls] * ext_ref[:, g, pad - hist + j:pad - hist + j + ts, :]
        outs.append(acc)
        new_hist = ext_ref[:, g, pad + ts - hist:pad + ts, :]
        ext_ref[:, g, pad - hist:pad, :] = new_hist
        hist_out_ref[:, :, ls] = new_hist
    return jnp.concatenate(outs, axis=-1)


def _lru_body(NS, TS, u2, cw, cb, wr_ref, br, wi_ref, bi, lam, ext_ref, h_ref, convo_ref, ho_ref):
    R = NS * TS
    C = D_LRU
    SB = V7X_SUBLANES
    uc2 = _causal_conv(ext_ref, u2.reshape(NS, TS, C), cw, cb, convo_ref).reshape(R, C)
    ucb = uc2.astype(BF16)

    def block_diag(w_ref):
        W = V7X_MXU_DIM
        return jnp.concatenate(
            [jnp.dot(ucb[:, g * W:(g + 1) * W], w_ref[g], preferred_element_type=F32)
             for g in range(C // W)], axis=1)

    r = jax.nn.sigmoid(block_diag(wr_ref) + br)
    i = jax.nn.sigmoid(block_diag(wi_ref) + bi)
    log_a = -LRU_C * r * jax.nn.softplus(-lam)
    a = jnp.exp(log_a)
    hh = jnp.sqrt(-jnp.tanh(log_a) * (a * a + 1.0)) * (i * uc2)

    a = a.reshape(R // SB, SB, C)
    hh = hh.reshape(R // SB, SB, C)
    sub = lax.broadcasted_iota(jnp.int32, (1, SB, C), 1)
    for d in (1, 2, 4):
        keep = sub >= d
        a_sh = pltpu.roll(a, d, axis=1)
        h_sh = pltpu.roll(hh, d, axis=1)
        hh = hh + a * jnp.where(keep, h_sh, 0.0)
        a = a * jnp.where(keep, a_sh, 1.0)

    nb = TS // SB
    a = a.reshape(NS, nb, SB, C)
    hh = hh.reshape(NS, nb, SB, C)
    h = jnp.broadcast_to(h_ref[...], (NS, SB, C))
    blocks = []
    for j in range(nb):
        hj = hh[:, j] + a[:, j] * h
        blocks.append(hj)
        h = jnp.broadcast_to(hj[:, SB - 1:, :], (NS, SB, C))
    h_ref[...] = h[:, 0:1, :]
    ho_ref[...] = h[:, 0:1, :]
    return jnp.concatenate(blocks, axis=1).reshape(R, C)


def _seg_scan(x, tpos, TS, op, ident):
    d = 1
    while d < TS:
        sh = pltpu.roll(x, d, axis=0)
        x = op(x, jnp.where(tpos >= d, sh, ident))
        d *= 2
    return x


def _pad_rows(x, rows):
    if x.shape[0] >= rows:
        return x
    return jnp.concatenate([x, jnp.zeros((rows - x.shape[0],) + x.shape[1:], x.dtype)], axis=0)


def _mlstm_body(NS, TS, get_qkvo, gt, hg_ref, c_s, n_s, m_s):
    R = NS * TS
    RC = max(R, V7X_LANES)
    LN = V7X_LANES
    shift = TS.bit_length() - 1

    ig4 = gt[:, :LN]
    lf4 = jax.nn.log_sigmoid(gt[:, LN:])
    tpos = lax.broadcasted_iota(jnp.int32, (R, LN), 0) & (TS - 1)
    b4 = _seg_scan(lf4, tpos, TS, jnp.add, 0.0)
    c4 = ig4 - b4
    cmax4 = _seg_scan(c4, tpos, TS, jnp.maximum, -jnp.inf)
    m_prev = [m_s[j] for j in range(NS)]
    m_rows = jnp.concatenate([jnp.broadcast_to(m, (TS, LN)) for m in m_prev], axis=0)
    big_m4 = jnp.maximum(cmax4, m_rows)
    e4 = jnp.exp(m_rows - big_m4)
    dinv4 = jnp.exp(-(b4 + big_m4))

    decay4, wk_parts = [], []
    for j in range(NS):
        b_last = b4[(j + 1) * TS - 1:(j + 1) * TS, :]
        g4 = b_last + c4[j * TS:(j + 1) * TS, :]
        mn = jnp.maximum(b_last + m_prev[j], jnp.max(g4, axis=0, keepdims=True))
        decay4.append(jnp.exp(b_last + m_prev[j] - mn))
        wk_parts.append(jnp.exp(g4 - mn))
        m_s[j] = mn
    wk4 = jnp.concatenate(wk_parts, axis=0)

    ri = lax.broadcasted_iota(jnp.int32, (R, RC), 0)
    ci = lax.broadcasted_iota(jnp.int32, (R, RC), 1)
    eye = ri == ci
    if NS == 1:
        causal = ci <= ri
    else:
        causal = (ci <= ri) & ((ri >> shift) == (ci >> shift))
    seq_of_row = lax.broadcasted_iota(jnp.int32, (R, D_HEAD), 0) >> shift

    outs = []
    for h in range(N_HEADS):
        sl = slice(h * D_HEAD, (h + 1) * D_HEAD)
        qh, kh, vh, oh = get_qkvo(h)
        kh_p = _pad_rows(kh, RC)
        vh_p = _pad_rows(vh, RC)
        c_c = c4[:, h:h + 1]
        big_m_c = big_m4[:, h:h + 1]
        e_c = e4[:, h:h + 1]
        dinv_c = dinv4[:, h:h + 1]
        wk_c = wk4[:, h:h + 1]

        qk = lax.dot_general(qh, kh_p, (((1,), (1,)), ((), ())), preferred_element_type=F32)
        c_r = jnp.sum(jnp.where(eye, c_c, 0.0), axis=0, keepdims=True)
        w = jnp.exp(jnp.where(causal, c_r - big_m_c, NEG_BIG))
        s = qk * w
        den = jnp.sum(s, axis=1, keepdims=True)
        num = jnp.dot(s.astype(BF16), vh_p, preferred_element_type=F32)

        qf = qh.astype(F32)
        if NS == 1:
            q_c = jnp.dot(qh, c_s[0, h].astype(BF16), preferred_element_type=F32)
            n_rows = n_s[0, h:h + 1, :]
        else:
            q_c = jnp.zeros((R, D_HEAD), F32)
            n_rows = jnp.zeros((R, D_HEAD), F32)
            for j in range(NS):
                mine = seq_of_row == j
                q_c = jnp.where(mine, jnp.dot(qh, c_s[j, h].astype(BF16),
                                              preferred_element_type=F32), q_c)
                n_rows = jnp.where(mine, n_s[j, h:h + 1, :], n_rows)
        q_n = jnp.sum(qf * n_rows, axis=1, keepdims=True)
        num = num + e_c * q_c
        den = den + e_c * q_n
        hh = num * (1.0 / jnp.maximum(jnp.abs(den), dinv_c))
        hh = hh * lax.rsqrt(jnp.mean(hh * hh, axis=1, keepdims=True) + EPS)
        outs.append(((hh * hg_ref[:, sl]) * jax.nn.sigmoid(oh)).astype(BF16))

        kw = kh.astype(F32) * wk_c
        for j in range(NS):
            kwj = kw if NS == 1 else jnp.where(seq_of_row == j, kw, 0.0)
            upd = lax.dot_general(_pad_rows(kwj, RC).astype(BF16), vh_p,
                                  (((0,), (0,)), ((), ())), preferred_element_type=F32)
            dec = decay4[j][:, h:h + 1]
            c_s[j, h] = dec * c_s[j, h] + upd
            n_s[j, h:h + 1, :] = dec * n_s[j, h:h + 1, :] + jnp.sum(kwj, axis=0, keepdims=True)
    return outs


def _merge_out(x, ha, hb, ga, gb, wa_ref, wb_ref, wo_ref):
    pa = jnp.dot(ha, wa_ref[...], preferred_element_type=F32)
    pb = jnp.dot(hb, wb_ref[...], preferred_element_type=F32)
    merged = jax.nn.sigmoid(ga) * pa + jax.nn.sigmoid(gb) * pb
    return x + jnp.dot(merged.astype(BF16), wo_ref[...], preferred_element_type=F32)


def _mixer_kernel(TS, x_ref, conv0_ref, h0_ref, c0_ref, n0_ref, m0_ref,
                  g_ref, w5_ref, w2_ref, b_ref, wg_ref, bg_ref, cw_ref, cb_ref, wr_ref, br_ref, wi_ref,
                  bi_ref, lam_ref, hg_ref, wa_ref, wb_ref, wo_ref,
                  x1_ref, convo_ref, ho_ref, co_ref, no_ref, mo_ref,
                  ext_ref, h_s, c_s, n_s, m_s):
    ti = pl.program_id(1)

    @pl.when(ti == 0)
    def _():
        _conv_init(ext_ref, conv0_ref, LRU_CONV)
        h_s[...] = h0_ref[...]
        c_s[...] = c0_ref[...]
        n_s[...] = n0_ref[...]
        m_s[...] = m0_ref[...]

    x = x_ref[...]
    xn = _rms(x, g_ref[...]).astype(BF16)
    proj = functools.partial(_in_proj, xn, w5_ref, w2_ref, b_ref)

    hs = _lru_body(1, TS, proj(COL_U, D_LRU), cw_ref[...], cb_ref[...], wr_ref, br_ref[...],
                   wi_ref, bi_ref[...], lam_ref[...], ext_ref, h_s, convo_ref, ho_ref)

    def get_qkvo(h):
        off = h * D_HEAD
        q = (proj(COL_Q + off, D_HEAD) * (D_HEAD ** -0.5)).astype(BF16)
        return (q, proj(COL_K + off, D_HEAD).astype(BF16), proj(COL_V + off, D_HEAD).astype(BF16),
                proj(COL_O + off, D_HEAD))

    gt = jnp.dot(xn, wg_ref[...], preferred_element_type=F32) + bg_ref[...]
    hb = jnp.concatenate(_mlstm_body(1, TS, get_qkvo, gt, hg_ref, c_s, n_s, m_s), axis=1)
    x1_ref[...] = _merge_out(x, hs.astype(BF16), hb, proj(COL_GA, D_MODEL), proj(COL_GB, D_MODEL),
                             wa_ref, wb_ref, wo_ref)

    @pl.when(ti == pl.num_programs(1) - 1)
    def _():
        co_ref[...] = c_s[...]
        no_ref[...] = n_s[...]
        mo_ref[...] = m_s[...]


def _lru_weight_specs():
    nb = D_LRU // V7X_MXU_DIM
    return [_resident((LRU_CONV, D_LRU)), _resident((1, D_LRU)),
            _resident((nb, V7X_MXU_DIM, V7X_MXU_DIM)), _resident((1, D_LRU)),
            _resident((nb, V7X_MXU_DIM, V7X_MXU_DIM)), _resident((1, D_LRU)),
            _resident((1, D_LRU))]


def _lru_weights(P):
    return (P["lru_conv_w"], P["lru_conv_b"], P["w_r"], P["lru_b_r"], P["w_i"], P["lru_b_i"],
            P["lru_lambda"])


def _state_specs(NS, shared=False):
    def spec(*tail):
        zeros = (0,) * len(tail)
        index = (lambda s, t: (0,) + zeros) if shared else (lambda s, t: (s,) + zeros)
        return pl.BlockSpec((NS,) + tail, index)

    return [spec(LRU_CONV - 1, D_LRU), spec(1, D_LRU), spec(N_HEADS, D_HEAD, D_HEAD),
            spec(N_HEADS, D_HEAD), spec(1, V7X_LANES)]


def _state_shapes(NSEQ):
    return [jax.ShapeDtypeStruct((NSEQ, LRU_CONV - 1, D_LRU), F32),
            jax.ShapeDtypeStruct((NSEQ, 1, D_LRU), F32),
            jax.ShapeDtypeStruct((NSEQ, N_HEADS, D_HEAD, D_HEAD), F32),
            jax.ShapeDtypeStruct((NSEQ, N_HEADS, D_HEAD), F32),
            jax.ShapeDtypeStruct((NSEQ, 1, V7X_LANES), F32)]


def _mixer(x2, state, P, NSEQ, L, TS):
    NT = L // TS
    rows = pl.BlockSpec((TS, D_MODEL), lambda s, t: (s * NT + t, 0))
    wsq = _resident((D_MODEL, D_MODEL))
    return pl.pallas_call(
        functools.partial(_mixer_kernel, TS),
        grid=(NSEQ, NT),
        in_specs=[rows] + _state_specs(1, shared=True) + _in_proj_specs()
        + _lru_weight_specs() + [_resident((1, D_MODEL)), wsq, wsq, wsq],
        out_specs=[rows] + _state_specs(1),
        out_shape=[jax.ShapeDtypeStruct((NSEQ * L, D_MODEL), F32)] + _state_shapes(NSEQ),
        scratch_shapes=[pltpu.VMEM((1, D_LRU // V7X_LANES, V7X_SUBLANES + TS, V7X_LANES), F32),
                        pltpu.VMEM((1, 1, D_LRU), F32),
                        pltpu.VMEM((1, N_HEADS, D_HEAD, D_HEAD), F32),
                        pltpu.VMEM((1, N_HEADS, D_HEAD), F32),
                        pltpu.VMEM((1, 1, V7X_LANES), F32)],
        compiler_params=_params(2, 56),
        name="mixer",
    )(x2, *state, *_in_proj_weights(P), *_lru_weights(P), P["mlstm_head_g"], P["w_branch_a"],
      P["w_branch_b"], P["w_out"])


def _proj_kernel(x_ref, g_ref, w5_ref, w2_ref, b_ref, wg_ref, bg_ref,
                 u_ref, q_ref, k_ref, v_ref, o_ref, ga_ref, gb_ref, gt_ref):
    xn = _rms(x_ref[...], g_ref[...]).astype(BF16)
    proj = functools.partial(_in_proj, xn, w5_ref, w2_ref, b_ref)
    u_ref[...] = proj(COL_U, D_MODEL)
    q_ref[...] = (proj(COL_Q, D_MODEL) * (D_HEAD ** -0.5)).astype(BF16)
    k_ref[...] = proj(COL_K, D_MODEL).astype(BF16)
    v_ref[...] = proj(COL_V, D_MODEL).astype(BF16)
    o_ref[...] = proj(COL_O, D_MODEL)
    ga_ref[...] = proj(COL_GA, D_MODEL)
    gb_ref[...] = proj(COL_GB, D_MODEL)
    gt_ref[...] = jnp.dot(xn, wg_ref[...], preferred_element_type=F32) + bg_ref[...]


def _proj(x2, P, tm):
    M = x2.shape[0]
    row = lambda w: pl.BlockSpec((tm, w), lambda i: (i, 0))
    f32o = jax.ShapeDtypeStruct((M, D_MODEL), F32)
    bf16o = jax.ShapeDtypeStruct((M, D_MODEL), BF16)
    return pl.pallas_call(
        _proj_kernel,
        grid=(M // tm,),
        in_specs=[row(D_MODEL)] + _in_proj_specs(),
        out_specs=[row(D_MODEL)] * 7 + [row(GATE_W)],
        out_shape=[f32o, bf16o, bf16o, bf16o, f32o, f32o, f32o,
                   jax.ShapeDtypeStruct((M, GATE_W), F32)],
        compiler_params=_params(1, 48),
        name="proj",
    )(x2, *_in_proj_weights(P))


def _lru_kernel(NS, TS, u_ref, conv0_ref, h0_ref, cw_ref, cb_ref, wr_ref, br_ref, wi_ref,
                bi_ref, lam_ref, ha_ref, convo_ref, ho_ref, ext_ref, h_s):
    @pl.when(pl.program_id(1) == 0)
    def _():
        _conv_init(ext_ref, conv0_ref, LRU_CONV)
        h_s[...] = h0_ref[...]

    hs = _lru_body(NS, TS, u_ref[...], cw_ref[...], cb_ref[...], wr_ref, br_ref[...], wi_ref,
                   bi_ref[...], lam_ref[...], ext_ref, h_s, convo_ref, ho_ref)
    ha_ref[...] = hs.astype(BF16)


def _lru(u, conv0, h0, P, NSEQ, L, NS, TS):
    R = NS * TS
    NT = L // TS
    rows = pl.BlockSpec((R, D_LRU), lambda s, t: (s * NT + t, 0))
    st = _state_specs(NS)[:2]
    return pl.pallas_call(
        functools.partial(_lru_kernel, NS, TS),
        grid=(NSEQ // NS, NT),
        in_specs=[rows] + st + _lru_weight_specs(),
        out_specs=[rows] + st,
        out_shape=[jax.ShapeDtypeStruct((NSEQ * L, D_LRU), BF16)] + _state_shapes(NSEQ)[:2],
        scratch_shapes=[pltpu.VMEM((NS, D_LRU // V7X_LANES, V7X_SUBLANES + TS, V7X_LANES), F32),
                        pltpu.VMEM((NS, 1, D_LRU), F32)],
        compiler_params=_params(2, 48),
        name="lru",
    )(u, conv0, h0, *_lru_weights(P))


def _mlstm_kernel(NS, TS, q_ref, k_ref, v_ref, gt_ref, o_ref, hg_ref, c0_ref, n0_ref, m0_ref,
                  hb_ref, co_ref, no_ref, mo_ref, c_s, n_s, m_s):
    ti = pl.program_id(1)

    @pl.when(ti == 0)
    def _():
        c_s[...] = c0_ref[...]
        n_s[...] = n0_ref[...]
        m_s[...] = m0_ref[...]

    def get_qkvo(h):
        sl = slice(h * D_HEAD, (h + 1) * D_HEAD)
        return q_ref[:, sl], k_ref[:, sl], v_ref[:, sl], o_ref[:, sl]

    outs = _mlstm_body(NS, TS, get_qkvo, gt_ref[...], hg_ref, c_s, n_s, m_s)
    for h, out in enumerate(outs):
        hb_ref[:, h * D_HEAD:(h + 1) * D_HEAD] = out

    @pl.when(ti == pl.num_programs(1) - 1)
    def _():
        co_ref[...] = c_s[...]
        no_ref[...] = n_s[...]
        mo_ref[...] = m_s[...]


def _mlstm(q, k, v, gt, o, c0, n0, m0, P, NSEQ, L, NS, TS):
    R = NS * TS
    NT = L // TS
    rows = lambda w: pl.BlockSpec((R, w), lambda s, t: (s * NT + t, 0))
    st = _state_specs(NS)[2:]
    return pl.pallas_call(
        functools.partial(_mlstm_kernel, NS, TS),
        grid=(NSEQ // NS, NT),
        in_specs=[rows(D_MODEL), rows(D_MODEL), rows(D_MODEL), rows(GATE_W), rows(D_MODEL),
                  _resident((1, D_MODEL))] + st,
        out_specs=[rows(D_MODEL)] + st,
        out_shape=[jax.ShapeDtypeStruct((NSEQ * L, D_MODEL), BF16)] + _state_shapes(NSEQ)[2:],
        scratch_shapes=[pltpu.VMEM((NS, N_HEADS, D_HEAD, D_HEAD), F32),
                        pltpu.VMEM((NS, N_HEADS, D_HEAD), F32),
                        pltpu.VMEM((NS, 1, V7X_LANES), F32)],
        compiler_params=_params(2, 48),
        name="mlstm",
    )(q, k, v, gt, o, P["mlstm_head_g"], c0, n0, m0)


def _post_kernel(x_ref, ha_ref, hb_ref, ga_ref, gb_ref, wa_ref, wb_ref, wo_ref, x1_ref):
    x1_ref[...] = _merge_out(x_ref[...], ha_ref[...], hb_ref[...], ga_ref[...], gb_ref[...],
                             wa_ref, wb_ref, wo_ref)


def _post(x2, ha, hb, ga, gb, P, tm):
    M = x2.shape[0]
    row = pl.BlockSpec((tm, D_MODEL), lambda i: (i, 0))
    wsp = _resident((D_MODEL, D_MODEL))
    return pl.pallas_call(
        _post_kernel,
        grid=(M // tm,),
        in_specs=[row] * 5 + [wsp] * 3,
        out_specs=row,
        out_shape=jax.ShapeDtypeStruct((M, D_MODEL), F32),
        compiler_params=_params(1, 48),
        name="post",
    )(x2, ha, hb, ga, gb, P["w_branch_a"], P["w_branch_b"], P["w_out"])


def _ffn_kernel(NS, TS, x1_ref, st0_ref, g2_ref, wup_ref, cw_ref, cb_ref, wdn_ref, gf_ref,
                y_ref, sto_ref, ext_ref):
    R = NS * TS
    W = 2 * D_FF

    @pl.when(pl.program_id(1) == 0)
    def _():
        _conv_init(ext_ref, st0_ref, FFN_CONV)

    x1 = x1_ref[...]
    xn = _rms(x1, g2_ref[...]).astype(BF16)
    up = jnp.dot(xn, wup_ref[...], preferred_element_type=F32).reshape(NS, TS, W)
    upc = _causal_conv(ext_ref, up, cw_ref[...], cb_ref[...], sto_ref).reshape(R, W)
    act = (jax.nn.gelu(upc[:, D_FF:]) * upc[:, :D_FF]).astype(BF16)
    x2 = x1 + jnp.dot(act, wdn_ref[...], preferred_element_type=F32)
    y_ref[...] = _rms(x2, gf_ref[...])


def _ffn(x1, st0, P, NSEQ, L, NS, TS, shared=False):
    R = NS * TS
    NT = L // TS
    W = 2 * D_FF
    rows = pl.BlockSpec((R, D_MODEL), lambda s, t: (s * NT + t, 0))
    stspec = pl.BlockSpec((NS, FFN_CONV - 1, W), lambda s, t: (s, 0, 0))
    st0spec = pl.BlockSpec((NS, FFN_CONV - 1, W), lambda s, t: (0, 0, 0)) if shared else stspec
    return pl.pallas_call(
        functools.partial(_ffn_kernel, NS, TS),
        grid=(NSEQ // NS, NT),
        in_specs=[rows, st0spec, _resident((1, D_MODEL)), _resident((D_MODEL, W)),
                  _resident((FFN_CONV, W)), _resident((1, W)), _resident((D_FF, D_MODEL)),
                  _resident((1, D_MODEL))],
        out_specs=[rows, stspec],
        out_shape=[jax.ShapeDtypeStruct((NSEQ * L, D_MODEL), F32),
                   jax.ShapeDtypeStruct((NSEQ, FFN_CONV - 1, W), F32)],
        scratch_shapes=[pltpu.VMEM((NS, W // V7X_LANES, V7X_SUBLANES + TS, V7X_LANES), F32)],
        compiler_params=_params(2, 56),
        name="ffn",
    )(x1, st0, P["norm2_g"], P["w_up"], P["ffn_conv_w"], P["ffn_conv_b"], P["w_down"],
      P["final_g"])


def _block_diag(w):
    per = V7X_MXU_DIM // w.shape[1]
    nb = w.shape[0] // per
    bw = w.shape[1]
    out = jnp.zeros((nb, V7X_MXU_DIM, V7X_MXU_DIM), w.dtype)
    for p in range(per):
        out = out.at[:, p * bw:(p + 1) * bw, p * bw:(p + 1) * bw].set(w[p::per])
    return out


def _run_group_fused(x3, state, P, TS):
    NSEQ, L, _ = x3.shape
    x2 = x3.reshape(NSEQ * L, D_MODEL)
    x1, conv1, h1, c1, n1, m1 = _mixer(x2, state[:5], P, NSEQ, L, TS)
    y, ffn1 = _ffn(x1, state[5], P, NSEQ, L, 1, TS, shared=True)
    return y.reshape(NSEQ, L, D_MODEL), (conv1, h1, c1, n1, m1, ffn1)


def _run_group_staged(x3, state, P, tiles):
    NSEQ, L, _ = x3.shape
    conv0, h0, c0, n0, m0, ffn0 = state
    x2 = x3.reshape(NSEQ * L, D_MODEL)
    u, q, k, v, o, ga, gb, gt = _proj(x2, P, tiles["tm"])
    ha, conv1, h1 = _lru(u, conv0, h0, P, NSEQ, L, tiles["lru_ns"], tiles["ts"])
    hb, c1, n1, m1 = _mlstm(q, k, v, gt, o, c0, n0, m0, P, NSEQ, L, tiles["ml_ns"], tiles["ts"])
    x1 = _post(x2, ha, hb, ga, gb, P, tiles["tm"])
    y, ffn1 = _ffn(x1, ffn0, P, NSEQ, L, tiles["lru_ns"], tiles["ts"])
    return y.reshape(NSEQ, L, D_MODEL), (conv1, h1, c1, n1, m1, ffn1)


def kernel(x_prompt, x_sample, state_lru_conv, state_lru_h, state_mlstm_C, state_mlstm_n,
           state_mlstm_m, state_ffn_conv, meta_tokens, norm1_g, w_in, b_in, lru_conv_w,
           lru_conv_b, lru_w_r, lru_b_r, lru_w_i, lru_b_i, lru_lambda, mlstm_head_g,
           w_branch_a, w_branch_b, w_out, norm2_g, w_up, ffn_conv_w, ffn_conv_b, w_down, final_g):
    assert w_in.shape[0] == 1, "single-layer trunk"
    n_qkvo = D_LRU + 4 * D_MODEL
    w0, b0 = w_in[0], b_in[0]
    gate_cols = lambda a: (jnp.zeros(a.shape[:-1] + (GATE_W,), a.dtype)
                           .at[..., :N_HEADS].set(a[..., n_qkvo:n_qkvo + N_HEADS])
                           .at[..., V7X_LANES:V7X_LANES + N_HEADS]
                           .set(a[..., n_qkvo + N_HEADS:n_qkvo + 2 * N_HEADS]))
    row = lambda a: a.reshape(1, -1).astype(F32)
    P = {
        "norm1_g": row(norm1_g[0]),
        "w5": w0[:, :n_qkvo].astype(BF16),
        "w2": w0[:, n_qkvo + 2 * N_HEADS:].astype(BF16),
        "b_main": row(jnp.concatenate([b0[:n_qkvo], b0[n_qkvo + 2 * N_HEADS:]])),
        "w_gate": gate_cols(w0).astype(BF16),
        "b_gate": row(gate_cols(b0)),
        "lru_conv_w": lru_conv_w[0],
        "lru_conv_b": row(lru_conv_b[0]),
        "w_r": _block_diag(lru_w_r[0]).astype(BF16),
        "lru_b_r": row(lru_b_r[0]),
        "w_i": _block_diag(lru_w_i[0]).astype(BF16),
        "lru_b_i": row(lru_b_i[0]),
        "lru_lambda": row(lru_lambda[0]),
        "mlstm_head_g": row(mlstm_head_g[0]),
        "w_branch_a": w_branch_a[0].astype(BF16),
        "w_branch_b": w_branch_b[0].astype(BF16),
        "w_out": w_out[0].astype(BF16),
        "norm2_g": row(norm2_g[0]),
        "w_up": w_up[0].astype(BF16),
        "ffn_conv_w": ffn_conv_w[0],
        "ffn_conv_b": row(ffn_conv_b[0]),
        "w_down": w_down[0].astype(BF16),
        "final_g": row(final_g),
    }

    def pack_state(conv, h, c, n, m, ffn):
        nseq = h.shape[0]
        m_pad = jnp.zeros((nseq, 1, V7X_LANES), F32).at[:, 0, :N_HEADS].set(m.astype(F32))
        return (conv.astype(F32), h.astype(F32).reshape(nseq, 1, D_LRU), c.astype(F32),
                n.astype(F32), m_pad, ffn.astype(F32))

    def unpack_state(st):
        conv, h, c, n, m, ffn = st
        return (conv[None], h.reshape(1, -1, D_LRU), c[None], n[None],
                m[:, 0, :N_HEADS][None], ffn[None])

    zero = pack_state(jnp.zeros((1, LRU_CONV - 1, D_LRU), F32), jnp.zeros((1, D_LRU), F32),
                      jnp.zeros((1, N_HEADS, D_HEAD, D_HEAD), F32),
                      jnp.zeros((1, N_HEADS, D_HEAD), F32), jnp.zeros((1, N_HEADS), F32),
                      jnp.zeros((1, FFN_CONV - 1, 2 * D_FF), F32))
    _, meta_state = _run_group_fused(meta_tokens[None].astype(F32), zero, P, N_META)
    y_prompt, prompt_state = _run_group_fused(x_prompt, meta_state, P, 256)
    sample_state0 = pack_state(state_lru_conv[0], state_lru_h[0], state_mlstm_C[0],
                               state_mlstm_n[0], state_mlstm_m[0], state_ffn_conv[0])
    y_sample, sample_state = _run_group_staged(
        x_sample, sample_state0, P, dict(tm=256, ts=x_sample.shape[1], lru_ns=32, ml_ns=4))
    return (y_prompt, y_sample) + unpack_state(prompt_state) + unpack_state(sample_state)
```

```python
import functools
from typing import Any, NamedTuple

import jax
import jax.numpy as jnp
from jax import lax
from jax.experimental import pallas as pl
from jax.experimental.pallas import tpu as pltpu

F32 = jnp.float32
BF16 = jnp.bfloat16

D_MODEL = 1024
D_LRU = 1024
LRU_CONV = 4
LRU_C = 8.0
N_HEADS = 4
D_HEAD = 256
D_FF = 2816
FFN_CONV = 3
N_META = 16
EPS = 1e-6

V7X_LANES = 128
V7X_SUBLANES = 8
V7X_MXU_DIM = 256
NEG_BIG = -1e30

LONG_TS = 256
SHORT_ROWS = 256

N_MAIN = 7 * D_MODEL
N_W5 = 5 * D_MODEL
N_W2 = 2 * D_MODEL
GATE_W = 2 * V7X_LANES
COL_U, COL_Q, COL_K, COL_V, COL_O, COL_GA, COL_GB = (j * D_MODEL for j in range(7))


def _resident(shape):
    return pl.BlockSpec(shape, lambda *_: (0,) * len(shape), pipeline_mode=pl.Buffered(1))


def _params(n_grid, vmem_mb):
    return pltpu.CompilerParams(
        dimension_semantics=("arbitrary",) * n_grid,
        vmem_limit_bytes=vmem_mb * 1024 * 1024,
    )


def _rms(x, g):
    ms = jnp.mean(x * x, axis=-1, keepdims=True)
    return x * lax.rsqrt(ms + EPS) * g


def _in_proj(xn, w5_ref, w2_ref, b_ref, col, width):
    if col < N_W5:
        w = w5_ref[:, col:col + width]
    else:
        w = w2_ref[:, col - N_W5:col - N_W5 + width]
    return jnp.dot(xn, w, preferred_element_type=F32) + b_ref[:, col:col + width]


def _in_proj_specs():
    return [_resident((1, D_MODEL)), _resident((D_MODEL, N_W5)), _resident((D_MODEL, N_W2)),
            _resident((1, N_MAIN)), _resident((D_MODEL, GATE_W)), _resident((1, GATE_W))]


def _in_proj_weights(P):
    return (P["norm1_g"], P["w5"], P["w2"], P["b_main"], P["w_gate"], P["b_gate"])


def _conv_init(ext_ref, hist0_ref, taps):
    pad, hist = V7X_SUBLANES, taps - 1
    for g in range(ext_ref.shape[1]):
        ls = slice(g * V7X_LANES, (g + 1) * V7X_LANES)
        ext_ref[:, g, pad - hist:pad, :] = hist0_ref[:, :, ls]


def _causal_conv(ext_ref, x3, cw, cb, hist_out_ref):
    taps = cw.shape[0]
    ts = x3.shape[1]
    pad, hist = V7X_SUBLANES, taps - 1
    outs = []
    for g in range(ext_ref.shape[1]):
        ls = slice(g * V7X_LANES, (g + 1) * V7X_LANES)
        xg = x3[:, :, ls]
        ext_ref[:, g, pad:, :] = xg
        acc = cb[:, ls] + cw[taps - 1:taps, ls] * xg
        for j in range(hist):
            acc = acc + cw[j:j + 1, ls] * ext_ref[:, g, pad - hist + j:pad - hist + j + ts, :]
        outs.append(acc)
        new_hist = ext_ref[:, g, pad + ts - hist:pad + ts, :]
        ext_ref[:, g, pad - hist:pad, :] = new_hist
        hist_out_ref[:, :, ls] = new_hist
    return jnp.concatenate(outs, axis=-1)


def _lru_body(NS, TS, u2, cw, cb, wr_ref, br, wi_ref, bi, lam, ext_ref, h_ref, convo_ref, ho_ref):
    R = NS * TS
    C = D_LRU
    SB = V7X_SUBLANES
    uc2 = _causal_conv(ext_ref, u2.reshape(NS, TS, C), cw, cb, convo_ref).reshape(R, C)
    ucb = uc2.astype(BF16)

    def block_diag(w_ref):
        W = V7X_MXU_DIM
        return jnp.concatenate(
            [jnp.dot(ucb[:, g * W:(g + 1) * W], w_ref[g], preferred_element_type=F32)
             for g in range(C // W)], axis=1)

    r = jax.nn.sigmoid(block_diag(wr_ref) + br)
    i = jax.nn.sigmoid(block_diag(wi_ref) + bi)
    log_a = -LRU_C * r * jax.nn.softplus(-lam)
    a = jnp.exp(log_a)
    hh = jnp.sqrt(-jnp.tanh(log_a) * (a * a + 1.0)) * (i * uc2)

    a = a.reshape(R // SB, SB, C)
    hh = hh.reshape(R // SB, SB, C)
    sub = lax.broadcasted_iota(jnp.int32, (1, SB, C), 1)
    for d in (1, 2, 4):
        keep = sub >= d
        a_sh = pltpu.roll(a, d, axis=1)
        h_sh = pltpu.roll(hh, d, axis=1)
        hh = hh + a * jnp.where(keep, h_sh, 0.0)
        a = a * jnp.where(keep, a_sh, 1.0)

    nb = TS // SB
    a = a.reshape(NS, nb, SB, C)
    hh = hh.reshape(NS, nb, SB, C)
    h = jnp.broadcast_to(h_ref[...], (NS, SB, C))
    blocks = []
    for j in range(nb):
        hj = hh[:, j] + a[:, j] * h
        blocks.append(hj)
        h = jnp.broadcast_to(hj[:, SB - 1:, :], (NS, SB, C))
    h_ref[...] = h[:, 0:1, :]
    ho_ref[...] = h[:, 0:1, :]
    return jnp.concatenate(blocks, axis=1).reshape(R, C)


def _seg_scan(x, tpos, TS, op, ident):
    d = 1
    while d < TS:
        sh = pltpu.roll(x, d, axis=0)
        x = op(x, jnp.where(tpos >= d, sh, ident))
        d *= 2
    return x


def _pad_rows(x, rows):
    if x.shape[0] >= rows:
        return x
    return jnp.concatenate([x, jnp.zeros((rows - x.shape[0],) + x.shape[1:], x.dtype)], axis=0)


def _mlstm_body(NS, TS, get_qkvo, gt, hg_ref, st_in, st_out):
    R = NS * TS
    RC = max(R, V7X_LANES)
    LN = V7X_LANES
    shift = TS.bit_length() - 1

    ig4 = gt[:, :LN]
    lf4 = jax.nn.log_sigmoid(gt[:, LN:])
    tpos = lax.broadcasted_iota(jnp.int32, (R, LN), 0) & (TS - 1)
    b4 = _seg_scan(lf4, tpos, TS, jnp.add, 0.0)
    c4 = ig4 - b4
    cmax4 = _seg_scan(c4, tpos, TS, jnp.maximum, -jnp.inf)
    c_in, n_in, m_in = st_in
    c_out, n_out, m_out = st_out
    m_prev = [m_in[j] for j in range(NS)]
    m_rows = jnp.concatenate([jnp.broadcast_to(m, (TS, LN)) for m in m_prev], axis=0)
    big_m4 = jnp.maximum(cmax4, m_rows)
    e4 = jnp.exp(m_rows - big_m4)
    dinv4 = jnp.exp(-(b4 + big_m4))

    decay4, wk_parts = [], []
    for j in range(NS):
        b_last = b4[(j + 1) * TS - 1:(j + 1) * TS, :]
        g4 = b_last + c4[j * TS:(j + 1) * TS, :]
        mn = jnp.maximum(b_last + m_prev[j], jnp.max(g4, axis=0, keepdims=True))
        decay4.append(jnp.exp(b_last + m_prev[j] - mn))
        wk_parts.append(jnp.exp(g4 - mn))
        m_out[j] = mn
    wk4 = jnp.concatenate(wk_parts, axis=0)

    ri = lax.broadcasted_iota(jnp.int32, (R, RC), 0)
    ci = lax.broadcasted_iota(jnp.int32, (R, RC), 1)
    eye = ri == ci
    if NS == 1:
        causal = ci <= ri
    else:
        causal = (ci <= ri) & ((ri >> shift) == (ci >> shift))
    seq_of_row = lax.broadcasted_iota(jnp.int32, (R, D_HEAD), 0) >> shift

    outs = []
    for h in range(N_HEADS):
        sl = slice(h * D_HEAD, (h + 1) * D_HEAD)
        qh, kh, vh, oh = get_qkvo(h)
        kh_p = _pad_rows(kh, RC)
        vh_p = _pad_rows(vh, RC)
        c_c = c4[:, h:h + 1]
        big_m_c = big_m4[:, h:h + 1]
        e_c = e4[:, h:h + 1]
        dinv_c = dinv4[:, h:h + 1]
        wk_c = wk4[:, h:h + 1]

        qk = lax.dot_general(qh, kh_p, (((1,), (1,)), ((), ())), preferred_element_type=F32)
        c_r = jnp.sum(jnp.where(eye, c_c, 0.0), axis=0, keepdims=True)
        w = jnp.exp(jnp.where(causal, c_r - big_m_c, NEG_BIG))
        s = qk * w
        den = jnp.sum(s, axis=1, keepdims=True)
        num = jnp.dot(s.astype(BF16), vh_p, preferred_element_type=F32)

        qf = qh.astype(F32)
        if NS == 1:
            q_c = jnp.dot(qh, c_in[0, h].astype(BF16), preferred_element_type=F32)
            n_rows = n_in[0, h:h + 1, :]
        else:
            q_c = jnp.zeros((R, D_HEAD), F32)
            n_rows = jnp.zeros((R, D_HEAD), F32)
            for j in range(NS):
                mine = seq_of_row == j
                q_c = jnp.where(mine, jnp.dot(qh, c_in[j, h].astype(BF16),
                                              preferred_element_type=F32), q_c)
                n_rows = jnp.where(mine, n_in[j, h:h + 1, :], n_rows)
        q_n = jnp.sum(qf * n_rows, axis=1, keepdims=True)
        num = num + e_c * q_c
        den = den + e_c * q_n
        hh = num * (1.0 / jnp.maximum(jnp.abs(den), dinv_c))
        hh = hh * lax.rsqrt(jnp.mean(hh * hh, axis=1, keepdims=True) + EPS)
        outs.append(((hh * hg_ref[:, sl]) * jax.nn.sigmoid(oh)).astype(BF16))

        kw = kh.astype(F32) * wk_c
        for j in range(NS):
            kwj = kw if NS == 1 else jnp.where(seq_of_row == j, kw, 0.0)
            upd = lax.dot_general(_pad_rows(kwj, RC).astype(BF16), vh_p,
                                  (((0,), (0,)), ((), ())), preferred_element_type=F32)
            dec = decay4[j][:, h:h + 1]
            c_out[j, h] = dec * c_in[j, h] + upd
            n_out[j, h:h + 1, :] = (dec * n_in[j, h:h + 1, :]
                                    + jnp.sum(kwj, axis=0, keepdims=True))
    return outs


def _merge_out(x, ha, hb, ga, gb, wa_ref, wb_ref, wo_ref):
    pa = jnp.dot(ha, wa_ref[...], preferred_element_type=F32)
    pb = jnp.dot(hb, wb_ref[...], preferred_element_type=F32)
    merged = jax.nn.sigmoid(ga) * pa + jax.nn.sigmoid(gb) * pb
    return x + jnp.dot(merged.astype(BF16), wo_ref[...], preferred_element_type=F32)


class MixerIn(NamedTuple):
    x: Any
    conv0: Any
    h0: Any
    c0: Any
    n0: Any
    m0: Any
    g: Any
    w5: Any
    w2: Any
    b: Any
    wg: Any
    bg: Any
    cw: Any
    cb: Any
    wr: Any
    br: Any
    wi: Any
    bi: Any
    lam: Any
    hg: Any
    wa: Any
    wb: Any
    wo: Any


class MixerOut(NamedTuple):
    x1: Any
    conv: Any
    h: Any
    c: Any
    n: Any
    m: Any


class MixerScratch(NamedTuple):
    ext: Any
    h: Any
    c: Any
    n: Any
    m: Any


class SideIn(NamedTuple):
    q: Any
    k: Any
    v: Any
    gt: Any
    o: Any
    c0: Any
    n0: Any
    m0: Any


class SideOut(NamedTuple):
    hb: Any
    c: Any
    n: Any
    m: Any


def _split_refs(refs, *kinds):
    out, pos = [], 0
    for kind in kinds:
        n = len(kind._fields)
        out.append(kind(*refs[pos:pos + n]))
        pos += n
    assert pos == len(refs)
    return out


def _mixer_step(TS, i: MixerIn, o: MixerOut, s: MixerScratch, side_job=None):
    ti = pl.program_id(1)

    @pl.when(ti == 0)
    def _():
        _conv_init(s.ext, i.conv0, LRU_CONV)
        s.h[...] = i.h0[...]
        s.c[...] = i.c0[...]
        s.n[...] = i.n0[...]
        s.m[...] = i.m0[...]

    if side_job is not None:
        side_job()

    x = i.x[...]
    xn = _rms(x, i.g[...]).astype(BF16)
    proj = functools.partial(_in_proj, xn, i.w5, i.w2, i.b)

    hs = _lru_body(1, TS, proj(COL_U, D_LRU), i.cw[...], i.cb[...], i.wr, i.br[...],
                   i.wi, i.bi[...], i.lam[...], s.ext, s.h, o.conv, o.h)

    def get_qkvo(h):
        off = h * D_HEAD
        q = (proj(COL_Q + off, D_HEAD) * (D_HEAD ** -0.5)).astype(BF16)
        return (q, proj(COL_K + off, D_HEAD).astype(BF16), proj(COL_V + off, D_HEAD).astype(BF16),
                proj(COL_O + off, D_HEAD))

    gt = jnp.dot(xn, i.wg[...], preferred_element_type=F32) + i.bg[...]
    state = (s.c, s.n, s.m)
    hb = jnp.concatenate(_mlstm_body(1, TS, get_qkvo, gt, i.hg, state, state), axis=1)
    o.x1[...] = _merge_out(x, hs.astype(BF16), hb, proj(COL_GA, D_MODEL), proj(COL_GB, D_MODEL),
                           i.wa, i.wb, i.wo)

    @pl.when(ti == pl.num_programs(1) - 1)
    def _():
        o.c[...] = s.c[...]
        o.n[...] = s.n[...]
        o.m[...] = s.m[...]


def _mixer_kernel(TS, *refs):
    _mixer_step(TS, *_split_refs(refs, MixerIn, MixerOut, MixerScratch))


def _mixer_side_kernel(TS, SIDE_NS, SIDE_TS, *refs):
    i, si, o, so, s = _split_refs(refs, MixerIn, SideIn, MixerOut, SideOut, MixerScratch)

    def get_qkvo(h):
        sl = slice(h * D_HEAD, (h + 1) * D_HEAD)
        return si.q[:, sl], si.k[:, sl], si.v[:, sl], si.o[:, sl]

    def side_job():
        outs = _mlstm_body(SIDE_NS, SIDE_TS, get_qkvo, si.gt[...], i.hg,
                           (si.c0, si.n0, si.m0), (so.c, so.n, so.m))
        for h, out in enumerate(outs):
            so.hb[:, h * D_HEAD:(h + 1) * D_HEAD] = out

    _mixer_step(TS, i, o, s, side_job)


def _lru_weight_specs():
    nb = D_LRU // V7X_MXU_DIM
    return [_resident((LRU_CONV, D_LRU)), _resident((1, D_LRU)),
            _resident((nb, V7X_MXU_DIM, V7X_MXU_DIM)), _resident((1, D_LRU)),
            _resident((nb, V7X_MXU_DIM, V7X_MXU_DIM)), _resident((1, D_LRU)),
            _resident((1, D_LRU))]


def _lru_weights(P):
    return (P["lru_conv_w"], P["lru_conv_b"], P["w_r"], P["lru_b_r"], P["w_i"], P["lru_b_i"],
            P["lru_lambda"])


def _state_specs(NS, shared=False):
    def spec(*tail):
        zeros = (0,) * len(tail)
        index = (lambda s, t: (0,) + zeros) if shared else (lambda s, t: (s,) + zeros)
        return pl.BlockSpec((NS,) + tail, index)

    return [spec(LRU_CONV - 1, D_LRU), spec(1, D_LRU), spec(N_HEADS, D_HEAD, D_HEAD),
            spec(N_HEADS, D_HEAD), spec(1, V7X_LANES)]


def _state_shapes(NSEQ):
    return [jax.ShapeDtypeStruct((NSEQ, LRU_CONV - 1, D_LRU), F32),
            jax.ShapeDtypeStruct((NSEQ, 1, D_LRU), F32),
            jax.ShapeDtypeStruct((NSEQ, N_HEADS, D_HEAD, D_HEAD), F32),
            jax.ShapeDtypeStruct((NSEQ, N_HEADS, D_HEAD), F32),
            jax.ShapeDtypeStruct((NSEQ, 1, V7X_LANES), F32)]


def _mixer(x2, state, P, NSEQ, L, TS, side=None):
    NT = L // TS
    rows = pl.BlockSpec((TS, D_MODEL), lambda s, t: (s * NT + t, 0))
    wsq = _resident((D_MODEL, D_MODEL))
    in_specs = ([rows] + _state_specs(1, shared=True) + _in_proj_specs() + _lru_weight_specs()
                + [_resident((1, D_MODEL)), wsq, wsq, wsq])
    operands = (x2, *state, *_in_proj_weights(P), *_lru_weights(P), P["mlstm_head_g"],
                P["w_branch_a"], P["w_branch_b"], P["w_out"])
    out_specs = [rows] + _state_specs(1)
    out_shape = [jax.ShapeDtypeStruct((NSEQ * L, D_MODEL), F32)] + _state_shapes(NSEQ)
    body = functools.partial(_mixer_kernel, TS)
    if side is not None:
        side_in, side_len = side
        n_side = side_in.c0.shape[0]
        side_ns = n_side // (NSEQ * NT)
        assert side_ns * NSEQ * NT == n_side
        step = lambda s, t: s * NT + t
        srows = lambda w: pl.BlockSpec((side_ns * side_len, w), lambda s, t: (step(s, t), 0))
        sstate = [pl.BlockSpec((side_ns, N_HEADS, D_HEAD, D_HEAD), lambda s, t: (step(s, t), 0, 0, 0)),
                  pl.BlockSpec((side_ns, N_HEADS, D_HEAD), lambda s, t: (step(s, t), 0, 0)),
                  pl.BlockSpec((side_ns, 1, V7X_LANES), lambda s, t: (step(s, t), 0, 0))]
        in_specs = in_specs + [srows(D_MODEL)] * 3 + [srows(GATE_W), srows(D_MODEL)] + sstate
        operands = operands + tuple(side_in)
        out_specs = out_specs + [srows(D_MODEL)] + sstate
        out_shape = (out_shape + [jax.ShapeDtypeStruct((n_side * side_len, D_MODEL), BF16)]
                     + _state_shapes(n_side)[2:])
        body = functools.partial(_mixer_side_kernel, TS, side_ns, side_len)
    return pl.pallas_call(
        body,
        grid=(NSEQ, NT),
        in_specs=in_specs,
        out_specs=out_specs,
        out_shape=out_shape,
        scratch_shapes=[pltpu.VMEM((1, D_LRU // V7X_LANES, V7X_SUBLANES + TS, V7X_LANES), F32),
                        pltpu.VMEM((1, 1, D_LRU), F32),
                        pltpu.VMEM((1, N_HEADS, D_HEAD, D_HEAD), F32),
                        pltpu.VMEM((1, N_HEADS, D_HEAD), F32),
                        pltpu.VMEM((1, 1, V7X_LANES), F32)],
        compiler_params=_params(2, 56),
        name="mixer",
    )(*operands)


def _proj_kernel(x_ref, g_ref, w5_ref, w2_ref, b_ref, wg_ref, bg_ref,
                 u_ref, q_ref, k_ref, v_ref, o_ref, ga_ref, gb_ref, gt_ref):
    xn = _rms(x_ref[...], g_ref[...]).astype(BF16)
    proj = functools.partial(_in_proj, xn, w5_ref, w2_ref, b_ref)
    u_ref[...] = proj(COL_U, D_MODEL)
    q_ref[...] = (proj(COL_Q, D_MODEL) * (D_HEAD ** -0.5)).astype(BF16)
    k_ref[...] = proj(COL_K, D_MODEL).astype(BF16)
    v_ref[...] = proj(COL_V, D_MODEL).astype(BF16)
    o_ref[...] = proj(COL_O, D_MODEL)
    ga_ref[...] = proj(COL_GA, D_MODEL)
    gb_ref[...] = proj(COL_GB, D_MODEL)
    gt_ref[...] = jnp.dot(xn, wg_ref[...], preferred_element_type=F32) + bg_ref[...]


def _proj(x2, P, tm):
    M = x2.shape[0]
    row = lambda w: pl.BlockSpec((tm, w), lambda i: (i, 0))
    f32o = jax.ShapeDtypeStruct((M, D_MODEL), F32)
    bf16o = jax.ShapeDtypeStruct((M, D_MODEL), BF16)
    return pl.pallas_call(
        _proj_kernel,
        grid=(M // tm,),
        in_specs=[row(D_MODEL)] + _in_proj_specs(),
        out_specs=[row(D_MODEL)] * 7 + [row(GATE_W)],
        out_shape=[f32o, bf16o, bf16o, bf16o, f32o, f32o, f32o,
                   jax.ShapeDtypeStruct((M, GATE_W), F32)],
        compiler_params=_params(1, 48),
        name="proj",
    )(x2, *_in_proj_weights(P))


def _lru_kernel(NS, TS, u_ref, conv0_ref, h0_ref, cw_ref, cb_ref, wr_ref, br_ref, wi_ref,
                bi_ref, lam_ref, ha_ref, convo_ref, ho_ref, ext_ref, h_s):
    @pl.when(pl.program_id(1) == 0)
    def _():
        _conv_init(ext_ref, conv0_ref, LRU_CONV)
        h_s[...] = h0_ref[...]

    hs = _lru_body(NS, TS, u_ref[...], cw_ref[...], cb_ref[...], wr_ref, br_ref[...], wi_ref,
                   bi_ref[...], lam_ref[...], ext_ref, h_s, convo_ref, ho_ref)
    ha_ref[...] = hs.astype(BF16)


def _lru(u, conv0, h0, P, NSEQ, L, NS, TS):
    R = NS * TS
    NT = L // TS
    rows = pl.BlockSpec((R, D_LRU), lambda s, t: (s * NT + t, 0))
    st = _state_specs(NS)[:2]
    return pl.pallas_call(
        functools.partial(_lru_kernel, NS, TS),
        grid=(NSEQ // NS, NT),
        in_specs=[rows] + st + _lru_weight_specs(),
        out_specs=[rows] + st,
        out_shape=[jax.ShapeDtypeStruct((NSEQ * L, D_LRU), BF16)] + _state_shapes(NSEQ)[:2],
        scratch_shapes=[pltpu.VMEM((NS, D_LRU // V7X_LANES, V7X_SUBLANES + TS, V7X_LANES), F32),
                        pltpu.VMEM((NS, 1, D_LRU), F32)],
        compiler_params=_params(2, 48),
        name="lru",
    )(u, conv0, h0, *_lru_weights(P))


def _post_kernel(x_ref, ha_ref, hb_ref, ga_ref, gb_ref, wa_ref, wb_ref, wo_ref, x1_ref):
    x1_ref[...] = _merge_out(x_ref[...], ha_ref[...], hb_ref[...], ga_ref[...], gb_ref[...],
                             wa_ref, wb_ref, wo_ref)


def _post(x2, ha, hb, ga, gb, P, tm):
    M = x2.shape[0]
    row = pl.BlockSpec((tm, D_MODEL), lambda i: (i, 0))
    wsp = _resident((D_MODEL, D_MODEL))
    return pl.pallas_call(
        _post_kernel,
        grid=(M // tm,),
        in_specs=[row] * 5 + [wsp] * 3,
        out_specs=row,
        out_shape=jax.ShapeDtypeStruct((M, D_MODEL), F32),
        compiler_params=_params(1, 48),
        name="post",
    )(x2, ha, hb, ga, gb, P["w_branch_a"], P["w_branch_b"], P["w_out"])


def _ffn_kernel(NS, TS, x1_ref, st0_ref, g2_ref, wup_ref, cw_ref, cb_ref, wdn_ref, gf_ref,
                y_ref, sto_ref, ext_ref):
    R = NS * TS
    W = 2 * D_FF

    @pl.when(pl.program_id(1) == 0)
    def _():
        _conv_init(ext_ref, st0_ref, FFN_CONV)

    x1 = x1_ref[...]
    xn = _rms(x1, g2_ref[...]).astype(BF16)
    up = jnp.dot(xn, wup_ref[...], preferred_element_type=F32).reshape(NS, TS, W)
    upc = _causal_conv(ext_ref, up, cw_ref[...], cb_ref[...], sto_ref).reshape(R, W)
    act = (jax.nn.gelu(upc[:, D_FF:]) * upc[:, :D_FF]).astype(BF16)
    x2 = x1 + jnp.dot(act, wdn_ref[...], preferred_element_type=F32)
    y_ref[...] = _rms(x2, gf_ref[...])


def _ffn(x1, st0, P, NSEQ, L, NS, TS, shared=False):
    R = NS * TS
    NT = L // TS
    W = 2 * D_FF
    rows = pl.BlockSpec((R, D_MODEL), lambda s, t: (s * NT + t, 0))
    stspec = pl.BlockSpec((NS, FFN_CONV - 1, W), lambda s, t: (s, 0, 0))
    st0spec = pl.BlockSpec((NS, FFN_CONV - 1, W), lambda s, t: (0, 0, 0)) if shared else stspec
    return pl.pallas_call(
        functools.partial(_ffn_kernel, NS, TS),
        grid=(NSEQ // NS, NT),
        in_specs=[rows, st0spec, _resident((1, D_MODEL)), _resident((D_MODEL, W)),
                  _resident((FFN_CONV, W)), _resident((1, W)), _resident((D_FF, D_MODEL)),
                  _resident((1, D_MODEL))],
        out_specs=[rows, stspec],
        out_shape=[jax.ShapeDtypeStruct((NSEQ * L, D_MODEL), F32),
                   jax.ShapeDtypeStruct((NSEQ, FFN_CONV - 1, W), F32)],
        scratch_shapes=[pltpu.VMEM((NS, W // V7X_LANES, V7X_SUBLANES + TS, V7X_LANES), F32)],
        compiler_params=_params(2, 56),
        name="ffn",
    )(x1, st0, P["norm2_g"], P["w_up"], P["ffn_conv_w"], P["ffn_conv_b"], P["w_down"],
      P["final_g"])


def _block_diag(w):
    per = V7X_MXU_DIM // w.shape[1]
    nb = w.shape[0] // per
    bw = w.shape[1]
    out = jnp.zeros((nb, V7X_MXU_DIM, V7X_MXU_DIM), w.dtype)
    for p in range(per):
        out = out.at[:, p * bw:(p + 1) * bw, p * bw:(p + 1) * bw].set(w[p::per])
    return out


def _run_long_group(x3, state, P, TS, side=None):
    NSEQ, L, _ = x3.shape
    x2 = x3.reshape(NSEQ * L, D_MODEL)
    x1, conv1, h1, c1, n1, m1, *side_out = _mixer(x2, state[:5], P, NSEQ, L, TS, side)
    y, ffn1 = _ffn(x1, state[5], P, NSEQ, L, 1, TS, shared=True)
    return y.reshape(NSEQ, L, D_MODEL), (conv1, h1, c1, n1, m1, ffn1), side_out


def _short_group_front(x2, state, P, NSEQ, L, tm, ns):
    conv0, h0, c0, n0, m0, _ = state
    u, q, k, v, o, ga, gb, gt = _proj(x2, P, tm)
    ha, conv1, h1 = _lru(u, conv0, h0, P, NSEQ, L, ns, L)
    return SideIn(q, k, v, gt, o, c0, n0, m0), (ha, ga, gb, conv1, h1)


def _short_group_back(x2, front, side_out, ffn0, P, NSEQ, L, tm, ns):
    ha, ga, gb, conv1, h1 = front
    hb, c1, n1, m1 = side_out
    x1 = _post(x2, ha, hb, ga, gb, P, tm)
    y, ffn1 = _ffn(x1, ffn0, P, NSEQ, L, ns, L)
    return y.reshape(NSEQ, L, D_MODEL), (conv1, h1, c1, n1, m1, ffn1)


def kernel(x_prompt, x_sample, state_lru_conv, state_lru_h, state_mlstm_C, state_mlstm_n,
           state_mlstm_m, state_ffn_conv, meta_tokens, norm1_g, w_in, b_in, lru_conv_w,
           lru_conv_b, lru_w_r, lru_b_r, lru_w_i, lru_b_i, lru_lambda, mlstm_head_g,
           w_branch_a, w_branch_b, w_out, norm2_g, w_up, ffn_conv_w, ffn_conv_b, w_down, final_g):
    assert w_in.shape[0] == 1, "single-layer trunk"
    n_qkvo = D_LRU + 4 * D_MODEL
    w0, b0 = w_in[0], b_in[0]
    gate_cols = lambda a: (jnp.zeros(a.shape[:-1] + (GATE_W,), a.dtype)
                           .at[..., :N_HEADS].set(a[..., n_qkvo:n_qkvo + N_HEADS])
                           .at[..., V7X_LANES:V7X_LANES + N_HEADS]
                           .set(a[..., n_qkvo + N_HEADS:n_qkvo + 2 * N_HEADS]))
    row = lambda a: a.reshape(1, -1).astype(F32)
    P = {
        "norm1_g": row(norm1_g[0]),
        "w5": w0[:, :n_qkvo].astype(BF16),
        "w2": w0[:, n_qkvo + 2 * N_HEADS:].astype(BF16),
        "b_main": row(jnp.concatenate([b0[:n_qkvo], b0[n_qkvo + 2 * N_HEADS:]])),
        "w_gate": gate_cols(w0).astype(BF16),
        "b_gate": row(gate_cols(b0)),
        "lru_conv_w": lru_conv_w[0],
        "lru_conv_b": row(lru_conv_b[0]),
        "w_r": _block_diag(lru_w_r[0]).astype(BF16),
        "lru_b_r": row(lru_b_r[0]),
        "w_i": _block_diag(lru_w_i[0]).astype(BF16),
        "lru_b_i": row(lru_b_i[0]),
        "lru_lambda": row(lru_lambda[0]),
        "mlstm_head_g": row(mlstm_head_g[0]),
        "w_branch_a": w_branch_a[0].astype(BF16),
        "w_branch_b": w_branch_b[0].astype(BF16),
        "w_out": w_out[0].astype(BF16),
        "norm2_g": row(norm2_g[0]),
        "w_up": w_up[0].astype(BF16),
        "ffn_conv_w": ffn_conv_w[0],
        "ffn_conv_b": row(ffn_conv_b[0]),
        "w_down": w_down[0].astype(BF16),
        "final_g": row(final_g),
    }

    def pack_state(conv, h, c, n, m, ffn):
        nseq = h.shape[0]
        m_pad = jnp.zeros((nseq, 1, V7X_LANES), F32).at[:, 0, :N_HEADS].set(m.astype(F32))
        return (conv.astype(F32), h.astype(F32).reshape(nseq, 1, D_LRU), c.astype(F32),
                n.astype(F32), m_pad, ffn.astype(F32))

    def unpack_state(st):
        conv, h, c, n, m, ffn = st
        return (conv[None], h.reshape(1, -1, D_LRU), c[None], n[None],
                m[:, 0, :N_HEADS][None], ffn[None])

    zero = pack_state(jnp.zeros((1, LRU_CONV - 1, D_LRU), F32), jnp.zeros((1, D_LRU), F32),
                      jnp.zeros((1, N_HEADS, D_HEAD, D_HEAD), F32),
                      jnp.zeros((1, N_HEADS, D_HEAD), F32), jnp.zeros((1, N_HEADS), F32),
                      jnp.zeros((1, FFN_CONV - 1, 2 * D_FF), F32))
    _, meta_state, _ = _run_long_group(meta_tokens[None].astype(F32), zero, P, N_META)
    sample_state0 = pack_state(state_lru_conv[0], state_lru_h[0], state_mlstm_C[0],
                               state_mlstm_n[0], state_mlstm_m[0], state_ffn_conv[0])
    n_sample, l_sample, _ = x_sample.shape
    xs2 = x_sample.reshape(n_sample * l_sample, D_MODEL)
    short = dict(NSEQ=n_sample, L=l_sample, tm=SHORT_ROWS, ns=SHORT_ROWS // l_sample)
    side_in, front = _short_group_front(xs2, sample_state0, P, **short)
    y_prompt, prompt_state, side_out = _run_long_group(x_prompt, meta_state, P, LONG_TS,
                                                       side=(side_in, l_sample))
    y_sample, sample_state = _short_group_back(xs2, front, side_out, sample_state0[5], P, **short)
    return (y_prompt, y_sample) + unpack_state(prompt_state) + unpack_state(sample_state)
```

```python
import functools
from typing import Any, NamedTuple

import jax
import jax.numpy as jnp
from jax import lax
from jax.experimental import pallas as pl
from jax.experimental.pallas import tpu as pltpu

F32 = jnp.float32
BF16 = jnp.bfloat16

D_MODEL = 1024
D_LRU = 1024
LRU_CONV = 4
LRU_C = 8.0
N_HEADS = 4
D_HEAD = 256
D_FF = 2816
FFN_CONV = 3
N_META = 16
EPS = 1e-6

V7X_LANES = 128
V7X_SUBLANES = 8
V7X_MXU_DIM = 256
NEG_BIG = -1e30

LONG_TS = 256
SHORT_ROWS = 256

N_MAIN = 7 * D_MODEL
N_W5 = 5 * D_MODEL
N_W2 = 2 * D_MODEL
GATE_W = 2 * V7X_LANES
COL_U, COL_Q, COL_K, COL_V, COL_O, COL_GA, COL_GB = (j * D_MODEL for j in range(7))


def _resident(shape):
    return pl.BlockSpec(shape, lambda *_: (0,) * len(shape), pipeline_mode=pl.Buffered(1))


def _params(n_grid, vmem_mb, flags=None):
    return pltpu.CompilerParams(
        dimension_semantics=("arbitrary",) * n_grid,
        vmem_limit_bytes=vmem_mb * 1024 * 1024,
        flags=flags,
    )


def _rms(x, g):
    ms = jnp.mean(x * x, axis=-1, keepdims=True)
    return x * lax.rsqrt(ms + EPS) * g


def _in_proj(xn, w5_ref, w2_ref, b_ref, col, width):
    if col < N_W5:
        w = w5_ref[:, col:col + width]
    else:
        w = w2_ref[:, col - N_W5:col - N_W5 + width]
    return jnp.dot(xn, w, preferred_element_type=F32) + b_ref[:, col:col + width]


def _in_proj_specs():
    return [_resident((1, D_MODEL)), _resident((D_MODEL, N_W5)), _resident((D_MODEL, N_W2)),
            _resident((1, N_MAIN)), _resident((D_MODEL, GATE_W)), _resident((1, GATE_W))]


def _in_proj_weights(P):
    return (P["norm1_g"], P["w5"], P["w2"], P["b_main"], P["w_gate"], P["b_gate"])


def _conv_init(ext_ref, hist0_ref, taps):
    pad, hist = V7X_SUBLANES, taps - 1
    for g in range(ext_ref.shape[1]):
        ls = slice(g * V7X_LANES, (g + 1) * V7X_LANES)
        ext_ref[:, g, pad - hist:pad, :] = hist0_ref[:, :, ls]


def _causal_conv(ext_ref, x3, cw, cb, hist_out_ref):
    taps = cw.shape[0]
    ts = x3.shape[1]
    pad, hist = V7X_SUBLANES, taps - 1
    outs = []
    for g in range(ext_ref.shape[1]):
        ls = slice(g * V7X_LANES, (g + 1) * V7X_LANES)
        xg = x3[:, :, ls]
        ext_ref[:, g, pad:, :] = xg
        acc = cb[:, ls] + cw[taps - 1:taps, ls] * xg
        for j in range(hist):
            acc = acc + cw[j:j + 1, ls] * ext_ref[:, g, pad - hist + j:pad - hist + j + ts, :]
        outs.append(acc)
        new_hist = ext_ref[:, g, pad + ts - hist:pad + ts, :]
        ext_ref[:, g, pad - hist:pad, :] = new_hist
        hist_out_ref[:, :, ls] = new_hist
    return jnp.concatenate(outs, axis=-1)


def _lru_body(NS, TS, u2, cw, cb, wr_ref, br, wi_ref, bi, lam, ext_ref, h_ref, convo_ref, ho_ref):
    R = NS * TS
    C = D_LRU
    SB = V7X_SUBLANES
    uc2 = _causal_conv(ext_ref, u2.reshape(NS, TS, C), cw, cb, convo_ref).reshape(R, C)
    ucb = uc2.astype(BF16)

    def block_diag(w_ref):
        W = V7X_MXU_DIM
        return jnp.concatenate(
            [jnp.dot(ucb[:, g * W:(g + 1) * W], w_ref[g], preferred_element_type=F32)
             for g in range(C // W)], axis=1)

    r = jax.nn.sigmoid(block_diag(wr_ref) + br)
    i = jax.nn.sigmoid(block_diag(wi_ref) + bi)
    log_a = -LRU_C * r * jax.nn.softplus(-lam)
    a = jnp.exp(log_a)
    hh = jnp.sqrt(-jnp.tanh(log_a) * (a * a + 1.0)) * (i * uc2)

    a = a.reshape(R // SB, SB, C)
    hh = hh.reshape(R // SB, SB, C)
    sub = lax.broadcasted_iota(jnp.int32, (1, SB, C), 1)
    for d in (1, 2, 4):
        keep = sub >= d
        a_sh = pltpu.roll(a, d, axis=1)
        h_sh = pltpu.roll(hh, d, axis=1)
        hh = hh + a * jnp.where(keep, h_sh, 0.0)
        a = a * jnp.where(keep, a_sh, 1.0)

    nb = TS // SB
    a = a.reshape(NS, nb, SB, C)
    hh = hh.reshape(NS, nb, SB, C)
    h = jnp.broadcast_to(h_ref[...], (NS, SB, C))
    blocks = []
    for j in range(nb):
        hj = hh[:, j] + a[:, j] * h
        blocks.append(hj)
        h = jnp.broadcast_to(hj[:, SB - 1:, :], (NS, SB, C))
    h_ref[...] = h[:, 0:1, :]
    ho_ref[...] = h[:, 0:1, :]
    return jnp.concatenate(blocks, axis=1).reshape(R, C)


def _seg_scan(x, tpos, TS, op, ident):
    d = 1
    while d < TS:
        sh = pltpu.roll(x, d, axis=0)
        x = op(x, jnp.where(tpos >= d, sh, ident))
        d *= 2
    return x


def _pad_rows(x, rows):
    if x.shape[0] >= rows:
        return x
    return jnp.concatenate([x, jnp.zeros((rows - x.shape[0],) + x.shape[1:], x.dtype)], axis=0)


def _mlstm_body(NS, TS, get_qkvo, gt, hg_ref, st_in, st_out):
    R = NS * TS
    RC = max(R, V7X_LANES)
    LN = V7X_LANES
    shift = TS.bit_length() - 1

    ig4 = gt[:, :LN]
    lf4 = jax.nn.log_sigmoid(gt[:, LN:])
    tpos = lax.broadcasted_iota(jnp.int32, (R, LN), 0) & (TS - 1)
    b4 = _seg_scan(lf4, tpos, TS, jnp.add, 0.0)
    c4 = ig4 - b4
    cmax4 = _seg_scan(c4, tpos, TS, jnp.maximum, -jnp.inf)
    c_in, n_in, m_in = st_in
    c_out, n_out, m_out = st_out
    m_prev = [m_in[j] for j in range(NS)]
    m_rows = jnp.concatenate([jnp.broadcast_to(m, (TS, LN)) for m in m_prev], axis=0)
    big_m4 = jnp.maximum(cmax4, m_rows)
    e4 = jnp.exp(m_rows - big_m4)
    dinv4 = jnp.exp(-(b4 + big_m4))

    decay4, wk_parts = [], []
    for j in range(NS):
        b_last = b4[(j + 1) * TS - 1:(j + 1) * TS, :]
        g4 = b_last + c4[j * TS:(j + 1) * TS, :]
        mn = jnp.maximum(b_last + m_prev[j], jnp.max(g4, axis=0, keepdims=True))
        decay4.append(jnp.exp(b_last + m_prev[j] - mn))
        wk_parts.append(jnp.exp(g4 - mn))
        m_out[j] = mn
    wk4 = jnp.concatenate(wk_parts, axis=0)

    ri = lax.broadcasted_iota(jnp.int32, (R, RC), 0)
    ci = lax.broadcasted_iota(jnp.int32, (R, RC), 1)
    eye = ri == ci
    if NS == 1:
        causal = ci <= ri
    else:
        causal = (ci <= ri) & ((ri >> shift) == (ci >> shift))
    seq_of_row = lax.broadcasted_iota(jnp.int32, (R, D_HEAD), 0) >> shift

    outs = []
    for h in range(N_HEADS):
        sl = slice(h * D_HEAD, (h + 1) * D_HEAD)
        qh, kh, vh, oh = get_qkvo(h)
        kh_p = _pad_rows(kh, RC)
        vh_p = _pad_rows(vh, RC)
        c_c = c4[:, h:h + 1]
        big_m_c = big_m4[:, h:h + 1]
        e_c = e4[:, h:h + 1]
        dinv_c = dinv4[:, h:h + 1]
        wk_c = wk4[:, h:h + 1]

        qk = lax.dot_general(qh, kh_p, (((1,), (1,)), ((), ())), preferred_element_type=F32)
        c_r = jnp.sum(jnp.where(eye, c_c, 0.0), axis=0, keepdims=True)
        w = jnp.exp(jnp.where(causal, c_r - big_m_c, NEG_BIG))
        s = qk * w
        den = jnp.sum(s, axis=1, keepdims=True)
        num = jnp.dot(s.astype(BF16), vh_p, preferred_element_type=F32)

        qf = qh.astype(F32)
        if NS == 1:
            q_c = jnp.dot(qh, c_in[0, h].astype(BF16), preferred_element_type=F32)
            n_rows = n_in[0, h:h + 1, :]
        else:
            q_c = jnp.zeros((R, D_HEAD), F32)
            n_rows = jnp.zeros((R, D_HEAD), F32)
            for j in range(NS):
                mine = seq_of_row == j
                q_c = jnp.where(mine, jnp.dot(qh, c_in[j, h].astype(BF16),
                                              preferred_element_type=F32), q_c)
                n_rows = jnp.where(mine, n_in[j, h:h + 1, :], n_rows)
        q_n = jnp.sum(qf * n_rows, axis=1, keepdims=True)
        num = num + e_c * q_c
        den = den + e_c * q_n
        hh = num * (1.0 / jnp.maximum(jnp.abs(den), dinv_c))
        hh = hh * lax.rsqrt(jnp.mean(hh * hh, axis=1, keepdims=True) + EPS)
        outs.append(((hh * hg_ref[:, sl]) * jax.nn.sigmoid(oh)).astype(BF16))

        kw = kh.astype(F32) * wk_c
        for j in range(NS):
            kwj = kw if NS == 1 else jnp.where(seq_of_row == j, kw, 0.0)
            upd = lax.dot_general(_pad_rows(kwj, RC).astype(BF16), vh_p,
                                  (((0,), (0,)), ((), ())), preferred_element_type=F32)
            dec = decay4[j][:, h:h + 1]
            c_out[j, h] = dec * c_in[j, h] + upd
            n_out[j, h:h + 1, :] = (dec * n_in[j, h:h + 1, :]
                                    + jnp.sum(kwj, axis=0, keepdims=True))
    return outs


def _merge_out(x, ha, hb, ga, gb, wa_ref, wb_ref, wo_ref):
    pa = jnp.dot(ha, wa_ref[...], preferred_element_type=F32)
    pb = jnp.dot(hb, wb_ref[...], preferred_element_type=F32)
    merged = jax.nn.sigmoid(ga) * pa + jax.nn.sigmoid(gb) * pb
    return x + jnp.dot(merged.astype(BF16), wo_ref[...], preferred_element_type=F32)


class MixerIn(NamedTuple):
    x: Any
    conv0: Any
    h0: Any
    c0: Any
    n0: Any
    m0: Any
    g: Any
    w5: Any
    w2: Any
    b: Any
    wg: Any
    bg: Any
    cw: Any
    cb: Any
    wr: Any
    br: Any
    wi: Any
    bi: Any
    lam: Any
    hg: Any
    wa: Any
    wb: Any
    wo: Any


class MixerOut(NamedTuple):
    x1: Any
    conv: Any
    h: Any
    c: Any
    n: Any
    m: Any


class MixerScratch(NamedTuple):
    ext: Any
    h: Any
    c: Any
    n: Any
    m: Any


class SideIn(NamedTuple):
    q: Any
    k: Any
    v: Any
    gt: Any
    o: Any
    c0: Any
    n0: Any
    m0: Any


class SideOut(NamedTuple):
    hb: Any
    c: Any
    n: Any
    m: Any


def _split_refs(refs, *kinds):
    out, pos = [], 0
    for kind in kinds:
        n = len(kind._fields)
        out.append(kind(*refs[pos:pos + n]))
        pos += n
    assert pos == len(refs)
    return out


def _mixer_step(TS, i: MixerIn, o: MixerOut, s: MixerScratch, side_job=None):
    ti = pl.program_id(1)

    @pl.when(ti == 0)
    def _():
        _conv_init(s.ext, i.conv0, LRU_CONV)
        s.h[...] = i.h0[...]
        s.c[...] = i.c0[...]
        s.n[...] = i.n0[...]
        s.m[...] = i.m0[...]

    if side_job is not None:
        side_job()

    x = i.x[...]
    xn = _rms(x, i.g[...]).astype(BF16)
    proj = functools.partial(_in_proj, xn, i.w5, i.w2, i.b)

    hs = _lru_body(1, TS, proj(COL_U, D_LRU), i.cw[...], i.cb[...], i.wr, i.br[...],
                   i.wi, i.bi[...], i.lam[...], s.ext, s.h, o.conv, o.h)

    def get_qkvo(h):
        off = h * D_HEAD
        q = (proj(COL_Q + off, D_HEAD) * (D_HEAD ** -0.5)).astype(BF16)
        return (q, proj(COL_K + off, D_HEAD).astype(BF16), proj(COL_V + off, D_HEAD).astype(BF16),
                proj(COL_O + off, D_HEAD))

    gt = jnp.dot(xn, i.wg[...], preferred_element_type=F32) + i.bg[...]
    state = (s.c, s.n, s.m)
    hb = jnp.concatenate(_mlstm_body(1, TS, get_qkvo, gt, i.hg, state, state), axis=1)
    o.x1[...] = _merge_out(x, hs.astype(BF16), hb, proj(COL_GA, D_MODEL), proj(COL_GB, D_MODEL),
                           i.wa, i.wb, i.wo)

    @pl.when(ti == pl.num_programs(1) - 1)
    def _():
        o.c[...] = s.c[...]
        o.n[...] = s.n[...]
        o.m[...] = s.m[...]


def _mixer_kernel(TS, *refs):
    _mixer_step(TS, *_split_refs(refs, MixerIn, MixerOut, MixerScratch))


def _mixer_side_kernel(TS, SIDE_NS, SIDE_TS, *refs):
    i, si, o, so, s = _split_refs(refs, MixerIn, SideIn, MixerOut, SideOut, MixerScratch)

    def get_qkvo(h):
        sl = slice(h * D_HEAD, (h + 1) * D_HEAD)
        return si.q[:, sl], si.k[:, sl], si.v[:, sl], si.o[:, sl]

    def side_job():
        outs = _mlstm_body(SIDE_NS, SIDE_TS, get_qkvo, si.gt[...], i.hg,
                           (si.c0, si.n0, si.m0), (so.c, so.n, so.m))
        for h, out in enumerate(outs):
            so.hb[:, h * D_HEAD:(h + 1) * D_HEAD] = out

    _mixer_step(TS, i, o, s, side_job)


def _lru_weight_specs():
    nb = D_LRU // V7X_MXU_DIM
    return [_resident((LRU_CONV, D_LRU)), _resident((1, D_LRU)),
            _resident((nb, V7X_MXU_DIM, V7X_MXU_DIM)), _resident((1, D_LRU)),
            _resident((nb, V7X_MXU_DIM, V7X_MXU_DIM)), _resident((1, D_LRU)),
            _resident((1, D_LRU))]


def _lru_weights(P):
    return (P["lru_conv_w"], P["lru_conv_b"], P["w_r"], P["lru_b_r"], P["w_i"], P["lru_b_i"],
            P["lru_lambda"])


def _state_specs(NS, shared=False):
    def spec(*tail):
        zeros = (0,) * len(tail)
        index = (lambda s, t: (0,) + zeros) if shared else (lambda s, t: (s,) + zeros)
        return pl.BlockSpec((NS,) + tail, index)

    return [spec(LRU_CONV - 1, D_LRU), spec(1, D_LRU), spec(N_HEADS, D_HEAD, D_HEAD),
            spec(N_HEADS, D_HEAD), spec(1, V7X_LANES)]


def _state_shapes(NSEQ):
    return [jax.ShapeDtypeStruct((NSEQ, LRU_CONV - 1, D_LRU), F32),
            jax.ShapeDtypeStruct((NSEQ, 1, D_LRU), F32),
            jax.ShapeDtypeStruct((NSEQ, N_HEADS, D_HEAD, D_HEAD), F32),
            jax.ShapeDtypeStruct((NSEQ, N_HEADS, D_HEAD), F32),
            jax.ShapeDtypeStruct((NSEQ, 1, V7X_LANES), F32)]


def _mixer(x2, state, P, NSEQ, L, TS, side=None):
    NT = L // TS
    rows = pl.BlockSpec((TS, D_MODEL), lambda s, t: (s * NT + t, 0))
    wsq = _resident((D_MODEL, D_MODEL))
    in_specs = ([rows] + _state_specs(1, shared=True) + _in_proj_specs() + _lru_weight_specs()
                + [_resident((1, D_MODEL)), wsq, wsq, wsq])
    operands = (x2, *state, *_in_proj_weights(P), *_lru_weights(P), P["mlstm_head_g"],
                P["w_branch_a"], P["w_branch_b"], P["w_out"])
    out_specs = [rows] + _state_specs(1)
    out_shape = [jax.ShapeDtypeStruct((NSEQ * L, D_MODEL), F32)] + _state_shapes(NSEQ)
    body = functools.partial(_mixer_kernel, TS)
    if side is not None:
        side_in, side_len = side
        n_side = side_in.c0.shape[0]
        side_ns = n_side // (NSEQ * NT)
        assert side_ns * NSEQ * NT == n_side
        step = lambda s, t: s * NT + t
        srows = lambda w: pl.BlockSpec((side_ns * side_len, w), lambda s, t: (step(s, t), 0))
        sstate = [pl.BlockSpec((side_ns, N_HEADS, D_HEAD, D_HEAD), lambda s, t: (step(s, t), 0, 0, 0)),
                  pl.BlockSpec((side_ns, N_HEADS, D_HEAD), lambda s, t: (step(s, t), 0, 0)),
                  pl.BlockSpec((side_ns, 1, V7X_LANES), lambda s, t: (step(s, t), 0, 0))]
        in_specs = in_specs + [srows(D_MODEL)] * 3 + [srows(GATE_W), srows(D_MODEL)] + sstate
        operands = operands + tuple(side_in)
        out_specs = out_specs + [srows(D_MODEL)] + sstate
        out_shape = (out_shape + [jax.ShapeDtypeStruct((n_side * side_len, D_MODEL), BF16)]
                     + _state_shapes(n_side)[2:])
        body = functools.partial(_mixer_side_kernel, TS, side_ns, side_len)
    return pl.pallas_call(
        body,
        grid=(NSEQ, NT),
        in_specs=in_specs,
        out_specs=out_specs,
        out_shape=out_shape,
        scratch_shapes=[pltpu.VMEM((1, D_LRU // V7X_LANES, V7X_SUBLANES + TS, V7X_LANES), F32),
                        pltpu.VMEM((1, 1, D_LRU), F32),
                        pltpu.VMEM((1, N_HEADS, D_HEAD, D_HEAD), F32),
                        pltpu.VMEM((1, N_HEADS, D_HEAD), F32),
                        pltpu.VMEM((1, 1, V7X_LANES), F32)],
        compiler_params=_params(2, 56),
        name="mixer",
    )(*operands)


def _proj_kernel(x_ref, g_ref, w5_ref, w2_ref, b_ref, wg_ref, bg_ref,
                 u_ref, q_ref, k_ref, v_ref, o_ref, ga_ref, gb_ref, gt_ref):
    xn = _rms(x_ref[...], g_ref[...]).astype(BF16)
    proj = functools.partial(_in_proj, xn, w5_ref, w2_ref, b_ref)
    u_ref[...] = proj(COL_U, D_MODEL)
    q_ref[...] = (proj(COL_Q, D_MODEL) * (D_HEAD ** -0.5)).astype(BF16)
    k_ref[...] = proj(COL_K, D_MODEL).astype(BF16)
    v_ref[...] = proj(COL_V, D_MODEL).astype(BF16)
    o_ref[...] = proj(COL_O, D_MODEL)
    ga_ref[...] = proj(COL_GA, D_MODEL)
    gb_ref[...] = proj(COL_GB, D_MODEL)
    gt_ref[...] = jnp.dot(xn, wg_ref[...], preferred_element_type=F32) + bg_ref[...]


def _proj(x2, P, tm):
    M = x2.shape[0]
    row = lambda w: pl.BlockSpec((tm, w), lambda i: (i, 0))
    f32o = jax.ShapeDtypeStruct((M, D_MODEL), F32)
    bf16o = jax.ShapeDtypeStruct((M, D_MODEL), BF16)
    return pl.pallas_call(
        _proj_kernel,
        grid=(M // tm,),
        in_specs=[row(D_MODEL)] + _in_proj_specs(),
        out_specs=[row(D_MODEL)] * 7 + [row(GATE_W)],
        out_shape=[f32o, bf16o, bf16o, bf16o, f32o, f32o, f32o,
                   jax.ShapeDtypeStruct((M, GATE_W), F32)],
        compiler_params=_params(1, 48),
        name="proj",
    )(x2, *_in_proj_weights(P))


def _lru_kernel(NS, TS, u_ref, conv0_ref, h0_ref, cw_ref, cb_ref, wr_ref, br_ref, wi_ref,
                bi_ref, lam_ref, ha_ref, convo_ref, ho_ref, ext_ref, h_s):
    @pl.when(pl.program_id(1) == 0)
    def _():
        _conv_init(ext_ref, conv0_ref, LRU_CONV)
        h_s[...] = h0_ref[...]

    hs = _lru_body(NS, TS, u_ref[...], cw_ref[...], cb_ref[...], wr_ref, br_ref[...], wi_ref,
                   bi_ref[...], lam_ref[...], ext_ref, h_s, convo_ref, ho_ref)
    ha_ref[...] = hs.astype(BF16)


def _lru(u, conv0, h0, P, NSEQ, L, NS, TS):
    R = NS * TS
    NT = L // TS
    rows = pl.BlockSpec((R, D_LRU), lambda s, t: (s * NT + t, 0))
    st = _state_specs(NS)[:2]
    return pl.pallas_call(
        functools.partial(_lru_kernel, NS, TS),
        grid=(NSEQ // NS, NT),
        in_specs=[rows] + st + _lru_weight_specs(),
        out_specs=[rows] + st,
        out_shape=[jax.ShapeDtypeStruct((NSEQ * L, D_LRU), BF16)] + _state_shapes(NSEQ)[:2],
        scratch_shapes=[pltpu.VMEM((NS, D_LRU // V7X_LANES, V7X_SUBLANES + TS, V7X_LANES), F32),
                        pltpu.VMEM((NS, 1, D_LRU), F32)],
        compiler_params=_params(2, 48),
        name="lru",
    )(u, conv0, h0, *_lru_weights(P))


def _post_kernel(x_ref, ha_ref, hb_ref, ga_ref, gb_ref, wa_ref, wb_ref, wo_ref, x1_ref):
    x1_ref[...] = _merge_out(x_ref[...], ha_ref[...], hb_ref[...], ga_ref[...], gb_ref[...],
                             wa_ref, wb_ref, wo_ref)


def _post(x2, ha, hb, ga, gb, P, tm):
    M = x2.shape[0]
    row = pl.BlockSpec((tm, D_MODEL), lambda i: (i, 0))
    wsp = _resident((D_MODEL, D_MODEL))
    return pl.pallas_call(
        _post_kernel,
        grid=(M // tm,),
        in_specs=[row] * 5 + [wsp] * 3,
        out_specs=row,
        out_shape=jax.ShapeDtypeStruct((M, D_MODEL), F32),
        compiler_params=_params(1, 48),
        name="post",
    )(x2, ha, hb, ga, gb, P["w_branch_a"], P["w_branch_b"], P["w_out"])


def _ffn_kernel(NS, TS, x1_ref, st0_ref, g2_ref, wup_ref, cw_ref, cb_ref, wdn_ref, gf_ref,
                y_ref, sto_ref, ext_ref):
    R = NS * TS
    W = 2 * D_FF

    @pl.when(pl.program_id(1) == 0)
    def _():
        _conv_init(ext_ref, st0_ref, FFN_CONV)

    x1 = x1_ref[...]
    xn = _rms(x1, g2_ref[...]).astype(BF16)
    up = jnp.dot(xn, wup_ref[...], preferred_element_type=F32).reshape(NS, TS, W)
    upc = _causal_conv(ext_ref, up, cw_ref[...], cb_ref[...], sto_ref).reshape(R, W)
    act = (jax.nn.gelu(upc[:, D_FF:]) * upc[:, :D_FF]).astype(BF16)
    x2 = x1 + jnp.dot(act, wdn_ref[...], preferred_element_type=F32)
    y_ref[...] = _rms(x2, gf_ref[...])


def _ffn(x1, st0, P, NSEQ, L, NS, TS, shared=False):
    R = NS * TS
    NT = L // TS
    W = 2 * D_FF
    rows = pl.BlockSpec((R, D_MODEL), lambda s, t: (s * NT + t, 0))
    stspec = pl.BlockSpec((NS, FFN_CONV - 1, W), lambda s, t: (s, 0, 0))
    st0spec = pl.BlockSpec((NS, FFN_CONV - 1, W), lambda s, t: (0, 0, 0)) if shared else stspec
    return pl.pallas_call(
        functools.partial(_ffn_kernel, NS, TS),
        grid=(NSEQ // NS, NT),
        in_specs=[rows, st0spec, _resident((1, D_MODEL)), _resident((D_MODEL, W)),
                  _resident((FFN_CONV, W)), _resident((1, W)), _resident((D_FF, D_MODEL)),
                  _resident((1, D_MODEL))],
        out_specs=[rows, stspec],
        out_shape=[jax.ShapeDtypeStruct((NSEQ * L, D_MODEL), F32),
                   jax.ShapeDtypeStruct((NSEQ, FFN_CONV - 1, W), F32)],
        scratch_shapes=[pltpu.VMEM((NS, W // V7X_LANES, V7X_SUBLANES + TS, V7X_LANES), F32)],
        compiler_params=_params(2, 56),
        name="ffn",
    )(x1, st0, P["norm2_g"], P["w_up"], P["ffn_conv_w"], P["ffn_conv_b"], P["w_down"],
      P["final_g"])


def _block_diag(w):
    bw = w.shape[1]
    per = V7X_MXU_DIM // bw
    nb = w.shape[0] // per
    w4 = w.reshape(nb, per, bw, 1, bw)
    on_diag = jnp.eye(per, dtype=w.dtype).reshape(1, per, 1, per, 1)
    return (w4 * on_diag).reshape(nb, V7X_MXU_DIM, V7X_MXU_DIM)


def _split_w_in_kernel(w_ref, w5_ref, w2_ref, wg_ref):
    w = w_ref[0]
    w5_ref[...] = w[:, :N_W5].astype(BF16)
    w2_ref[...] = w[:, N_W5 + 2 * N_HEADS:].astype(BF16)
    t = w[:, N_W5:N_W5 + V7X_LANES]
    head_lane = lax.broadcasted_iota(jnp.int32, t.shape, 1) < N_HEADS
    ig = jnp.where(head_lane, t, 0.0)
    fg = jnp.where(head_lane, pltpu.roll(t, V7X_LANES - N_HEADS, axis=1), 0.0)
    wg_ref[...] = jnp.concatenate([ig, fg], axis=1).astype(BF16)


def _split_w_in(w_in):
    n_in = w_in.shape[2]
    assert n_in == N_W5 + 2 * N_HEADS + N_W2
    tr = V7X_LANES
    blk = lambda w: pl.BlockSpec((tr, w), lambda i: (i, 0))
    return pl.pallas_call(
        _split_w_in_kernel,
        grid=(D_MODEL // tr,),
        in_specs=[pl.BlockSpec((1, tr, n_in), lambda i: (0, i, 0))],
        out_specs=[blk(N_W5), blk(N_W2), blk(GATE_W)],
        out_shape=[jax.ShapeDtypeStruct((D_MODEL, N_W5), BF16),
                   jax.ShapeDtypeStruct((D_MODEL, N_W2), BF16),
                   jax.ShapeDtypeStruct((D_MODEL, GATE_W), BF16)],
        compiler_params=_params(1, 32),
        name="split_w_in",
    )(w_in)


def _run_long_group(x3, state, P, TS, side=None):
    NSEQ, L, _ = x3.shape
    x2 = x3.reshape(NSEQ * L, D_MODEL)
    x1, conv1, h1, c1, n1, m1, *side_out = _mixer(x2, state[:5], P, NSEQ, L, TS, side)
    y, ffn1 = _ffn(x1, state[5], P, NSEQ, L, 1, TS, shared=True)
    return y.reshape(NSEQ, L, D_MODEL), (conv1, h1, c1, n1, m1, ffn1), side_out


def _short_group_front(x2, state, P, NSEQ, L, tm, ns):
    conv0, h0, c0, n0, m0, _ = state
    u, q, k, v, o, ga, gb, gt = _proj(x2, P, tm)
    ha, conv1, h1 = _lru(u, conv0, h0, P, NSEQ, L, ns, L)
    return SideIn(q, k, v, gt, o, c0, n0, m0), (ha, ga, gb, conv1, h1)


def _short_group_back(x2, front, side_out, ffn0, P, NSEQ, L, tm, ns):
    ha, ga, gb, conv1, h1 = front
    hb, c1, n1, m1 = side_out
    x1 = _post(x2, ha, hb, ga, gb, P, tm)
    y, ffn1 = _ffn(x1, ffn0, P, NSEQ, L, ns, L)
    return y.reshape(NSEQ, L, D_MODEL), (conv1, h1, c1, n1, m1, ffn1)


def kernel(x_prompt, x_sample, state_lru_conv, state_lru_h, state_mlstm_C, state_mlstm_n,
           state_mlstm_m, state_ffn_conv, meta_tokens, norm1_g, w_in, b_in, lru_conv_w,
           lru_conv_b, lru_w_r, lru_b_r, lru_w_i, lru_b_i, lru_lambda, mlstm_head_g,
           w_branch_a, w_branch_b, w_out, norm2_g, w_up, ffn_conv_w, ffn_conv_b, w_down, final_g):
    assert w_in.shape[0] == 1, "single-layer trunk"
    b0 = b_in[0]
    gate_pad = jnp.zeros((V7X_LANES - N_HEADS,), b0.dtype)
    row = lambda a: a.reshape(1, -1).astype(F32)
    w5, w2, w_gate = _split_w_in(w_in)
    P = {
        "norm1_g": row(norm1_g[0]),
        "w5": w5,
        "w2": w2,
        "b_main": row(jnp.concatenate([b0[:N_W5], b0[N_W5 + 2 * N_HEADS:]])),
        "w_gate": w_gate,
        "b_gate": row(jnp.concatenate([b0[N_W5:N_W5 + N_HEADS], gate_pad,
                                       b0[N_W5 + N_HEADS:N_W5 + 2 * N_HEADS], gate_pad])),
        "lru_conv_w": lru_conv_w[0],
        "lru_conv_b": row(lru_conv_b[0]),
        "w_r": _block_diag(lru_w_r[0]).astype(BF16),
        "lru_b_r": row(lru_b_r[0]),
        "w_i": _block_diag(lru_w_i[0]).astype(BF16),
        "lru_b_i": row(lru_b_i[0]),
        "lru_lambda": row(lru_lambda[0]),
        "mlstm_head_g": row(mlstm_head_g[0]),
        "w_branch_a": w_branch_a[0].astype(BF16),
        "w_branch_b": w_branch_b[0].astype(BF16),
        "w_out": w_out[0].astype(BF16),
        "norm2_g": row(norm2_g[0]),
        "w_up": w_up[0].astype(BF16),
        "ffn_conv_w": ffn_conv_w[0],
        "ffn_conv_b": row(ffn_conv_b[0]),
        "w_down": w_down[0].astype(BF16),
        "final_g": row(final_g),
    }

    def pack_state(conv, h, c, n, m, ffn):
        nseq = h.shape[0]
        m_pad = jnp.pad(m.astype(F32)[:, None, :], ((0, 0), (0, 0), (0, V7X_LANES - N_HEADS)))
        return (conv.astype(F32), h.astype(F32).reshape(nseq, 1, D_LRU), c.astype(F32),
                n.astype(F32), m_pad, ffn.astype(F32))

    def unpack_state(st):
        conv, h, c, n, m, ffn = st
        return (conv[None], h.reshape(1, -1, D_LRU), c[None], n[None],
                m[:, 0, :N_HEADS][None], ffn[None])

    zero = pack_state(jnp.zeros((1, LRU_CONV - 1, D_LRU), F32), jnp.zeros((1, D_LRU), F32),
                      jnp.zeros((1, N_HEADS, D_HEAD, D_HEAD), F32),
                      jnp.zeros((1, N_HEADS, D_HEAD), F32), jnp.zeros((1, N_HEADS), F32),
                      jnp.zeros((1, FFN_CONV - 1, 2 * D_FF), F32))
    _, meta_state, _ = _run_long_group(meta_tokens[None].astype(F32), zero, P, N_META)
    sample_state0 = pack_state(state_lru_conv[0], state_lru_h[0], state_mlstm_C[0],
                               state_mlstm_n[0], state_mlstm_m[0], state_ffn_conv[0])
    n_sample, l_sample, _ = x_sample.shape
    xs2 = x_sample.reshape(n_sample * l_sample, D_MODEL)
    short = dict(NSEQ=n_sample, L=l_sample, tm=SHORT_ROWS, ns=SHORT_ROWS // l_sample)
    side_in, front = _short_group_front(xs2, sample_state0, P, **short)
    y_prompt, prompt_state, side_out = _run_long_group(x_prompt, meta_state, P, LONG_TS,
                                                       side=(side_in, l_sample))
    y_sample, sample_state = _short_group_back(xs2, front, side_out, sample_state0[5], P, **short)
    return (y_prompt, y_sample) + unpack_state(prompt_state) + unpack_state(sample_state)
```

```python
import functools
from typing import Any, NamedTuple

import jax
import jax.numpy as jnp
from jax import lax
from jax.experimental import pallas as pl
from jax.experimental.pallas import tpu as pltpu

F32 = jnp.float32
BF16 = jnp.bfloat16

D_MODEL = 1024
D_LRU = 1024
LRU_CONV = 4
LRU_C = 8.0
N_HEADS = 4
D_HEAD = 256
D_FF = 2816
FFN_CONV = 3
N_META = 16
EPS = 1e-6

V7X_LANES = 128
V7X_SUBLANES = 8
V7X_MXU_DIM = 256
NEG_BIG = -1e30

LONG_TS = 256
SHORT_ROWS = 256

N_MAIN = 7 * D_MODEL
N_W5 = 5 * D_MODEL
N_W2 = 2 * D_MODEL
W_PITCH_PAD = V7X_LANES
GATE_W = 2 * V7X_LANES
COL_U, COL_Q, COL_K, COL_V, COL_O, COL_GA, COL_GB = (j * D_MODEL for j in range(7))


def _resident(shape):
    return pl.BlockSpec(shape, lambda *_: (0,) * len(shape), pipeline_mode=pl.Buffered(1))


def _params(n_grid, vmem_mb, flags=None):
    return pltpu.CompilerParams(
        dimension_semantics=("arbitrary",) * n_grid,
        vmem_limit_bytes=vmem_mb * 1024 * 1024,
        flags=flags,
    )


def _rms(x, g):
    ms = jnp.mean(x * x, axis=-1, keepdims=True)
    return x * lax.rsqrt(ms + EPS) * g


def _in_proj(xn, w5_ref, w2_ref, b_ref, col, width):
    if col < N_W5:
        w = w5_ref[:, col:col + width]
    else:
        w = w2_ref[:, col - N_W5:col - N_W5 + width]
    return jnp.dot(xn, w, preferred_element_type=F32) + b_ref[:, col:col + width]


def _in_proj_specs():
    return [_resident((1, D_MODEL)), _resident((D_MODEL, N_W5 + W_PITCH_PAD)),
            _resident((D_MODEL, N_W2 + W_PITCH_PAD)),
            _resident((1, N_MAIN)), _resident((D_MODEL, GATE_W)), _resident((1, GATE_W))]


def _in_proj_weights(P):
    return (P["norm1_g"], P["w5"], P["w2"], P["b_main"], P["w_gate"], P["b_gate"])


def _conv_init(ext_ref, hist0_ref, taps):
    pad, hist = V7X_SUBLANES, taps - 1
    for g in range(ext_ref.shape[1]):
        ls = slice(g * V7X_LANES, (g + 1) * V7X_LANES)
        ext_ref[:, g, pad - hist:pad, :] = hist0_ref[:, :, ls]


def _causal_conv(ext_ref, x3, cw, cb, hist_out_ref):
    taps = cw.shape[0]
    ts = x3.shape[1]
    pad, hist = V7X_SUBLANES, taps - 1
    outs = []
    for g in range(ext_ref.shape[1]):
        ls = slice(g * V7X_LANES, (g + 1) * V7X_LANES)
        xg = x3[:, :, ls]
        ext_ref[:, g, pad:, :] = xg
        acc = cb[:, ls] + cw[taps - 1:taps, ls] * xg
        for j in range(hist):
            acc = acc + cw[j:j + 1, ls] * ext_ref[:, g, pad - hist + j:pad - hist + j + ts, :]
        outs.append(acc)
        new_hist = ext_ref[:, g, pad + ts - hist:pad + ts, :]
        ext_ref[:, g, pad - hist:pad, :] = new_hist
        hist_out_ref[:, :, ls] = new_hist
    return jnp.concatenate(outs, axis=-1)


def _lru_body(NS, TS, u2, cw, cb, wr_ref, br, wi_ref, bi, lam, ext_ref, h_ref, convo_ref, ho_ref):
    R = NS * TS
    C = D_LRU
    SB = V7X_SUBLANES
    uc2 = _causal_conv(ext_ref, u2.reshape(NS, TS, C), cw, cb, convo_ref).reshape(R, C)
    ucb = uc2.astype(BF16)

    def block_diag(w_ref):
        W = V7X_MXU_DIM
        return jnp.concatenate(
            [jnp.dot(ucb[:, g * W:(g + 1) * W], w_ref[g], preferred_element_type=F32)
             for g in range(C // W)], axis=1)

    r = jax.nn.sigmoid(block_diag(wr_ref) + br)
    i = jax.nn.sigmoid(block_diag(wi_ref) + bi)
    log_a = -LRU_C * r * jax.nn.softplus(-lam)
    a = jnp.exp(log_a)
    hh = jnp.sqrt(-jnp.tanh(log_a) * (a * a + 1.0)) * (i * uc2)

    a = a.reshape(R // SB, SB, C)
    hh = hh.reshape(R // SB, SB, C)
    sub = lax.broadcasted_iota(jnp.int32, (1, SB, C), 1)
    for d in (1, 2, 4):
        keep = sub >= d
        a_sh = pltpu.roll(a, d, axis=1)
        h_sh = pltpu.roll(hh, d, axis=1)
        hh = hh + a * jnp.where(keep, h_sh, 0.0)
        a = a * jnp.where(keep, a_sh, 1.0)

    nb = TS // SB
    a = a.reshape(NS, nb, SB, C)
    hh = hh.reshape(NS, nb, SB, C)
    h = jnp.broadcast_to(h_ref[...], (NS, SB, C))
    blocks = []
    for j in range(nb):
        hj = hh[:, j] + a[:, j] * h
        blocks.append(hj)
        h = jnp.broadcast_to(hj[:, SB - 1:, :], (NS, SB, C))
    h_ref[...] = h[:, 0:1, :]
    ho_ref[...] = h[:, 0:1, :]
    return jnp.concatenate(blocks, axis=1).reshape(R, C)


def _seg_scan(x, tpos, TS, op, ident):
    d = 1
    while d < TS:
        sh = pltpu.roll(x, d, axis=0)
        x = op(x, jnp.where(tpos >= d, sh, ident))
        d *= 2
    return x


def _pad_rows(x, rows):
    if x.shape[0] >= rows:
        return x
    return jnp.concatenate([x, jnp.zeros((rows - x.shape[0],) + x.shape[1:], x.dtype)], axis=0)


def _mlstm_body(NS, TS, get_qkvo, gt, hg_ref, st_in, st_out):
    R = NS * TS
    RC = max(R, V7X_LANES)
    LN = V7X_LANES
    shift = TS.bit_length() - 1

    ig4 = gt[:, :LN]
    lf4 = jax.nn.log_sigmoid(gt[:, LN:])
    tpos = lax.broadcasted_iota(jnp.int32, (R, LN), 0) & (TS - 1)
    b4 = _seg_scan(lf4, tpos, TS, jnp.add, 0.0)
    c4 = ig4 - b4
    cmax4 = _seg_scan(c4, tpos, TS, jnp.maximum, -jnp.inf)
    c_in, n_in, m_in = st_in
    c_out, n_out, m_out = st_out
    m_prev = [m_in[j] for j in range(NS)]
    m_rows = jnp.concatenate([jnp.broadcast_to(m, (TS, LN)) for m in m_prev], axis=0)
    big_m4 = jnp.maximum(cmax4, m_rows)
    e4 = jnp.exp(m_rows - big_m4)
    dinv4 = jnp.exp(-(b4 + big_m4))

    decay4, wk_parts = [], []
    for j in range(NS):
        b_last = b4[(j + 1) * TS - 1:(j + 1) * TS, :]
        g4 = b_last + c4[j * TS:(j + 1) * TS, :]
        mn = jnp.maximum(b_last + m_prev[j], jnp.max(g4, axis=0, keepdims=True))
        decay4.append(jnp.exp(b_last + m_prev[j] - mn))
        wk_parts.append(jnp.exp(g4 - mn))
        m_out[j] = mn
    wk4 = jnp.concatenate(wk_parts, axis=0)

    ri = lax.broadcasted_iota(jnp.int32, (R, RC), 0)
    ci = lax.broadcasted_iota(jnp.int32, (R, RC), 1)
    eye = ri == ci
    if NS == 1:
        causal = ci <= ri
    else:
        causal = (ci <= ri) & ((ri >> shift) == (ci >> shift))
    seq_of_row = lax.broadcasted_iota(jnp.int32, (R, D_HEAD), 0) >> shift

    outs = []
    for h in range(N_HEADS):
        sl = slice(h * D_HEAD, (h + 1) * D_HEAD)
        qh, kh, vh, oh = get_qkvo(h)
        kh_p = _pad_rows(kh, RC)
        vh_p = _pad_rows(vh, RC)
        c_c = c4[:, h:h + 1]
        big_m_c = big_m4[:, h:h + 1]
        e_c = e4[:, h:h + 1]
        dinv_c = dinv4[:, h:h + 1]
        wk_c = wk4[:, h:h + 1]

        qk = lax.dot_general(qh, kh_p, (((1,), (1,)), ((), ())), preferred_element_type=F32)
        c_r = jnp.sum(jnp.where(eye, c_c, 0.0), axis=0, keepdims=True)
        w = jnp.exp(jnp.where(causal, c_r - big_m_c, NEG_BIG))
        s = qk * w
        den = jnp.sum(s, axis=1, keepdims=True)
        num = jnp.dot(s.astype(BF16), vh_p, preferred_element_type=F32)

        qf = qh.astype(F32)
        if NS == 1:
            q_c = jnp.dot(qh, c_in[0, h].astype(BF16), preferred_element_type=F32)
            n_rows = n_in[0, h:h + 1, :]
        else:
            q_c = jnp.zeros((R, D_HEAD), F32)
            n_rows = jnp.zeros((R, D_HEAD), F32)
            for j in range(NS):
                mine = seq_of_row == j
                q_c = jnp.where(mine, jnp.dot(qh, c_in[j, h].astype(BF16),
                                              preferred_element_type=F32), q_c)
                n_rows = jnp.where(mine, n_in[j, h:h + 1, :], n_rows)
        q_n = jnp.sum(qf * n_rows, axis=1, keepdims=True)
        num = num + e_c * q_c
        den = den + e_c * q_n
        hh = num * (1.0 / jnp.maximum(jnp.abs(den), dinv_c))
        hh = hh * lax.rsqrt(jnp.mean(hh * hh, axis=1, keepdims=True) + EPS)
        outs.append(((hh * hg_ref[:, sl]) * jax.nn.sigmoid(oh)).astype(BF16))

        kw = kh.astype(F32) * wk_c
        for j in range(NS):
            kwj = kw if NS == 1 else jnp.where(seq_of_row == j, kw, 0.0)
            upd = lax.dot_general(_pad_rows(kwj, RC).astype(BF16), vh_p,
                                  (((0,), (0,)), ((), ())), preferred_element_type=F32)
            dec = decay4[j][:, h:h + 1]
            c_out[j, h] = dec * c_in[j, h] + upd
            n_out[j, h:h + 1, :] = (dec * n_in[j, h:h + 1, :]
                                    + jnp.sum(kwj, axis=0, keepdims=True))
    return outs


def _merge_out(x, ha, hb, ga, gb, wa_ref, wb_ref, wo_ref):
    pa = jnp.dot(ha, wa_ref[:, :D_MODEL], preferred_element_type=F32)
    pb = jnp.dot(hb, wb_ref[:, :D_MODEL], preferred_element_type=F32)
    merged = jax.nn.sigmoid(ga) * pa + jax.nn.sigmoid(gb) * pb
    return x + jnp.dot(merged.astype(BF16), wo_ref[:, :D_MODEL], preferred_element_type=F32)


class MixerIn(NamedTuple):
    x: Any
    conv0: Any
    h0: Any
    c0: Any
    n0: Any
    m0: Any
    g: Any
    w5: Any
    w2: Any
    b: Any
    wg: Any
    bg: Any
    cw: Any
    cb: Any
    wr: Any
    br: Any
    wi: Any
    bi: Any
    lam: Any
    hg: Any
    wa: Any
    wb: Any
    wo: Any


class MixerOut(NamedTuple):
    x1: Any
    conv: Any
    h: Any
    c: Any
    n: Any
    m: Any


class MixerScratch(NamedTuple):
    ext: Any
    h: Any
    c: Any
    n: Any
    m: Any


class SideIn(NamedTuple):
    q: Any
    k: Any
    v: Any
    gt: Any
    o: Any
    c0: Any
    n0: Any
    m0: Any


class SideOut(NamedTuple):
    hb: Any
    c: Any
    n: Any
    m: Any


def _split_refs(refs, *kinds):
    out, pos = [], 0
    for kind in kinds:
        n = len(kind._fields)
        out.append(kind(*refs[pos:pos + n]))
        pos += n
    assert pos == len(refs)
    return out


def _mixer_step(TS, i: MixerIn, o: MixerOut, s: MixerScratch, side_job=None):
    ti = pl.program_id(1)

    @pl.when(ti == 0)
    def _():
        _conv_init(s.ext, i.conv0, LRU_CONV)
        s.h[...] = i.h0[...]
        s.c[...] = i.c0[...]
        s.n[...] = i.n0[...]
        s.m[...] = i.m0[...]

    if side_job is not None:
        side_job()

    x = i.x[...]
    xn = _rms(x, i.g[...]).astype(BF16)
    proj = functools.partial(_in_proj, xn, i.w5, i.w2, i.b)

    hs = _lru_body(1, TS, proj(COL_U, D_LRU), i.cw[...], i.cb[...], i.wr, i.br[...],
                   i.wi, i.bi[...], i.lam[...], s.ext, s.h, o.conv, o.h)

    def get_qkvo(h):
        off = h * D_HEAD
        q = (proj(COL_Q + off, D_HEAD) * (D_HEAD ** -0.5)).astype(BF16)
        return (q, proj(COL_K + off, D_HEAD).astype(BF16), proj(COL_V + off, D_HEAD).astype(BF16),
                proj(COL_O + off, D_HEAD))

    gt = jnp.dot(xn, i.wg[...], preferred_element_type=F32) + i.bg[...]
    state = (s.c, s.n, s.m)
    hb = jnp.concatenate(_mlstm_body(1, TS, get_qkvo, gt, i.hg, state, state), axis=1)
    o.x1[...] = _merge_out(x, hs.astype(BF16), hb, proj(COL_GA, D_MODEL), proj(COL_GB, D_MODEL),
                           i.wa, i.wb, i.wo)

    @pl.when(ti == pl.num_programs(1) - 1)
    def _():
        o.c[...] = s.c[...]
        o.n[...] = s.n[...]
        o.m[...] = s.m[...]


def _mixer_kernel(TS, *refs):
    _mixer_step(TS, *_split_refs(refs, MixerIn, MixerOut, MixerScratch))


def _mixer_side_kernel(TS, SIDE_NS, SIDE_TS, *refs):
    i, si, o, so, s = _split_refs(refs, MixerIn, SideIn, MixerOut, SideOut, MixerScratch)

    def get_qkvo(h):
        sl = slice(h * D_HEAD, (h + 1) * D_HEAD)
        return si.q[:, sl], si.k[:, sl], si.v[:, sl], si.o[:, sl]

    def side_job():
        outs = _mlstm_body(SIDE_NS, SIDE_TS, get_qkvo, si.gt[...], i.hg,
                           (si.c0, si.n0, si.m0), (so.c, so.n, so.m))
        for h, out in enumerate(outs):
            so.hb[:, h * D_HEAD:(h + 1) * D_HEAD] = out

    _mixer_step(TS, i, o, s, side_job)


def _lru_weight_specs():
    nb = D_LRU // V7X_MXU_DIM
    return [_resident((LRU_CONV, D_LRU)), _resident((1, D_LRU)),
            _resident((nb, V7X_MXU_DIM, V7X_MXU_DIM)), _resident((1, D_LRU)),
            _resident((nb, V7X_MXU_DIM, V7X_MXU_DIM)), _resident((1, D_LRU)),
            _resident((1, D_LRU))]


def _lru_weights(P):
    return (P["lru_conv_w"], P["lru_conv_b"], P["w_r"], P["lru_b_r"], P["w_i"], P["lru_b_i"],
            P["lru_lambda"])


def _state_specs(NS, shared=False):
    def spec(*tail):
        zeros = (0,) * len(tail)
        index = (lambda s, t: (0,) + zeros) if shared else (lambda s, t: (s,) + zeros)
        return pl.BlockSpec((NS,) + tail, index)

    return [spec(LRU_CONV - 1, D_LRU), spec(1, D_LRU), spec(N_HEADS, D_HEAD, D_HEAD),
            spec(N_HEADS, D_HEAD), spec(1, V7X_LANES)]


def _state_shapes(NSEQ):
    return [jax.ShapeDtypeStruct((NSEQ, LRU_CONV - 1, D_LRU), F32),
            jax.ShapeDtypeStruct((NSEQ, 1, D_LRU), F32),
            jax.ShapeDtypeStruct((NSEQ, N_HEADS, D_HEAD, D_HEAD), F32),
            jax.ShapeDtypeStruct((NSEQ, N_HEADS, D_HEAD), F32),
            jax.ShapeDtypeStruct((NSEQ, 1, V7X_LANES), F32)]


def _mixer(x2, state, P, NSEQ, L, TS, side=None):
    NT = L // TS
    rows = pl.BlockSpec((TS, D_MODEL), lambda s, t: (s * NT + t, 0))
    wsq = _resident((D_MODEL, D_MODEL + W_PITCH_PAD))
    in_specs = ([rows] + _state_specs(1, shared=True) + _in_proj_specs() + _lru_weight_specs()
                + [_resident((1, D_MODEL)), wsq, wsq, wsq])
    operands = (x2, *state, *_in_proj_weights(P), *_lru_weights(P), P["mlstm_head_g"],
                P["w_branch_a"], P["w_branch_b"], P["w_out"])
    out_specs = [rows] + _state_specs(1)
    out_shape = [jax.ShapeDtypeStruct((NSEQ * L, D_MODEL), F32)] + _state_shapes(NSEQ)
    body = functools.partial(_mixer_kernel, TS)
    if side is not None:
        side_in, side_len = side
        n_side = side_in.c0.shape[0]
        side_ns = n_side // (NSEQ * NT)
        assert side_ns * NSEQ * NT == n_side
        step = lambda s, t: s * NT + t
        srows = lambda w: pl.BlockSpec((side_ns * side_len, w), lambda s, t: (step(s, t), 0))
        sstate = [pl.BlockSpec((side_ns, N_HEADS, D_HEAD, D_HEAD), lambda s, t: (step(s, t), 0, 0, 0)),
                  pl.BlockSpec((side_ns, N_HEADS, D_HEAD), lambda s, t: (step(s, t), 0, 0)),
                  pl.BlockSpec((side_ns, 1, V7X_LANES), lambda s, t: (step(s, t), 0, 0))]
        in_specs = in_specs + [srows(D_MODEL)] * 3 + [srows(GATE_W), srows(D_MODEL)] + sstate
        operands = operands + tuple(side_in)
        out_specs = out_specs + [srows(D_MODEL)] + sstate
        out_shape = (out_shape + [jax.ShapeDtypeStruct((n_side * side_len, D_MODEL), BF16)]
                     + _state_shapes(n_side)[2:])
        body = functools.partial(_mixer_side_kernel, TS, side_ns, side_len)
    return pl.pallas_call(
        body,
        grid=(NSEQ, NT),
        in_specs=in_specs,
        out_specs=out_specs,
        out_shape=out_shape,
        scratch_shapes=[pltpu.VMEM((1, D_LRU // V7X_LANES, V7X_SUBLANES + TS, V7X_LANES), F32),
                        pltpu.VMEM((1, 1, D_LRU), F32),
                        pltpu.VMEM((1, N_HEADS, D_HEAD, D_HEAD), F32),
                        pltpu.VMEM((1, N_HEADS, D_HEAD), F32),
                        pltpu.VMEM((1, 1, V7X_LANES), F32)],
        compiler_params=_params(2, 56),
        name="mixer",
    )(*operands)


def _proj_kernel(x_ref, g_ref, w5_ref, w2_ref, b_ref, wg_ref, bg_ref,
                 u_ref, q_ref, k_ref, v_ref, o_ref, ga_ref, gb_ref, gt_ref):
    xn = _rms(x_ref[...], g_ref[...]).astype(BF16)
    proj = functools.partial(_in_proj, xn, w5_ref, w2_ref, b_ref)
    u_ref[...] = proj(COL_U, D_MODEL)
    q_ref[...] = (proj(COL_Q, D_MODEL) * (D_HEAD ** -0.5)).astype(BF16)
    k_ref[...] = proj(COL_K, D_MODEL).astype(BF16)
    v_ref[...] = proj(COL_V, D_MODEL).astype(BF16)
    o_ref[...] = proj(COL_O, D_MODEL)
    ga_ref[...] = proj(COL_GA, D_MODEL)
    gb_ref[...] = proj(COL_GB, D_MODEL)
    gt_ref[...] = jnp.dot(xn, wg_ref[...], preferred_element_type=F32) + bg_ref[...]


def _proj(x2, P, tm):
    M = x2.shape[0]
    row = lambda w: pl.BlockSpec((tm, w), lambda i: (i, 0))
    f32o = jax.ShapeDtypeStruct((M, D_MODEL), F32)
    bf16o = jax.ShapeDtypeStruct((M, D_MODEL), BF16)
    return pl.pallas_call(
        _proj_kernel,
        grid=(M // tm,),
        in_specs=[row(D_MODEL)] + _in_proj_specs(),
        out_specs=[row(D_MODEL)] * 7 + [row(GATE_W)],
        out_shape=[f32o, bf16o, bf16o, bf16o, f32o, f32o, f32o,
                   jax.ShapeDtypeStruct((M, GATE_W), F32)],
        compiler_params=_params(1, 48),
        name="proj",
    )(x2, *_in_proj_weights(P))


def _lru_kernel(NS, TS, u_ref, conv0_ref, h0_ref, cw_ref, cb_ref, wr_ref, br_ref, wi_ref,
                bi_ref, lam_ref, ha_ref, convo_ref, ho_ref, ext_ref, h_s):
    @pl.when(pl.program_id(1) == 0)
    def _():
        _conv_init(ext_ref, conv0_ref, LRU_CONV)
        h_s[...] = h0_ref[...]

    hs = _lru_body(NS, TS, u_ref[...], cw_ref[...], cb_ref[...], wr_ref, br_ref[...], wi_ref,
                   bi_ref[...], lam_ref[...], ext_ref, h_s, convo_ref, ho_ref)
    ha_ref[...] = hs.astype(BF16)


def _lru(u, conv0, h0, P, NSEQ, L, NS, TS):
    R = NS * TS
    NT = L // TS
    rows = pl.BlockSpec((R, D_LRU), lambda s, t: (s * NT + t, 0))
    st = _state_specs(NS)[:2]
    return pl.pallas_call(
        functools.partial(_lru_kernel, NS, TS),
        grid=(NSEQ // NS, NT),
        in_specs=[rows] + st + _lru_weight_specs(),
        out_specs=[rows] + st,
        out_shape=[jax.ShapeDtypeStruct((NSEQ * L, D_LRU), BF16)] + _state_shapes(NSEQ)[:2],
        scratch_shapes=[pltpu.VMEM((NS, D_LRU // V7X_LANES, V7X_SUBLANES + TS, V7X_LANES), F32),
                        pltpu.VMEM((NS, 1, D_LRU), F32)],
        compiler_params=_params(2, 48),
        name="lru",
    )(u, conv0, h0, *_lru_weights(P))


def _post_kernel(x_ref, ha_ref, hb_ref, ga_ref, gb_ref, wa_ref, wb_ref, wo_ref, x1_ref):
    x1_ref[...] = _merge_out(x_ref[...], ha_ref[...], hb_ref[...], ga_ref[...], gb_ref[...],
                             wa_ref, wb_ref, wo_ref)


def _post(x2, ha, hb, ga, gb, P, tm):
    M = x2.shape[0]
    row = pl.BlockSpec((tm, D_MODEL), lambda i: (i, 0))
    wsp = _resident((D_MODEL, D_MODEL + W_PITCH_PAD))
    return pl.pallas_call(
        _post_kernel,
        grid=(M // tm,),
        in_specs=[row] * 5 + [wsp] * 3,
        out_specs=row,
        out_shape=jax.ShapeDtypeStruct((M, D_MODEL), F32),
        compiler_params=_params(1, 48),
        name="post",
    )(x2, ha, hb, ga, gb, P["w_branch_a"], P["w_branch_b"], P["w_out"])


def _ffn_kernel(NS, TS, x1_ref, st0_ref, g2_ref, wup_ref, cw_ref, cb_ref, wdn_ref, gf_ref,
                y_ref, sto_ref, ext_ref):
    R = NS * TS
    W = 2 * D_FF

    @pl.when(pl.program_id(1) == 0)
    def _():
        _conv_init(ext_ref, st0_ref, FFN_CONV)

    x1 = x1_ref[...]
    xn = _rms(x1, g2_ref[...]).astype(BF16)
    up = jnp.dot(xn, wup_ref[...], preferred_element_type=F32).reshape(NS, TS, W)
    upc = _causal_conv(ext_ref, up, cw_ref[...], cb_ref[...], sto_ref).reshape(R, W)
    act = (jax.nn.gelu(upc[:, D_FF:]) * upc[:, :D_FF]).astype(BF16)
    x2 = x1 + jnp.dot(act, wdn_ref[:, :D_MODEL], preferred_element_type=F32)
    y_ref[...] = _rms(x2, gf_ref[...])


def _ffn(x1, st0, P, NSEQ, L, NS, TS, shared=False):
    R = NS * TS
    NT = L // TS
    W = 2 * D_FF
    rows = pl.BlockSpec((R, D_MODEL), lambda s, t: (s * NT + t, 0))
    stspec = pl.BlockSpec((NS, FFN_CONV - 1, W), lambda s, t: (s, 0, 0))
    st0spec = pl.BlockSpec((NS, FFN_CONV - 1, W), lambda s, t: (0, 0, 0)) if shared else stspec
    return pl.pallas_call(
        functools.partial(_ffn_kernel, NS, TS),
        grid=(NSEQ // NS, NT),
        in_specs=[rows, st0spec, _resident((1, D_MODEL)), _resident((D_MODEL, W)),
                  _resident((FFN_CONV, W)), _resident((1, W)),
                  _resident((D_FF, D_MODEL + W_PITCH_PAD)),
                  _resident((1, D_MODEL))],
        out_specs=[rows, stspec],
        out_shape=[jax.ShapeDtypeStruct((NSEQ * L, D_MODEL), F32),
                   jax.ShapeDtypeStruct((NSEQ, FFN_CONV - 1, W), F32)],
        scratch_shapes=[pltpu.VMEM((NS, W // V7X_LANES, V7X_SUBLANES + TS, V7X_LANES), F32)],
        compiler_params=_params(2, 56),
        name="ffn",
    )(x1, st0, P["norm2_g"], P["w_up"], P["ffn_conv_w"], P["ffn_conv_b"], P["w_down"],
      P["final_g"])


def _block_diag(w):
    bw = w.shape[1]
    per = V7X_MXU_DIM // bw
    nb = w.shape[0] // per
    w4 = w.reshape(nb, per, bw, 1, bw)
    on_diag = jnp.eye(per, dtype=w.dtype).reshape(1, per, 1, per, 1)
    return (w4 * on_diag).reshape(nb, V7X_MXU_DIM, V7X_MXU_DIM)


def _split_w_in_kernel(w_ref, w5_ref, w2_ref, wg_ref):
    w = w_ref[0]
    pad = jnp.zeros((w.shape[0], W_PITCH_PAD), BF16)
    w5_ref[...] = jnp.concatenate([w[:, :N_W5].astype(BF16), pad], axis=1)
    w2_ref[...] = jnp.concatenate([w[:, N_W5 + 2 * N_HEADS:].astype(BF16), pad], axis=1)
    t = w[:, N_W5:N_W5 + V7X_LANES]
    head_lane = lax.broadcasted_iota(jnp.int32, t.shape, 1) < N_HEADS
    ig = jnp.where(head_lane, t, 0.0)
    fg = jnp.where(head_lane, pltpu.roll(t, V7X_LANES - N_HEADS, axis=1), 0.0)
    wg_ref[...] = jnp.concatenate([ig, fg], axis=1).astype(BF16)


def _split_w_in(w_in):
    n_in = w_in.shape[2]
    assert n_in == N_W5 + 2 * N_HEADS + N_W2
    tr = V7X_LANES
    blk = lambda w: pl.BlockSpec((tr, w), lambda i: (i, 0))
    return pl.pallas_call(
        _split_w_in_kernel,
        grid=(D_MODEL // tr,),
        in_specs=[pl.BlockSpec((1, tr, n_in), lambda i: (0, i, 0))],
        out_specs=[blk(N_W5 + W_PITCH_PAD), blk(N_W2 + W_PITCH_PAD), blk(GATE_W)],
        out_shape=[jax.ShapeDtypeStruct((D_MODEL, N_W5 + W_PITCH_PAD), BF16),
                   jax.ShapeDtypeStruct((D_MODEL, N_W2 + W_PITCH_PAD), BF16),
                   jax.ShapeDtypeStruct((D_MODEL, GATE_W), BF16)],
        compiler_params=_params(1, 32),
        name="split_w_in",
    )(w_in)


def _run_long_group(x3, state, P, TS, side=None):
    NSEQ, L, _ = x3.shape
    x2 = x3.reshape(NSEQ * L, D_MODEL)
    x1, conv1, h1, c1, n1, m1, *side_out = _mixer(x2, state[:5], P, NSEQ, L, TS, side)
    y, ffn1 = _ffn(x1, state[5], P, NSEQ, L, 1, TS, shared=True)
    return y.reshape(NSEQ, L, D_MODEL), (conv1, h1, c1, n1, m1, ffn1), side_out


def _short_group_front(x2, state, P, NSEQ, L, tm, ns):
    conv0, h0, c0, n0, m0, _ = state
    u, q, k, v, o, ga, gb, gt = _proj(x2, P, tm)
    ha, conv1, h1 = _lru(u, conv0, h0, P, NSEQ, L, ns, L)
    return SideIn(q, k, v, gt, o, c0, n0, m0), (ha, ga, gb, conv1, h1)


def _short_group_back(x2, front, side_out, ffn0, P, NSEQ, L, tm, ns):
    ha, ga, gb, conv1, h1 = front
    hb, c1, n1, m1 = side_out
    x1 = _post(x2, ha, hb, ga, gb, P, tm)
    y, ffn1 = _ffn(x1, ffn0, P, NSEQ, L, ns, L)
    return y.reshape(NSEQ, L, D_MODEL), (conv1, h1, c1, n1, m1, ffn1)


def kernel(x_prompt, x_sample, state_lru_conv, state_lru_h, state_mlstm_C, state_mlstm_n,
           state_mlstm_m, state_ffn_conv, meta_tokens, norm1_g, w_in, b_in, lru_conv_w,
           lru_conv_b, lru_w_r, lru_b_r, lru_w_i, lru_b_i, lru_lambda, mlstm_head_g,
           w_branch_a, w_branch_b, w_out, norm2_g, w_up, ffn_conv_w, ffn_conv_b, w_down, final_g):
    assert w_in.shape[0] == 1, "single-layer trunk"
    b0 = b_in[0]
    gate_pad = jnp.zeros((V7X_LANES - N_HEADS,), b0.dtype)
    row = lambda a: a.reshape(1, -1).astype(F32)
    pitch_pad = lambda w: jnp.pad(w.astype(BF16), ((0, 0), (0, W_PITCH_PAD)))
    w5, w2, w_gate = _split_w_in(w_in)
    P = {
        "norm1_g": row(norm1_g[0]),
        "w5": w5,
        "w2": w2,
        "b_main": row(jnp.concatenate([b0[:N_W5], b0[N_W5 + 2 * N_HEADS:]])),
        "w_gate": w_gate,
        "b_gate": row(jnp.concatenate([b0[N_W5:N_W5 + N_HEADS], gate_pad,
                                       b0[N_W5 + N_HEADS:N_W5 + 2 * N_HEADS], gate_pad])),
        "lru_conv_w": lru_conv_w[0],
        "lru_conv_b": row(lru_conv_b[0]),
        "w_r": _block_diag(lru_w_r[0]).astype(BF16),
        "lru_b_r": row(lru_b_r[0]),
        "w_i": _block_diag(lru_w_i[0]).astype(BF16),
        "lru_b_i": row(lru_b_i[0]),
        "lru_lambda": row(lru_lambda[0]),
        "mlstm_head_g": row(mlstm_head_g[0]),
        "w_branch_a": pitch_pad(w_branch_a[0]),
        "w_branch_b": pitch_pad(w_branch_b[0]),
        "w_out": pitch_pad(w_out[0]),
        "norm2_g": row(norm2_g[0]),
        "w_up": w_up[0].astype(BF16),
        "ffn_conv_w": ffn_conv_w[0],
        "ffn_conv_b": row(ffn_conv_b[0]),
        "w_down": pitch_pad(w_down[0]),
        "final_g": row(final_g),
    }

    def pack_state(conv, h, c, n, m, ffn):
        nseq = h.shape[0]
        m_pad = jnp.pad(m.astype(F32)[:, None, :], ((0, 0), (0, 0), (0, V7X_LANES - N_HEADS)))
        return (conv.astype(F32), h.astype(F32).reshape(nseq, 1, D_LRU), c.astype(F32),
                n.astype(F32), m_pad, ffn.astype(F32))

    def unpack_state(st):
        conv, h, c, n, m, ffn = st
        return (conv[None], h.reshape(1, -1, D_LRU), c[None], n[None],
                m[:, 0, :N_HEADS][None], ffn[None])

    zero = pack_state(jnp.zeros((1, LRU_CONV - 1, D_LRU), F32), jnp.zeros((1, D_LRU), F32),
                      jnp.zeros((1, N_HEADS, D_HEAD, D_HEAD), F32),
                      jnp.zeros((1, N_HEADS, D_HEAD), F32), jnp.zeros((1, N_HEADS), F32),
                      jnp.zeros((1, FFN_CONV - 1, 2 * D_FF), F32))
    _, meta_state, _ = _run_long_group(meta_tokens[None].astype(F32), zero, P, N_META)
    sample_state0 = pack_state(state_lru_conv[0], state_lru_h[0], state_mlstm_C[0],
                               state_mlstm_n[0], state_mlstm_m[0], state_ffn_conv[0])
    n_sample, l_sample, _ = x_sample.shape
    xs2 = x_sample.reshape(n_sample * l_sample, D_MODEL)
    short = dict(NSEQ=n_sample, L=l_sample, tm=SHORT_ROWS, ns=SHORT_ROWS // l_sample)
    side_in, front = _short_group_front(xs2, sample_state0, P, **short)
    y_prompt, prompt_state, side_out = _run_long_group(x_prompt, meta_state, P, LONG_TS,
                                                       side=(side_in, l_sample))
    y_sample, sample_state = _short_group_back(xs2, front, side_out, sample_state0[5], P, **short)
    return (y_prompt, y_sample) + unpack_state(prompt_state) + unpack_state(sample_state)
```

```python
import functools
from typing import Any, NamedTuple

import jax
import jax.numpy as jnp
from jax import lax
from jax.experimental import pallas as pl
from jax.experimental.pallas import tpu as pltpu

F32 = jnp.float32
BF16 = jnp.bfloat16

D_MODEL = 1024
D_LRU = 1024
LRU_CONV = 4
LRU_C = 8.0
N_HEADS = 4
D_HEAD = 256
D_FF = 2816
FFN_CONV = 3
N_META = 16
EPS = 1e-6

V7X_LANES = 128
V7X_SUBLANES = 8
V7X_MXU_DIM = 256
NEG_BIG = -1e30

LONG_TS = 256
SHORT_ROWS = 256

N_MAIN = 7 * D_MODEL
N_W5 = 5 * D_MODEL
N_W2 = 2 * D_MODEL
W_PITCH_PAD = V7X_LANES
GATE_W = 2 * V7X_LANES
COL_U, COL_Q, COL_K, COL_V, COL_O, COL_GA, COL_GB = (j * D_MODEL for j in range(7))


def _resident(shape):
    return pl.BlockSpec(shape, lambda *_: (0,) * len(shape), pipeline_mode=pl.Buffered(1))


def _params(n_grid, vmem_mb, flags=None):
    return pltpu.CompilerParams(
        dimension_semantics=("arbitrary",) * n_grid,
        vmem_limit_bytes=vmem_mb * 1024 * 1024,
        flags=flags,
    )


def _rms(x, g):
    ms = jnp.mean(x * x, axis=-1, keepdims=True)
    return x * lax.rsqrt(ms + EPS) * g


def _in_proj(xn, w5_ref, w2_ref, b_ref, col, width):
    if col < N_W5:
        w = w5_ref[:, col:col + width]
    else:
        w = w2_ref[:, col - N_W5:col - N_W5 + width]
    return jnp.dot(xn, w, preferred_element_type=F32) + b_ref[:, col:col + width]


def _in_proj_specs():
    return [_resident((1, D_MODEL)), _resident((D_MODEL, N_W5 + W_PITCH_PAD)),
            _resident((D_MODEL, N_W2 + W_PITCH_PAD)),
            _resident((1, N_MAIN)), _resident((D_MODEL, GATE_W)), _resident((1, GATE_W))]


def _in_proj_weights(P):
    return (P["norm1_g"], P["w5"], P["w2"], P["b_main"], P["w_gate"], P["b_gate"])


def _conv_init(ext_ref, hist0_ref, taps):
    pad, hist = V7X_SUBLANES, taps - 1
    for g in range(ext_ref.shape[1]):
        ls = slice(g * V7X_LANES, (g + 1) * V7X_LANES)
        ext_ref[:, g, pad - hist:pad, :] = hist0_ref[:, :, ls]


def _causal_conv(ext_ref, x3, cw, cb, hist_out_ref):
    taps = cw.shape[0]
    ts = x3.shape[1]
    pad, hist = V7X_SUBLANES, taps - 1
    outs = []
    for g in range(ext_ref.shape[1]):
        ls = slice(g * V7X_LANES, (g + 1) * V7X_LANES)
        xg = x3[:, :, ls]
        ext_ref[:, g, pad:, :] = xg
        acc = cb[:, ls] + cw[taps - 1:taps, ls] * xg
        for j in range(hist):
            acc = acc + cw[j:j + 1, ls] * ext_ref[:, g, pad - hist + j:pad - hist + j + ts, :]
        outs.append(acc)
        new_hist = ext_ref[:, g, pad + ts - hist:pad + ts, :]
        ext_ref[:, g, pad - hist:pad, :] = new_hist
        hist_out_ref[:, :, ls] = new_hist
    return jnp.concatenate(outs, axis=-1)


def _lru_body(NS, TS, u2, cw, cb, wr_ref, br, wi_ref, bi, lam, ext_ref, h_ref, convo_ref, ho_ref):
    R = NS * TS
    C = D_LRU
    SB = V7X_SUBLANES
    uc2 = _causal_conv(ext_ref, u2.reshape(NS, TS, C), cw, cb, convo_ref).reshape(R, C)
    ucb = uc2.astype(BF16)

    def block_diag(w_ref):
        W = V7X_MXU_DIM
        return jnp.concatenate(
            [jnp.dot(ucb[:, g * W:(g + 1) * W], w_ref[g], preferred_element_type=F32)
             for g in range(C // W)], axis=1)

    r = jax.nn.sigmoid(block_diag(wr_ref) + br)
    i = jax.nn.sigmoid(block_diag(wi_ref) + bi)
    log_a = -LRU_C * r * jax.nn.softplus(-lam)
    a = jnp.exp(log_a)
    hh = jnp.sqrt(-jnp.tanh(log_a) * (a * a + 1.0)) * (i * uc2)

    a = a.reshape(R // SB, SB, C)
    hh = hh.reshape(R // SB, SB, C)
    sub = lax.broadcasted_iota(jnp.int32, (1, SB, C), 1)
    for d in (1, 2, 4):
        keep = sub >= d
        a_sh = pltpu.roll(a, d, axis=1)
        h_sh = pltpu.roll(hh, d, axis=1)
        hh = hh + a * jnp.where(keep, h_sh, 0.0)
        a = a * jnp.where(keep, a_sh, 1.0)

    nb = TS // SB
    a = a.reshape(NS, nb, SB, C)
    hh = hh.reshape(NS, nb, SB, C)
    h = jnp.broadcast_to(h_ref[...], (NS, SB, C))
    blocks = []
    for j in range(nb):
        hj = hh[:, j] + a[:, j] * h
        blocks.append(hj)
        h = jnp.broadcast_to(hj[:, SB - 1:, :], (NS, SB, C))
    h_ref[...] = h[:, 0:1, :]
    ho_ref[...] = h[:, 0:1, :]
    return jnp.concatenate(blocks, axis=1).reshape(R, C)


def _seg_scan(x, tpos, TS, op, ident):
    d = 1
    while d < TS:
        sh = pltpu.roll(x, d, axis=0)
        x = op(x, jnp.where(tpos >= d, sh, ident))
        d *= 2
    return x


def _pad_rows(x, rows):
    if x.shape[0] >= rows:
        return x
    return jnp.concatenate([x, jnp.zeros((rows - x.shape[0],) + x.shape[1:], x.dtype)], axis=0)


def _mlstm_body(NS, TS, get_qkvo, gt, hg_ref, st_in, st_out):
    R = NS * TS
    RC = max(R, V7X_LANES)
    LN = V7X_LANES
    shift = TS.bit_length() - 1

    ig4 = gt[:, :LN]
    lf4 = jax.nn.log_sigmoid(gt[:, LN:])
    tpos = lax.broadcasted_iota(jnp.int32, (R, LN), 0) & (TS - 1)
    b4 = _seg_scan(lf4, tpos, TS, jnp.add, 0.0)
    c4 = ig4 - b4
    cmax4 = _seg_scan(c4, tpos, TS, jnp.maximum, -jnp.inf)
    c_in, n_in, m_in = st_in
    c_out, n_out, m_out = st_out
    m_prev = [m_in[j] for j in range(NS)]
    m_rows = jnp.concatenate([jnp.broadcast_to(m, (TS, LN)) for m in m_prev], axis=0)
    big_m4 = jnp.maximum(cmax4, m_rows)
    e4 = jnp.exp(m_rows - big_m4)
    dinv4 = jnp.exp(-(b4 + big_m4))

    decay4, wk_parts = [], []
    for j in range(NS):
        b_last = b4[(j + 1) * TS - 1:(j + 1) * TS, :]
        g4 = b_last + c4[j * TS:(j + 1) * TS, :]
        mn = jnp.maximum(b_last + m_prev[j], jnp.max(g4, axis=0, keepdims=True))
        decay4.append(jnp.exp(b_last + m_prev[j] - mn))
        wk_parts.append(jnp.exp(g4 - mn))
        m_out[j] = mn
    wk4 = jnp.concatenate(wk_parts, axis=0)

    ri = lax.broadcasted_iota(jnp.int32, (R, RC), 0)
    ci = lax.broadcasted_iota(jnp.int32, (R, RC), 1)
    eye = ri == ci
    if NS == 1:
        causal = ci <= ri
    else:
        causal = (ci <= ri) & ((ri >> shift) == (ci >> shift))
    seq_of_row = lax.broadcasted_iota(jnp.int32, (R, D_HEAD), 0) >> shift

    outs = []
    for h in range(N_HEADS):
        sl = slice(h * D_HEAD, (h + 1) * D_HEAD)
        qh, kh, vh, oh = get_qkvo(h)
        kh_p = _pad_rows(kh, RC)
        vh_p = _pad_rows(vh, RC)
        c_c = c4[:, h:h + 1]
        big_m_c = big_m4[:, h:h + 1]
        e_c = e4[:, h:h + 1]
        dinv_c = dinv4[:, h:h + 1]
        wk_c = wk4[:, h:h + 1]

        qk = lax.dot_general(qh, kh_p, (((1,), (1,)), ((), ())), preferred_element_type=F32)
        c_r = jnp.sum(jnp.where(eye, c_c, 0.0), axis=0, keepdims=True)
        w = jnp.exp(jnp.where(causal, c_r - big_m_c, NEG_BIG))
        s = qk * w
        den = jnp.sum(s, axis=1, keepdims=True)
        num = jnp.dot(s.astype(BF16), vh_p, preferred_element_type=F32)

        qf = qh.astype(F32)
        if NS == 1:
            q_c = jnp.dot(qh, c_in[0, h].astype(BF16), preferred_element_type=F32)
            n_rows = n_in[0, h:h + 1, :]
        else:
            q_c = jnp.zeros((R, D_HEAD), F32)
            n_rows = jnp.zeros((R, D_HEAD), F32)
            for j in range(NS):
                mine = seq_of_row == j
                q_c = jnp.where(mine, jnp.dot(qh, c_in[j, h].astype(BF16),
                                              preferred_element_type=F32), q_c)
                n_rows = jnp.where(mine, n_in[j, h:h + 1, :], n_rows)
        q_n = jnp.sum(qf * n_rows, axis=1, keepdims=True)
        num = num + e_c * q_c
        den = den + e_c * q_n
        hh = num * (1.0 / jnp.maximum(jnp.abs(den), dinv_c))
        hh = hh * lax.rsqrt(jnp.mean(hh * hh, axis=1, keepdims=True) + EPS)
        outs.append(((hh * hg_ref[:, sl]) * jax.nn.sigmoid(oh)).astype(BF16))

        kw = kh.astype(F32) * wk_c
        for j in range(NS):
            kwj = kw if NS == 1 else jnp.where(seq_of_row == j, kw, 0.0)
            upd = lax.dot_general(_pad_rows(kwj, RC).astype(BF16), vh_p,
                                  (((0,), (0,)), ((), ())), preferred_element_type=F32)
            dec = decay4[j][:, h:h + 1]
            c_out[j, h] = dec * c_in[j, h] + upd
            n_out[j, h:h + 1, :] = (dec * n_in[j, h:h + 1, :]
                                    + jnp.sum(kwj, axis=0, keepdims=True))
    return outs


def _merge_out(x, ha, hb, ga, gb, wa_ref, wb_ref, wo_ref):
    pa = jnp.dot(ha, wa_ref[:, :D_MODEL], preferred_element_type=F32)
    pb = jnp.dot(hb, wb_ref[:, :D_MODEL], preferred_element_type=F32)
    merged = jax.nn.sigmoid(ga) * pa + jax.nn.sigmoid(gb) * pb
    return x + jnp.dot(merged.astype(BF16), wo_ref[:, :D_MODEL], preferred_element_type=F32)


class MixerIn(NamedTuple):
    x: Any
    conv0: Any
    h0: Any
    c0: Any
    n0: Any
    m0: Any
    g: Any
    w5: Any
    w2: Any
    b: Any
    wg: Any
    bg: Any
    cw: Any
    cb: Any
    wr: Any
    br: Any
    wi: Any
    bi: Any
    lam: Any
    hg: Any
    wa: Any
    wb: Any
    wo: Any


class MixerOut(NamedTuple):
    x1: Any
    conv: Any
    h: Any
    c: Any
    n: Any
    m: Any


class MixerScratch(NamedTuple):
    ext: Any
    h: Any
    c: Any
    n: Any
    m: Any


class SideIn(NamedTuple):
    q: Any
    k: Any
    v: Any
    gt: Any
    o: Any
    c0: Any
    n0: Any
    m0: Any


class SideOut(NamedTuple):
    hb: Any
    c: Any
    n: Any
    m: Any


def _split_refs(refs, *kinds):
    out, pos = [], 0
    for kind in kinds:
        n = len(kind._fields)
        out.append(kind(*refs[pos:pos + n]))
        pos += n
    assert pos == len(refs)
    return out


def _mixer_step(TS, i: MixerIn, o: MixerOut, s: MixerScratch, side_job=None):
    ti = pl.program_id(1)

    @pl.when(ti == 0)
    def _():
        _conv_init(s.ext, i.conv0, LRU_CONV)
        s.h[...] = i.h0[...]
        s.c[...] = i.c0[...]
        s.n[...] = i.n0[...]
        s.m[...] = i.m0[...]

    if side_job is not None:
        side_job()

    x = i.x[...]
    xn = _rms(x, i.g[...]).astype(BF16)
    proj = functools.partial(_in_proj, xn, i.w5, i.w2, i.b)

    hs = _lru_body(1, TS, proj(COL_U, D_LRU), i.cw[...], i.cb[...], i.wr, i.br[...],
                   i.wi, i.bi[...], i.lam[...], s.ext, s.h, o.conv, o.h)

    def get_qkvo(h):
        off = h * D_HEAD
        q = (proj(COL_Q + off, D_HEAD) * (D_HEAD ** -0.5)).astype(BF16)
        return (q, proj(COL_K + off, D_HEAD).astype(BF16), proj(COL_V + off, D_HEAD).astype(BF16),
                proj(COL_O + off, D_HEAD))

    gt = jnp.dot(xn, i.wg[...], preferred_element_type=F32) + i.bg[...]
    state = (s.c, s.n, s.m)
    hb = jnp.concatenate(_mlstm_body(1, TS, get_qkvo, gt, i.hg, state, state), axis=1)
    o.x1[...] = _merge_out(x, hs.astype(BF16), hb, proj(COL_GA, D_MODEL), proj(COL_GB, D_MODEL),
                           i.wa, i.wb, i.wo)

    @pl.when(ti == pl.num_programs(1) - 1)
    def _():
        o.c[...] = s.c[...]
        o.n[...] = s.n[...]
        o.m[...] = s.m[...]


def _mixer_kernel(TS, *refs):
    _mixer_step(TS, *_split_refs(refs, MixerIn, MixerOut, MixerScratch))


def _mixer_side_kernel(TS, SIDE_NS, SIDE_TS, *refs):
    i, si, o, so, s = _split_refs(refs, MixerIn, SideIn, MixerOut, SideOut, MixerScratch)

    def get_qkvo(h):
        sl = slice(h * D_HEAD, (h + 1) * D_HEAD)
        return si.q[:, sl], si.k[:, sl], si.v[:, sl], si.o[:, sl]

    def side_job():
        outs = _mlstm_body(SIDE_NS, SIDE_TS, get_qkvo, si.gt[...], i.hg,
                           (si.c0, si.n0, si.m0), (so.c, so.n, so.m))
        for h, out in enumerate(outs):
            so.hb[:, h * D_HEAD:(h + 1) * D_HEAD] = out

    _mixer_step(TS, i, o, s, side_job)


def _lru_weight_specs():
    nb = D_LRU // V7X_MXU_DIM
    return [_resident((LRU_CONV, D_LRU)), _resident((1, D_LRU)),
            _resident((nb, V7X_MXU_DIM, V7X_MXU_DIM)), _resident((1, D_LRU)),
            _resident((nb, V7X_MXU_DIM, V7X_MXU_DIM)), _resident((1, D_LRU)),
            _resident((1, D_LRU))]


def _lru_weights(P):
    return (P["lru_conv_w"], P["lru_conv_b"], P["w_r"], P["lru_b_r"], P["w_i"], P["lru_b_i"],
            P["lru_lambda"])


def _state_specs(NS, shared=False):
    def spec(*tail):
        zeros = (0,) * len(tail)
        index = (lambda s, t: (0,) + zeros) if shared else (lambda s, t: (s,) + zeros)
        return pl.BlockSpec((NS,) + tail, index)

    return [spec(LRU_CONV - 1, D_LRU), spec(1, D_LRU), spec(N_HEADS, D_HEAD, D_HEAD),
            spec(N_HEADS, D_HEAD), spec(1, V7X_LANES)]


def _state_shapes(NSEQ):
    return [jax.ShapeDtypeStruct((NSEQ, LRU_CONV - 1, D_LRU), F32),
            jax.ShapeDtypeStruct((NSEQ, 1, D_LRU), F32),
            jax.ShapeDtypeStruct((NSEQ, N_HEADS, D_HEAD, D_HEAD), F32),
            jax.ShapeDtypeStruct((NSEQ, N_HEADS, D_HEAD), F32),
            jax.ShapeDtypeStruct((NSEQ, 1, V7X_LANES), F32)]


def _mixer(x2, state, P, NSEQ, L, TS, side=None):
    NT = L // TS
    rows = pl.BlockSpec((TS, D_MODEL), lambda s, t: (s * NT + t, 0))
    wsq = _resident((D_MODEL, D_MODEL + W_PITCH_PAD))
    in_specs = ([rows] + _state_specs(1, shared=True) + _in_proj_specs() + _lru_weight_specs()
                + [_resident((1, D_MODEL)), wsq, wsq, wsq])
    operands = (x2, *state, *_in_proj_weights(P), *_lru_weights(P), P["mlstm_head_g"],
                P["w_branch_a"], P["w_branch_b"], P["w_out"])
    out_specs = [rows] + _state_specs(1)
    out_shape = [jax.ShapeDtypeStruct((NSEQ * L, D_MODEL), F32)] + _state_shapes(NSEQ)
    body = functools.partial(_mixer_kernel, TS)
    if side is not None:
        side_in, side_len = side
        n_side = side_in.c0.shape[0]
        side_ns = n_side // (NSEQ * NT)
        assert side_ns * NSEQ * NT == n_side
        step = lambda s, t: s * NT + t
        srows = lambda w: pl.BlockSpec((side_ns * side_len, w), lambda s, t: (step(s, t), 0))
        sstate = [pl.BlockSpec((side_ns, N_HEADS, D_HEAD, D_HEAD), lambda s, t: (step(s, t), 0, 0, 0)),
                  pl.BlockSpec((side_ns, N_HEADS, D_HEAD), lambda s, t: (step(s, t), 0, 0)),
                  pl.BlockSpec((side_ns, 1, V7X_LANES), lambda s, t: (step(s, t), 0, 0))]
        in_specs = in_specs + [srows(D_MODEL)] * 3 + [srows(GATE_W), srows(D_MODEL)] + sstate
        operands = operands + tuple(side_in)
        out_specs = out_specs + [srows(D_MODEL)] + sstate
        out_shape = (out_shape + [jax.ShapeDtypeStruct((n_side * side_len, D_MODEL), BF16)]
                     + _state_shapes(n_side)[2:])
        body = functools.partial(_mixer_side_kernel, TS, side_ns, side_len)
    return pl.pallas_call(
        body,
        grid=(NSEQ, NT),
        in_specs=in_specs,
        out_specs=out_specs,
        out_shape=out_shape,
        scratch_shapes=[pltpu.VMEM((1, D_LRU // V7X_LANES, V7X_SUBLANES + TS, V7X_LANES), F32),
                        pltpu.VMEM((1, 1, D_LRU), F32),
                        pltpu.VMEM((1, N_HEADS, D_HEAD, D_HEAD), F32),
                        pltpu.VMEM((1, N_HEADS, D_HEAD), F32),
                        pltpu.VMEM((1, 1, V7X_LANES), F32)],
        compiler_params=_params(2, 56),
        name="mixer",
    )(*operands)


def _proj_kernel(x_ref, g_ref, w5_ref, w2_ref, b_ref, wg_ref, bg_ref,
                 u_ref, q_ref, k_ref, v_ref, o_ref, ga_ref, gb_ref, gt_ref):
    xn = _rms(x_ref[...], g_ref[...]).astype(BF16)
    proj = functools.partial(_in_proj, xn, w5_ref, w2_ref, b_ref)
    u_ref[...] = proj(COL_U, D_MODEL)
    q_ref[...] = (proj(COL_Q, D_MODEL) * (D_HEAD ** -0.5)).astype(BF16)
    k_ref[...] = proj(COL_K, D_MODEL).astype(BF16)
    v_ref[...] = proj(COL_V, D_MODEL).astype(BF16)
    o_ref[...] = proj(COL_O, D_MODEL)
    ga_ref[...] = proj(COL_GA, D_MODEL)
    gb_ref[...] = proj(COL_GB, D_MODEL)
    gt_ref[...] = jnp.dot(xn, wg_ref[...], preferred_element_type=F32) + bg_ref[...]


def _proj(x2, P, tm):
    M = x2.shape[0]
    row = lambda w: pl.BlockSpec((tm, w), lambda i: (i, 0))
    f32o = jax.ShapeDtypeStruct((M, D_MODEL), F32)
    bf16o = jax.ShapeDtypeStruct((M, D_MODEL), BF16)
    return pl.pallas_call(
        _proj_kernel,
        grid=(M // tm,),
        in_specs=[row(D_MODEL)] + _in_proj_specs(),
        out_specs=[row(D_MODEL)] * 7 + [row(GATE_W)],
        out_shape=[f32o, bf16o, bf16o, bf16o, f32o, f32o, f32o,
                   jax.ShapeDtypeStruct((M, GATE_W), F32)],
        compiler_params=_params(1, 48),
        name="proj",
    )(x2, *_in_proj_weights(P))


def _lru_kernel(NS, TS, u_ref, conv0_ref, h0_ref, cw_ref, cb_ref, wr_ref, br_ref, wi_ref,
                bi_ref, lam_ref, ha_ref, convo_ref, ho_ref, ext_ref, h_s):
    @pl.when(pl.program_id(1) == 0)
    def _():
        _conv_init(ext_ref, conv0_ref, LRU_CONV)
        h_s[...] = h0_ref[...]

    hs = _lru_body(NS, TS, u_ref[...], cw_ref[...], cb_ref[...], wr_ref, br_ref[...], wi_ref,
                   bi_ref[...], lam_ref[...], ext_ref, h_s, convo_ref, ho_ref)
    ha_ref[...] = hs.astype(BF16)


def _lru(u, conv0, h0, P, NSEQ, L, NS, TS):
    R = NS * TS
    NT = L // TS
    rows = pl.BlockSpec((R, D_LRU), lambda s, t: (s * NT + t, 0))
    st = _state_specs(NS)[:2]
    return pl.pallas_call(
        functools.partial(_lru_kernel, NS, TS),
        grid=(NSEQ // NS, NT),
        in_specs=[rows] + st + _lru_weight_specs(),
        out_specs=[rows] + st,
        out_shape=[jax.ShapeDtypeStruct((NSEQ * L, D_LRU), BF16)] + _state_shapes(NSEQ)[:2],
        scratch_shapes=[pltpu.VMEM((NS, D_LRU // V7X_LANES, V7X_SUBLANES + TS, V7X_LANES), F32),
                        pltpu.VMEM((NS, 1, D_LRU), F32)],
        compiler_params=_params(2, 48),
        name="lru",
    )(u, conv0, h0, *_lru_weights(P))


def _post_kernel(x_ref, ha_ref, hb_ref, ga_ref, gb_ref, wa_ref, wb_ref, wo_ref, x1_ref):
    x1_ref[...] = _merge_out(x_ref[...], ha_ref[...], hb_ref[...], ga_ref[...], gb_ref[...],
                             wa_ref, wb_ref, wo_ref)


def _post(x2, ha, hb, ga, gb, P, tm):
    M = x2.shape[0]
    row = pl.BlockSpec((tm, D_MODEL), lambda i: (i, 0))
    wsp = _resident((D_MODEL, D_MODEL + W_PITCH_PAD))
    return pl.pallas_call(
        _post_kernel,
        grid=(M // tm,),
        in_specs=[row] * 5 + [wsp] * 3,
        out_specs=row,
        out_shape=jax.ShapeDtypeStruct((M, D_MODEL), F32),
        compiler_params=_params(1, 48),
        name="post",
    )(x2, ha, hb, ga, gb, P["w_branch_a"], P["w_branch_b"], P["w_out"])


def _ffn_kernel(NS, TS, x1_ref, st0_ref, g2_ref, wup_ref, cw_ref, cb_ref, wdn_ref, gf_ref,
                y_ref, sto_ref, ext_ref):
    R = NS * TS
    W = 2 * D_FF

    @pl.when(pl.program_id(1) == 0)
    def _():
        _conv_init(ext_ref, st0_ref, FFN_CONV)

    x1 = x1_ref[...]
    xn = _rms(x1, g2_ref[...]).astype(BF16)
    up = jnp.dot(xn, wup_ref[...], preferred_element_type=F32).reshape(NS, TS, W)
    upc = _causal_conv(ext_ref, up, cw_ref[...], cb_ref[...], sto_ref).reshape(R, W)
    act = (jax.nn.gelu(upc[:, D_FF:]) * upc[:, :D_FF]).astype(BF16)
    x2 = x1 + jnp.dot(act, wdn_ref[:, :D_MODEL], preferred_element_type=F32)
    y_ref[...] = _rms(x2, gf_ref[...])


def _ffn(x1, st0, P, NSEQ, L, NS, TS, shared=False):
    R = NS * TS
    NT = L // TS
    W = 2 * D_FF
    rows = pl.BlockSpec((R, D_MODEL), lambda s, t: (s * NT + t, 0))
    stspec = pl.BlockSpec((NS, FFN_CONV - 1, W), lambda s, t: (s, 0, 0))
    st0spec = pl.BlockSpec((NS, FFN_CONV - 1, W), lambda s, t: (0, 0, 0)) if shared else stspec
    return pl.pallas_call(
        functools.partial(_ffn_kernel, NS, TS),
        grid=(NSEQ // NS, NT),
        in_specs=[rows, st0spec, _resident((1, D_MODEL)), _resident((D_MODEL, W)),
                  _resident((FFN_CONV, W)), _resident((1, W)),
                  _resident((D_FF, D_MODEL + W_PITCH_PAD)),
                  _resident((1, D_MODEL))],
        out_specs=[rows, stspec],
        out_shape=[jax.ShapeDtypeStruct((NSEQ * L, D_MODEL), F32),
                   jax.ShapeDtypeStruct((NSEQ, FFN_CONV - 1, W), F32)],
        scratch_shapes=[pltpu.VMEM((NS, W // V7X_LANES, V7X_SUBLANES + TS, V7X_LANES), F32)],
        compiler_params=_params(2, 56),
        name="ffn",
    )(x1, st0, P["norm2_g"], P["w_up"], P["ffn_conv_w"], P["ffn_conv_b"], P["w_down"],
      P["final_g"])


def _block_diag(w):
    bw = w.shape[1]
    per = V7X_MXU_DIM // bw
    nb = w.shape[0] // per
    w4 = w.reshape(nb, per, bw, 1, bw)
    on_diag = jnp.eye(per, dtype=w.dtype).reshape(1, per, 1, per, 1)
    return (w4 * on_diag).reshape(nb, V7X_MXU_DIM, V7X_MXU_DIM)


def _pitch_padded(w):
    pad = jnp.zeros((w.shape[0], W_PITCH_PAD), BF16)
    return jnp.concatenate([w.astype(BF16), pad], axis=1)


def _prep_weights_kernel(wt_ref, wup_ref, wdn_ref, wa_ref, wb_ref, wo_ref,
                         w5_ref, w2_ref, wg_ref, wup_o, wdn_o, wa_o, wb_o, wo_o):
    wt = wt_ref[...]
    n_gate = 2 * N_HEADS
    w5_ref[...] = _pitch_padded(wt[:N_W5].T)
    w2_ref[...] = _pitch_padded(wt[N_W5 + n_gate:].T)
    g = jnp.concatenate([wt[N_W5:N_W5 + n_gate],
                         jnp.zeros((V7X_LANES - n_gate, wt.shape[1]), F32)], axis=0).T
    head_lane = lax.broadcasted_iota(jnp.int32, g.shape, 1) < N_HEADS
    ig = jnp.where(head_lane, g, 0.0)
    fg = jnp.where(head_lane, pltpu.roll(g, V7X_LANES - N_HEADS, axis=1), 0.0)
    wg_ref[...] = jnp.concatenate([ig, fg], axis=1).astype(BF16)

    wup_o[...] = wup_ref[0].astype(BF16)
    wdn_o[...] = _pitch_padded(wdn_ref[0])
    wa_o[...] = _pitch_padded(wa_ref[0])
    wb_o[...] = _pitch_padded(wb_ref[0])
    wo_o[...] = _pitch_padded(wo_ref[0])


def _prep_weights(w_in, w_up, w_down, w_branch_a, w_branch_b, w_out):
    n_in = w_in.shape[2]
    assert n_in == N_W5 + 2 * N_HEADS + N_W2
    steps = D_MODEL // V7X_LANES
    rows_dn = D_FF // steps
    wt = jnp.transpose(w_in[0])
    slab3 = lambda r, w: pl.BlockSpec((1, r, w), lambda i: (0, i, 0))
    slab = lambda r, w: pl.BlockSpec((r, w), lambda i: (i, 0))
    sq_pad = D_MODEL + W_PITCH_PAD
    bf = lambda r, w: jax.ShapeDtypeStruct((r, w), BF16)
    return pl.pallas_call(
        _prep_weights_kernel,
        grid=(steps,),
        in_specs=[pl.BlockSpec((n_in, V7X_LANES), lambda i: (0, i)),
                  slab3(V7X_LANES, 2 * D_FF), slab3(rows_dn, D_MODEL),
                  slab3(V7X_LANES, D_MODEL), slab3(V7X_LANES, D_MODEL), slab3(V7X_LANES, D_MODEL)],
        out_specs=[slab(V7X_LANES, N_W5 + W_PITCH_PAD), slab(V7X_LANES, N_W2 + W_PITCH_PAD),
                   slab(V7X_LANES, GATE_W), slab(V7X_LANES, 2 * D_FF), slab(rows_dn, sq_pad),
                   slab(V7X_LANES, sq_pad), slab(V7X_LANES, sq_pad), slab(V7X_LANES, sq_pad)],
        out_shape=[bf(D_MODEL, N_W5 + W_PITCH_PAD), bf(D_MODEL, N_W2 + W_PITCH_PAD),
                   bf(D_MODEL, GATE_W), bf(D_MODEL, 2 * D_FF), bf(D_FF, sq_pad),
                   bf(D_MODEL, sq_pad), bf(D_MODEL, sq_pad), bf(D_MODEL, sq_pad)],
        compiler_params=_params(1, 48),
        name="prep_weights",
    )(wt, w_up, w_down, w_branch_a, w_branch_b, w_out)


def _run_long_group(x3, state, P, TS, side=None):
    NSEQ, L, _ = x3.shape
    x2 = x3.reshape(NSEQ * L, D_MODEL)
    x1, conv1, h1, c1, n1, m1, *side_out = _mixer(x2, state[:5], P, NSEQ, L, TS, side)
    y, ffn1 = _ffn(x1, state[5], P, NSEQ, L, 1, TS, shared=True)
    return y.reshape(NSEQ, L, D_MODEL), (conv1, h1, c1, n1, m1, ffn1), side_out


def _short_group_front(x2, state, P, NSEQ, L, tm, ns):
    conv0, h0, c0, n0, m0, _ = state
    u, q, k, v, o, ga, gb, gt = _proj(x2, P, tm)
    ha, conv1, h1 = _lru(u, conv0, h0, P, NSEQ, L, ns, L)
    return SideIn(q, k, v, gt, o, c0, n0, m0), (ha, ga, gb, conv1, h1)


def _short_group_back(x2, front, side_out, ffn0, P, NSEQ, L, tm, ns):
    ha, ga, gb, conv1, h1 = front
    hb, c1, n1, m1 = side_out
    x1 = _post(x2, ha, hb, ga, gb, P, tm)
    y, ffn1 = _ffn(x1, ffn0, P, NSEQ, L, ns, L)
    return y.reshape(NSEQ, L, D_MODEL), (conv1, h1, c1, n1, m1, ffn1)


def kernel(x_prompt, x_sample, state_lru_conv, state_lru_h, state_mlstm_C, state_mlstm_n,
           state_mlstm_m, state_ffn_conv, meta_tokens, norm1_g, w_in, b_in, lru_conv_w,
           lru_conv_b, lru_w_r, lru_b_r, lru_w_i, lru_b_i, lru_lambda, mlstm_head_g,
           w_branch_a, w_branch_b, w_out, norm2_g, w_up, ffn_conv_w, ffn_conv_b, w_down, final_g):
    assert w_in.shape[0] == 1, "single-layer trunk"
    b0 = b_in[0]
    gate_pad = jnp.zeros((V7X_LANES - N_HEADS,), b0.dtype)
    row = lambda a: a.reshape(1, -1).astype(F32)
    w5, w2, w_gate, w_up_b, w_down_b, w_a_b, w_b_b, w_o_b = _prep_weights(
        w_in, w_up, w_down, w_branch_a, w_branch_b, w_out)
    P = {
        "norm1_g": row(norm1_g[0]),
        "w5": w5,
        "w2": w2,
        "b_main": row(jnp.concatenate([b0[:N_W5], b0[N_W5 + 2 * N_HEADS:]])),
        "w_gate": w_gate,
        "b_gate": row(jnp.concatenate([b0[N_W5:N_W5 + N_HEADS], gate_pad,
                                       b0[N_W5 + N_HEADS:N_W5 + 2 * N_HEADS], gate_pad])),
        "lru_conv_w": lru_conv_w[0],
        "lru_conv_b": row(lru_conv_b[0]),
        "w_r": _block_diag(lru_w_r[0]).astype(BF16),
        "lru_b_r": row(lru_b_r[0]),
        "w_i": _block_diag(lru_w_i[0]).astype(BF16),
        "lru_b_i": row(lru_b_i[0]),
        "lru_lambda": row(lru_lambda[0]),
        "mlstm_head_g": row(mlstm_head_g[0]),
        "w_branch_a": w_a_b,
        "w_branch_b": w_b_b,
        "w_out": w_o_b,
        "norm2_g": row(norm2_g[0]),
        "w_up": w_up_b,
        "ffn_conv_w": ffn_conv_w[0],
        "ffn_conv_b": row(ffn_conv_b[0]),
        "w_down": w_down_b,
        "final_g": row(final_g),
    }

    def pack_state(conv, h, c, n, m, ffn):
        nseq = h.shape[0]
        m_pad = jnp.pad(m.astype(F32)[:, None, :], ((0, 0), (0, 0), (0, V7X_LANES - N_HEADS)))
        return (conv.astype(F32), h.astype(F32).reshape(nseq, 1, D_LRU), c.astype(F32),
                n.astype(F32), m_pad, ffn.astype(F32))

    def unpack_state(st):
        conv, h, c, n, m, ffn = st
        return (conv[None], h.reshape(1, -1, D_LRU), c[None], n[None],
                m[:, 0, :N_HEADS][None], ffn[None])

    zero = pack_state(jnp.zeros((1, LRU_CONV - 1, D_LRU), F32), jnp.zeros((1, D_LRU), F32),
                      jnp.zeros((1, N_HEADS, D_HEAD, D_HEAD), F32),
                      jnp.zeros((1, N_HEADS, D_HEAD), F32), jnp.zeros((1, N_HEADS), F32),
                      jnp.zeros((1, FFN_CONV - 1, 2 * D_FF), F32))
    _, meta_state, _ = _run_long_group(meta_tokens[None].astype(F32), zero, P, N_META)
    sample_state0 = pack_state(state_lru_conv[0], state_lru_h[0], state_mlstm_C[0],
                               state_mlstm_n[0], state_mlstm_m[0], state_ffn_conv[0])
    n_sample, l_sample, _ = x_sample.shape
    xs2 = x_sample.reshape(n_sample * l_sample, D_MODEL)
    short = dict(NSEQ=n_sample, L=l_sample, tm=SHORT_ROWS, ns=SHORT_ROWS // l_sample)
    side_in, front = _short_group_front(xs2, sample_state0, P, **short)
    y_prompt, prompt_state, side_out = _run_long_group(x_prompt, meta_state, P, LONG_TS,
                                                       side=(side_in, l_sample))
    y_sample, sample_state = _short_group_back(xs2, front, side_out, sample_state0[5], P, **short)
    return (y_prompt, y_sample) + unpack_state(prompt_state) + unpack_state(sample_state)
```

```python
import functools
from typing import Any, NamedTuple

import jax
import jax.numpy as jnp
from jax import lax
from jax.experimental import pallas as pl
from jax.experimental.pallas import tpu as pltpu

F32 = jnp.float32
BF16 = jnp.bfloat16

D_MODEL = 1024
D_LRU = 1024
LRU_CONV = 4
LRU_C = 8.0
N_HEADS = 4
D_HEAD = 256
D_FF = 2816
FFN_CONV = 3
N_META = 16
EPS = 1e-6

V7X_LANES = 128
V7X_SUBLANES = 8
V7X_MXU_DIM = 256
NEG_BIG = -1e30

LONG_TS = 256
SHORT_ROWS = 256
FFN_CHUNK = 256

N_MAIN = 7 * D_MODEL
N_W5 = 5 * D_MODEL
N_W2 = 2 * D_MODEL
W_PITCH_PAD = V7X_LANES
GATE_W = 2 * V7X_LANES
COL_U, COL_Q, COL_K, COL_V, COL_O, COL_GA, COL_GB = (j * D_MODEL for j in range(7))


def _resident(shape):
    return pl.BlockSpec(shape, lambda *_: (0,) * len(shape), pipeline_mode=pl.Buffered(1))


def _params(n_grid, vmem_mb, flags=None):
    return pltpu.CompilerParams(
        dimension_semantics=("arbitrary",) * n_grid,
        vmem_limit_bytes=vmem_mb * 1024 * 1024,
        flags=flags,
    )


def _rms(x, g):
    ms = jnp.mean(x * x, axis=-1, keepdims=True)
    return x * lax.rsqrt(ms + EPS) * g


def _in_proj(xn, w5_ref, w2_ref, b_ref, col, width):
    if col < N_W5:
        w = w5_ref[:, col:col + width]
    else:
        w = w2_ref[:, col - N_W5:col - N_W5 + width]
    return jnp.dot(xn, w, preferred_element_type=F32) + b_ref[:, col:col + width]


def _in_proj_specs():
    return [_resident((1, D_MODEL)), _resident((D_MODEL, N_W5 + W_PITCH_PAD)),
            _resident((D_MODEL, N_W2 + W_PITCH_PAD)),
            _resident((1, N_MAIN)), _resident((D_MODEL, GATE_W)), _resident((1, GATE_W))]


def _in_proj_weights(P):
    return (P["norm1_g"], P["w5"], P["w2"], P["b_main"], P["w_gate"], P["b_gate"])


def _conv_init(ext_ref, hist0_ref, taps):
    pad, hist = V7X_SUBLANES, taps - 1
    for g in range(ext_ref.shape[1]):
        ls = slice(g * V7X_LANES, (g + 1) * V7X_LANES)
        ext_ref[:, g, pad - hist:pad, :] = hist0_ref[:, :, ls]


def _causal_conv(ext_ref, x3, cw_ref, cb_ref, hist_out_ref, col0=0):
    taps = cw_ref.shape[0]
    ts = x3.shape[1]
    pad, hist = V7X_SUBLANES, taps - 1
    outs = []
    for k in range(x3.shape[2] // V7X_LANES):
        g = col0 // V7X_LANES + k
        ls = slice(g * V7X_LANES, (g + 1) * V7X_LANES)
        xg = x3[:, :, k * V7X_LANES:(k + 1) * V7X_LANES]
        ext_ref[:, g, pad:, :] = xg
        acc = cb_ref[:, ls] + cw_ref[taps - 1:taps, ls] * xg
        for j in range(hist):
            acc = acc + cw_ref[j:j + 1, ls] * ext_ref[:, g, pad - hist + j:pad - hist + j + ts, :]
        outs.append(acc)
        new_hist = ext_ref[:, g, pad + ts - hist:pad + ts, :]
        ext_ref[:, g, pad - hist:pad, :] = new_hist
        hist_out_ref[:, :, ls] = new_hist
    return jnp.concatenate(outs, axis=-1)


def _lru_body(NS, TS, u2, cw_ref, cb_ref, wr_ref, br, wi_ref, bi, lam, ext_ref, h_ref, convo_ref,
              ho_ref):
    R = NS * TS
    C = D_LRU
    SB = V7X_SUBLANES
    uc2 = _causal_conv(ext_ref, u2.reshape(NS, TS, C), cw_ref, cb_ref, convo_ref).reshape(R, C)
    ucb = uc2.astype(BF16)

    def block_diag(w_ref):
        W = V7X_MXU_DIM
        return jnp.concatenate(
            [jnp.dot(ucb[:, g * W:(g + 1) * W], w_ref[g], preferred_element_type=F32)
             for g in range(C // W)], axis=1)

    r = jax.nn.sigmoid(block_diag(wr_ref) + br)
    i = jax.nn.sigmoid(block_diag(wi_ref) + bi)
    log_a = -LRU_C * r * jax.nn.softplus(-lam)
    a = jnp.exp(log_a)
    hh = jnp.sqrt(-jnp.tanh(log_a) * (a * a + 1.0)) * (i * uc2)

    a = a.reshape(R // SB, SB, C)
    hh = hh.reshape(R // SB, SB, C)
    sub = lax.broadcasted_iota(jnp.int32, (1, SB, C), 1)
    for d in (1, 2, 4):
        keep = sub >= d
        a_sh = pltpu.roll(a, d, axis=1)
        h_sh = pltpu.roll(hh, d, axis=1)
        hh = hh + a * jnp.where(keep, h_sh, 0.0)
        a = a * jnp.where(keep, a_sh, 1.0)

    nb = TS // SB
    a = a.reshape(NS, nb, SB, C)
    hh = hh.reshape(NS, nb, SB, C)
    h = jnp.broadcast_to(h_ref[...], (NS, SB, C))
    blocks = []
    for j in range(nb):
        hj = hh[:, j] + a[:, j] * h
        blocks.append(hj)
        h = jnp.broadcast_to(hj[:, SB - 1:, :], (NS, SB, C))
    h_ref[...] = h[:, 0:1, :]
    ho_ref[...] = h[:, 0:1, :]
    return jnp.concatenate(blocks, axis=1).reshape(R, C)


def _seg_scan(x, tpos, TS, op, ident):
    d = 1
    while d < TS:
        sh = pltpu.roll(x, d, axis=0)
        x = op(x, jnp.where(tpos >= d, sh, ident))
        d *= 2
    return x


def _pad_rows(x, rows):
    if x.shape[0] >= rows:
        return x
    return jnp.concatenate([x, jnp.zeros((rows - x.shape[0],) + x.shape[1:], x.dtype)], axis=0)


def _mlstm_body(NS, TS, get_qkvo, gt, hg_ref, st_in, st_out):
    R = NS * TS
    RC = max(R, V7X_LANES)
    LN = V7X_LANES
    shift = TS.bit_length() - 1

    ig4 = gt[:, :LN]
    lf4 = jax.nn.log_sigmoid(gt[:, LN:])
    tpos = lax.broadcasted_iota(jnp.int32, (R, LN), 0) & (TS - 1)
    b4 = _seg_scan(lf4, tpos, TS, jnp.add, 0.0)
    c4 = ig4 - b4
    cmax4 = _seg_scan(c4, tpos, TS, jnp.maximum, -jnp.inf)
    c_in, n_in, m_in = st_in
    c_out, n_out, m_out = st_out
    m_prev = [m_in[j] for j in range(NS)]
    m_rows = jnp.concatenate([jnp.broadcast_to(m, (TS, LN)) for m in m_prev], axis=0)
    big_m4 = jnp.maximum(cmax4, m_rows)
    e4 = jnp.exp(m_rows - big_m4)
    dinv4 = jnp.exp(-(b4 + big_m4))

    decay4, wk_parts = [], []
    for j in range(NS):
        b_last = b4[(j + 1) * TS - 1:(j + 1) * TS, :]
        g4 = b_last + c4[j * TS:(j + 1) * TS, :]
        mn = jnp.maximum(b_last + m_prev[j], jnp.max(g4, axis=0, keepdims=True))
        decay4.append(jnp.exp(b_last + m_prev[j] - mn))
        wk_parts.append(jnp.exp(g4 - mn))
        m_out[j] = mn
    wk4 = jnp.concatenate(wk_parts, axis=0)

    ri = lax.broadcasted_iota(jnp.int32, (R, RC), 0)
    ci = lax.broadcasted_iota(jnp.int32, (R, RC), 1)
    eye = ri == ci
    if NS == 1:
        causal = ci <= ri
    else:
        causal = (ci <= ri) & ((ri >> shift) == (ci >> shift))
    seq_of_row = lax.broadcasted_iota(jnp.int32, (R, D_HEAD), 0) >> shift

    outs = []
    for h in range(N_HEADS):
        sl = slice(h * D_HEAD, (h + 1) * D_HEAD)
        qh, kh, vh, oh = get_qkvo(h)
        kh_p = _pad_rows(kh, RC)
        vh_p = _pad_rows(vh, RC)
        c_c = c4[:, h:h + 1]
        big_m_c = big_m4[:, h:h + 1]
        e_c = e4[:, h:h + 1]
        dinv_c = dinv4[:, h:h + 1]
        wk_c = wk4[:, h:h + 1]

        qk = lax.dot_general(qh, kh_p, (((1,), (1,)), ((), ())), preferred_element_type=F32)
        c_r = jnp.sum(jnp.where(eye, c_c, 0.0), axis=0, keepdims=True)
        w = jnp.exp(jnp.where(causal, c_r - big_m_c, NEG_BIG))
        s = qk * w
        den = jnp.sum(s, axis=1, keepdims=True)
        num = jnp.dot(s.astype(BF16), vh_p, preferred_element_type=F32)

        qf = qh.astype(F32)
        if NS == 1:
            q_c = jnp.dot(qh, c_in[0, h].astype(BF16), preferred_element_type=F32)
            n_rows = n_in[0, h:h + 1, :]
        else:
            q_c = jnp.zeros((R, D_HEAD), F32)
            n_rows = jnp.zeros((R, D_HEAD), F32)
            for j in range(NS):
                mine = seq_of_row == j
                q_c = jnp.where(mine, jnp.dot(qh, c_in[j, h].astype(BF16),
                                              preferred_element_type=F32), q_c)
                n_rows = jnp.where(mine, n_in[j, h:h + 1, :], n_rows)
        q_n = jnp.sum(qf * n_rows, axis=1, keepdims=True)
        num = num + e_c * q_c
        den = den + e_c * q_n
        hh = num * (1.0 / jnp.maximum(jnp.abs(den), dinv_c))
        hh = hh * lax.rsqrt(jnp.mean(hh * hh, axis=1, keepdims=True) + EPS)
        outs.append(((hh * hg_ref[:, sl]) * jax.nn.sigmoid(oh)).astype(BF16))

        kw = kh.astype(F32) * wk_c
        for j in range(NS):
            kwj = kw if NS == 1 else jnp.where(seq_of_row == j, kw, 0.0)
            upd = lax.dot_general(_pad_rows(kwj, RC).astype(BF16), vh_p,
                                  (((0,), (0,)), ((), ())), preferred_element_type=F32)
            dec = decay4[j][:, h:h + 1]
            c_out[j, h] = dec * c_in[j, h] + upd
            n_out[j, h:h + 1, :] = (dec * n_in[j, h:h + 1, :]
                                    + jnp.sum(kwj, axis=0, keepdims=True))
    return outs


def _merge_out(x, ha, hb, ga, gb, wa_ref, wb_ref, wo_ref):
    pa = jnp.dot(ha, wa_ref[:, :D_MODEL], preferred_element_type=F32)
    pb = jnp.dot(hb, wb_ref[:, :D_MODEL], preferred_element_type=F32)
    merged = jax.nn.sigmoid(ga) * pa + jax.nn.sigmoid(gb) * pb
    return x + jnp.dot(merged.astype(BF16), wo_ref[:, :D_MODEL], preferred_element_type=F32)


class MixerIn(NamedTuple):
    x: Any
    conv0: Any
    h0: Any
    c0: Any
    n0: Any
    m0: Any
    g: Any
    w5: Any
    w2: Any
    b: Any
    wg: Any
    bg: Any
    cw: Any
    cb: Any
    wr: Any
    br: Any
    wi: Any
    bi: Any
    lam: Any
    hg: Any
    wa: Any
    wb: Any
    wo: Any


class MixerOut(NamedTuple):
    x1: Any
    conv: Any
    h: Any
    c: Any
    n: Any
    m: Any


class MixerScratch(NamedTuple):
    ext: Any
    h: Any
    c: Any
    n: Any
    m: Any


class SideIn(NamedTuple):
    q: Any
    k: Any
    v: Any
    gt: Any
    o: Any
    c0: Any
    n0: Any
    m0: Any


class SideOut(NamedTuple):
    hb: Any
    c: Any
    n: Any
    m: Any


def _split_refs(refs, *kinds):
    out, pos = [], 0
    for kind in kinds:
        n = len(kind._fields)
        out.append(kind(*refs[pos:pos + n]))
        pos += n
    assert pos == len(refs)
    return out


def _mixer_step(TS, i: MixerIn, o: MixerOut, s: MixerScratch, side_job=None):
    ti = pl.program_id(1)

    @pl.when(ti == 0)
    def _():
        _conv_init(s.ext, i.conv0, LRU_CONV)
        s.h[...] = i.h0[...]
        s.c[...] = i.c0[...]
        s.n[...] = i.n0[...]
        s.m[...] = i.m0[...]

    if side_job is not None:
        side_job()

    x = i.x[...]
    xn = _rms(x, i.g[...]).astype(BF16)
    proj = functools.partial(_in_proj, xn, i.w5, i.w2, i.b)

    hs = _lru_body(1, TS, proj(COL_U, D_LRU), i.cw, i.cb, i.wr, i.br[...],
                   i.wi, i.bi[...], i.lam[...], s.ext, s.h, o.conv, o.h)

    def get_qkvo(h):
        off = h * D_HEAD
        q = (proj(COL_Q + off, D_HEAD) * (D_HEAD ** -0.5)).astype(BF16)
        return (q, proj(COL_K + off, D_HEAD).astype(BF16), proj(COL_V + off, D_HEAD).astype(BF16),
                proj(COL_O + off, D_HEAD))

    gt = jnp.dot(xn, i.wg[...], preferred_element_type=F32) + i.bg[...]
    state = (s.c, s.n, s.m)
    hb = jnp.concatenate(_mlstm_body(1, TS, get_qkvo, gt, i.hg, state, state), axis=1)
    o.x1[...] = _merge_out(x, hs.astype(BF16), hb, proj(COL_GA, D_MODEL), proj(COL_GB, D_MODEL),
                           i.wa, i.wb, i.wo)

    @pl.when(ti == pl.num_programs(1) - 1)
    def _():
        o.c[...] = s.c[...]
        o.n[...] = s.n[...]
        o.m[...] = s.m[...]


def _mixer_kernel(TS, *refs):
    _mixer_step(TS, *_split_refs(refs, MixerIn, MixerOut, MixerScratch))


def _mixer_side_kernel(TS, SIDE_NS, SIDE_TS, *refs):
    i, si, o, so, s = _split_refs(refs, MixerIn, SideIn, MixerOut, SideOut, MixerScratch)

    def get_qkvo(h):
        sl = slice(h * D_HEAD, (h + 1) * D_HEAD)
        return si.q[:, sl], si.k[:, sl], si.v[:, sl], si.o[:, sl]

    def side_job():
        outs = _mlstm_body(SIDE_NS, SIDE_TS, get_qkvo, si.gt[...], i.hg,
                           (si.c0, si.n0, si.m0), (so.c, so.n, so.m))
        for h, out in enumerate(outs):
            so.hb[:, h * D_HEAD:(h + 1) * D_HEAD] = out

    _mixer_step(TS, i, o, s, side_job)


def _lru_weight_specs():
    nb = D_LRU // V7X_MXU_DIM
    return [_resident((LRU_CONV, D_LRU)), _resident((1, D_LRU)),
            _resident((nb, V7X_MXU_DIM, V7X_MXU_DIM)), _resident((1, D_LRU)),
            _resident((nb, V7X_MXU_DIM, V7X_MXU_DIM)), _resident((1, D_LRU)),
            _resident((1, D_LRU))]


def _lru_weights(P):
    return (P["lru_conv_w"], P["lru_conv_b"], P["w_r"], P["lru_b_r"], P["w_i"], P["lru_b_i"],
            P["lru_lambda"])


def _state_specs(NS, shared=False):
    def spec(*tail):
        zeros = (0,) * len(tail)
        index = (lambda s, t: (0,) + zeros) if shared else (lambda s, t: (s,) + zeros)
        return pl.BlockSpec((NS,) + tail, index)

    return [spec(LRU_CONV - 1, D_LRU), spec(1, D_LRU), spec(N_HEADS, D_HEAD, D_HEAD),
            spec(N_HEADS, D_HEAD), spec(1, V7X_LANES)]


def _state_shapes(NSEQ):
    return [jax.ShapeDtypeStruct((NSEQ, LRU_CONV - 1, D_LRU), F32),
            jax.ShapeDtypeStruct((NSEQ, 1, D_LRU), F32),
            jax.ShapeDtypeStruct((NSEQ, N_HEADS, D_HEAD, D_HEAD), F32),
            jax.ShapeDtypeStruct((NSEQ, N_HEADS, D_HEAD), F32),
            jax.ShapeDtypeStruct((NSEQ, 1, V7X_LANES), F32)]


def _mixer(x2, state, P, NSEQ, L, TS, side=None):
    NT = L // TS
    rows = pl.BlockSpec((TS, D_MODEL), lambda s, t: (s * NT + t, 0))
    wsq = _resident((D_MODEL, D_MODEL + W_PITCH_PAD))
    in_specs = ([rows] + _state_specs(1, shared=True) + _in_proj_specs() + _lru_weight_specs()
                + [_resident((1, D_MODEL)), wsq, wsq, wsq])
    operands = (x2, *state, *_in_proj_weights(P), *_lru_weights(P), P["mlstm_head_g"],
                P["w_branch_a"], P["w_branch_b"], P["w_out"])
    out_specs = [rows] + _state_specs(1)
    out_shape = [jax.ShapeDtypeStruct((NSEQ * L, D_MODEL), F32)] + _state_shapes(NSEQ)
    body = functools.partial(_mixer_kernel, TS)
    if side is not None:
        side_in, side_len = side
        n_side = side_in.c0.shape[0]
        side_ns = n_side // (NSEQ * NT)
        assert side_ns * NSEQ * NT == n_side
        step = lambda s, t: s * NT + t
        srows = lambda w: pl.BlockSpec((side_ns * side_len, w), lambda s, t: (step(s, t), 0))
        sstate = [pl.BlockSpec((side_ns, N_HEADS, D_HEAD, D_HEAD), lambda s, t: (step(s, t), 0, 0, 0)),
                  pl.BlockSpec((side_ns, N_HEADS, D_HEAD), lambda s, t: (step(s, t), 0, 0)),
                  pl.BlockSpec((side_ns, 1, V7X_LANES), lambda s, t: (step(s, t), 0, 0))]
        in_specs = in_specs + [srows(D_MODEL)] * 3 + [srows(GATE_W), srows(D_MODEL)] + sstate
        operands = operands + tuple(side_in)
        out_specs = out_specs + [srows(D_MODEL)] + sstate
        out_shape = (out_shape + [jax.ShapeDtypeStruct((n_side * side_len, D_MODEL), BF16)]
                     + _state_shapes(n_side)[2:])
        body = functools.partial(_mixer_side_kernel, TS, side_ns, side_len)
    return pl.pallas_call(
        body,
        grid=(NSEQ, NT),
        in_specs=in_specs,
        out_specs=out_specs,
        out_shape=out_shape,
        scratch_shapes=[pltpu.VMEM((1, D_LRU // V7X_LANES, V7X_SUBLANES + TS, V7X_LANES), F32),
                        pltpu.VMEM((1, 1, D_LRU), F32),
                        pltpu.VMEM((1, N_HEADS, D_HEAD, D_HEAD), F32),
                        pltpu.VMEM((1, N_HEADS, D_HEAD), F32),
                        pltpu.VMEM((1, 1, V7X_LANES), F32)],
        compiler_params=_params(2, 56),
        name="mixer",
    )(*operands)


def _proj_kernel(x_ref, g_ref, w5_ref, w2_ref, b_ref, wg_ref, bg_ref,
                 u_ref, q_ref, k_ref, v_ref, o_ref, ga_ref, gb_ref, gt_ref):
    xn = _rms(x_ref[...], g_ref[...]).astype(BF16)
    proj = functools.partial(_in_proj, xn, w5_ref, w2_ref, b_ref)
    u_ref[...] = proj(COL_U, D_MODEL)
    q_ref[...] = (proj(COL_Q, D_MODEL) * (D_HEAD ** -0.5)).astype(BF16)
    k_ref[...] = proj(COL_K, D_MODEL).astype(BF16)
    v_ref[...] = proj(COL_V, D_MODEL).astype(BF16)
    o_ref[...] = proj(COL_O, D_MODEL)
    ga_ref[...] = proj(COL_GA, D_MODEL)
    gb_ref[...] = proj(COL_GB, D_MODEL)
    gt_ref[...] = jnp.dot(xn, wg_ref[...], preferred_element_type=F32) + bg_ref[...]


def _proj(x2, P, tm):
    M = x2.shape[0]
    row = lambda w: pl.BlockSpec((tm, w), lambda i: (i, 0))
    f32o = jax.ShapeDtypeStruct((M, D_MODEL), F32)
    bf16o = jax.ShapeDtypeStruct((M, D_MODEL), BF16)
    return pl.pallas_call(
        _proj_kernel,
        grid=(M // tm,),
        in_specs=[row(D_MODEL)] + _in_proj_specs(),
        out_specs=[row(D_MODEL)] * 7 + [row(GATE_W)],
        out_shape=[f32o, bf16o, bf16o, bf16o, f32o, f32o, f32o,
                   jax.ShapeDtypeStruct((M, GATE_W), F32)],
        compiler_params=_params(1, 48),
        name="proj",
    )(x2, *_in_proj_weights(P))


def _lru_kernel(NS, TS, u_ref, conv0_ref, h0_ref, cw_ref, cb_ref, wr_ref, br_ref, wi_ref,
                bi_ref, lam_ref, ha_ref, convo_ref, ho_ref, ext_ref, h_s):
    @pl.when(pl.program_id(1) == 0)
    def _():
        _conv_init(ext_ref, conv0_ref, LRU_CONV)
        h_s[...] = h0_ref[...]

    hs = _lru_body(NS, TS, u_ref[...], cw_ref, cb_ref, wr_ref, br_ref[...], wi_ref,
                   bi_ref[...], lam_ref[...], ext_ref, h_s, convo_ref, ho_ref)
    ha_ref[...] = hs.astype(BF16)


def _lru(u, conv0, h0, P, NSEQ, L, NS, TS):
    R = NS * TS
    NT = L // TS
    rows = pl.BlockSpec((R, D_LRU), lambda s, t: (s * NT + t, 0))
    st = _state_specs(NS)[:2]
    return pl.pallas_call(
        functools.partial(_lru_kernel, NS, TS),
        grid=(NSEQ // NS, NT),
        in_specs=[rows] + st + _lru_weight_specs(),
        out_specs=[rows] + st,
        out_shape=[jax.ShapeDtypeStruct((NSEQ * L, D_LRU), BF16)] + _state_shapes(NSEQ)[:2],
        scratch_shapes=[pltpu.VMEM((NS, D_LRU // V7X_LANES, V7X_SUBLANES + TS, V7X_LANES), F32),
                        pltpu.VMEM((NS, 1, D_LRU), F32)],
        compiler_params=_params(2, 48),
        name="lru",
    )(u, conv0, h0, *_lru_weights(P))


def _post_kernel(x_ref, ha_ref, hb_ref, ga_ref, gb_ref, wa_ref, wb_ref, wo_ref, x1_ref):
    x1_ref[...] = _merge_out(x_ref[...], ha_ref[...], hb_ref[...], ga_ref[...], gb_ref[...],
                             wa_ref, wb_ref, wo_ref)


def _post(x2, ha, hb, ga, gb, P, tm):
    M = x2.shape[0]
    row = pl.BlockSpec((tm, D_MODEL), lambda i: (i, 0))
    wsp = _resident((D_MODEL, D_MODEL + W_PITCH_PAD))
    return pl.pallas_call(
        _post_kernel,
        grid=(M // tm,),
        in_specs=[row] * 5 + [wsp] * 3,
        out_specs=row,
        out_shape=jax.ShapeDtypeStruct((M, D_MODEL), F32),
        compiler_params=_params(1, 48),
        name="post",
    )(x2, ha, hb, ga, gb, P["w_branch_a"], P["w_branch_b"], P["w_out"])


def _ffn_kernel(NS, TS, x1_ref, st0_ref, g2_ref, wup_ref, cw_ref, cb_ref, wdn_ref, gf_ref,
                y_ref, sto_ref, ext_ref):
    R = NS * TS
    W = 2 * D_FF

    @pl.when(pl.program_id(1) == 0)
    def _():
        _conv_init(ext_ref, st0_ref, FFN_CONV)

    x1 = x1_ref[...]
    xn = _rms(x1, g2_ref[...]).astype(BF16)

    def up_conv(col0):
        up = jnp.dot(xn, wup_ref[:, col0:col0 + FFN_CHUNK], preferred_element_type=F32)
        return _causal_conv(ext_ref, up.reshape(NS, TS, FFN_CHUNK), cw_ref, cb_ref, sto_ref,
                            col0).reshape(R, FFN_CHUNK)

    act = jnp.concatenate(
        [(jax.nn.gelu(up_conv(D_FF + c)) * up_conv(c)).astype(BF16)
         for c in range(0, D_FF, FFN_CHUNK)], axis=1)
    x2 = x1 + jnp.dot(act, wdn_ref[:, :D_MODEL], preferred_element_type=F32)
    y_ref[...] = _rms(x2, gf_ref[...])


def _ffn(x1, st0, P, NSEQ, L, NS, TS, shared=False):
    R = NS * TS
    NT = L // TS
    W = 2 * D_FF
    rows = pl.BlockSpec((R, D_MODEL), lambda s, t: (s * NT + t, 0))
    stspec = pl.BlockSpec((NS, FFN_CONV - 1, W), lambda s, t: (s, 0, 0))
    st0spec = pl.BlockSpec((NS, FFN_CONV - 1, W), lambda s, t: (0, 0, 0)) if shared else stspec
    return pl.pallas_call(
        functools.partial(_ffn_kernel, NS, TS),
        grid=(NSEQ // NS, NT),
        in_specs=[rows, st0spec, _resident((1, D_MODEL)), _resident((D_MODEL, W)),
                  _resident((FFN_CONV, W)), _resident((1, W)),
                  _resident((D_FF, D_MODEL + W_PITCH_PAD)),
                  _resident((1, D_MODEL))],
        out_specs=[rows, stspec],
        out_shape=[jax.ShapeDtypeStruct((NSEQ * L, D_MODEL), F32),
                   jax.ShapeDtypeStruct((NSEQ, FFN_CONV - 1, W), F32)],
        scratch_shapes=[pltpu.VMEM((NS, W // V7X_LANES, V7X_SUBLANES + TS, V7X_LANES), F32)],
        compiler_params=_params(2, 56),
        name="ffn",
    )(x1, st0, P["norm2_g"], P["w_up"], P["ffn_conv_w"], P["ffn_conv_b"], P["w_down"],
      P["final_g"])


def _block_diag(w):
    bw = w.shape[1]
    per = V7X_MXU_DIM // bw
    nb = w.shape[0] // per
    w4 = w.reshape(nb, per, bw, 1, bw)
    on_diag = jnp.eye(per, dtype=w.dtype).reshape(1, per, 1, per, 1)
    return (w4 * on_diag).reshape(nb, V7X_MXU_DIM, V7X_MXU_DIM)


def _pitch_padded(w):
    pad = jnp.zeros((w.shape[0], W_PITCH_PAD), BF16)
    return jnp.concatenate([w.astype(BF16), pad], axis=1)


def _prep_weights_kernel(wt_ref, wup_ref, wdn_ref, wa_ref, wb_ref, wo_ref,
                         w5_ref, w2_ref, wg_ref, wup_o, wdn_o, wa_o, wb_o, wo_o):
    wt = wt_ref[...]
    n_gate = 2 * N_HEADS
    w5_ref[...] = _pitch_padded(wt[:N_W5].T)
    w2_ref[...] = _pitch_padded(wt[N_W5 + n_gate:].T)
    g = jnp.concatenate([wt[N_W5:N_W5 + n_gate],
                         jnp.zeros((V7X_LANES - n_gate, wt.shape[1]), F32)], axis=0).T
    head_lane = lax.broadcasted_iota(jnp.int32, g.shape, 1) < N_HEADS
    ig = jnp.where(head_lane, g, 0.0)
    fg = jnp.where(head_lane, pltpu.roll(g, V7X_LANES - N_HEADS, axis=1), 0.0)
    wg_ref[...] = jnp.concatenate([ig, fg], axis=1).astype(BF16)

    wup_o[...] = wup_ref[0].astype(BF16)
    wdn_o[...] = _pitch_padded(wdn_ref[0])
    wa_o[...] = _pitch_padded(wa_ref[0])
    wb_o[...] = _pitch_padded(wb_ref[0])
    wo_o[...] = _pitch_padded(wo_ref[0])


def _prep_weights(w_in, w_up, w_down, w_branch_a, w_branch_b, w_out):
    n_in = w_in.shape[2]
    assert n_in == N_W5 + 2 * N_HEADS + N_W2
    steps = D_MODEL // V7X_LANES
    rows_dn = D_FF // steps
    wt = jnp.transpose(w_in[0])
    slab3 = lambda r, w: pl.BlockSpec((1, r, w), lambda i: (0, i, 0))
    slab = lambda r, w: pl.BlockSpec((r, w), lambda i: (i, 0))
    sq_pad = D_MODEL + W_PITCH_PAD
    bf = lambda r, w: jax.ShapeDtypeStruct((r, w), BF16)
    return pl.pallas_call(
        _prep_weights_kernel,
        grid=(steps,),
        in_specs=[pl.BlockSpec((n_in, V7X_LANES), lambda i: (0, i)),
                  slab3(V7X_LANES, 2 * D_FF), slab3(rows_dn, D_MODEL),
                  slab3(V7X_LANES, D_MODEL), slab3(V7X_LANES, D_MODEL), slab3(V7X_LANES, D_MODEL)],
        out_specs=[slab(V7X_LANES, N_W5 + W_PITCH_PAD), slab(V7X_LANES, N_W2 + W_PITCH_PAD),
                   slab(V7X_LANES, GATE_W), slab(V7X_LANES, 2 * D_FF), slab(rows_dn, sq_pad),
                   slab(V7X_LANES, sq_pad), slab(V7X_LANES, sq_pad), slab(V7X_LANES, sq_pad)],
        out_shape=[bf(D_MODEL, N_W5 + W_PITCH_PAD), bf(D_MODEL, N_W2 + W_PITCH_PAD),
                   bf(D_MODEL, GATE_W), bf(D_MODEL, 2 * D_FF), bf(D_FF, sq_pad),
                   bf(D_MODEL, sq_pad), bf(D_MODEL, sq_pad), bf(D_MODEL, sq_pad)],
        compiler_params=_params(1, 48),
        name="prep_weights",
    )(wt, w_up, w_down, w_branch_a, w_branch_b, w_out)


def _run_long_group(x3, state, P, TS, side=None):
    NSEQ, L, _ = x3.shape
    x2 = x3.reshape(NSEQ * L, D_MODEL)
    x1, conv1, h1, c1, n1, m1, *side_out = _mixer(x2, state[:5], P, NSEQ, L, TS, side)
    y, ffn1 = _ffn(x1, state[5], P, NSEQ, L, 1, TS, shared=True)
    return y.reshape(NSEQ, L, D_MODEL), (conv1, h1, c1, n1, m1, ffn1), side_out


def _short_group_front(x2, state, P, NSEQ, L, tm, ns):
    conv0, h0, c0, n0, m0, _ = state
    u, q, k, v, o, ga, gb, gt = _proj(x2, P, tm)
    ha, conv1, h1 = _lru(u, conv0, h0, P, NSEQ, L, ns, L)
    return SideIn(q, k, v, gt, o, c0, n0, m0), (ha, ga, gb, conv1, h1)


def _short_group_back(x2, front, side_out, ffn0, P, NSEQ, L, tm, ns):
    ha, ga, gb, conv1, h1 = front
    hb, c1, n1, m1 = side_out
    x1 = _post(x2, ha, hb, ga, gb, P, tm)
    y, ffn1 = _ffn(x1, ffn0, P, NSEQ, L, ns, L)
    return y.reshape(NSEQ, L, D_MODEL), (conv1, h1, c1, n1, m1, ffn1)


def kernel(x_prompt, x_sample, state_lru_conv, state_lru_h, state_mlstm_C, state_mlstm_n,
           state_mlstm_m, state_ffn_conv, meta_tokens, norm1_g, w_in, b_in, lru_conv_w,
           lru_conv_b, lru_w_r, lru_b_r, lru_w_i, lru_b_i, lru_lambda, mlstm_head_g,
           w_branch_a, w_branch_b, w_out, norm2_g, w_up, ffn_conv_w, ffn_conv_b, w_down, final_g):
    assert w_in.shape[0] == 1, "single-layer trunk"
    b0 = b_in[0]
    gate_pad = jnp.zeros((V7X_LANES - N_HEADS,), b0.dtype)
    row = lambda a: a.reshape(1, -1).astype(F32)
    w5, w2, w_gate, w_up_b, w_down_b, w_a_b, w_b_b, w_o_b = _prep_weights(
        w_in, w_up, w_down, w_branch_a, w_branch_b, w_out)
    P = {
        "norm1_g": row(norm1_g[0]),
        "w5": w5,
        "w2": w2,
        "b_main": row(jnp.concatenate([b0[:N_W5], b0[N_W5 + 2 * N_HEADS:]])),
        "w_gate": w_gate,
        "b_gate": row(jnp.concatenate([b0[N_W5:N_W5 + N_HEADS], gate_pad,
                                       b0[N_W5 + N_HEADS:N_W5 + 2 * N_HEADS], gate_pad])),
        "lru_conv_w": lru_conv_w[0],
        "lru_conv_b": row(lru_conv_b[0]),
        "w_r": _block_diag(lru_w_r[0]).astype(BF16),
        "lru_b_r": row(lru_b_r[0]),
        "w_i": _block_diag(lru_w_i[0]).astype(BF16),
        "lru_b_i": row(lru_b_i[0]),
        "lru_lambda": row(lru_lambda[0]),
        "mlstm_head_g": row(mlstm_head_g[0]),
        "w_branch_a": w_a_b,
        "w_branch_b": w_b_b,
        "w_out": w_o_b,
        "norm2_g": row(norm2_g[0]),
        "w_up": w_up_b,
        "ffn_conv_w": ffn_conv_w[0],
        "ffn_conv_b": row(ffn_conv_b[0]),
        "w_down": w_down_b,
        "final_g": row(final_g),
    }

    def pack_state(conv, h, c, n, m, ffn):
        nseq = h.shape[0]
        m_pad = jnp.pad(m.astype(F32)[:, None, :], ((0, 0), (0, 0), (0, V7X_LANES - N_HEADS)))
        return (conv.astype(F32), h.astype(F32).reshape(nseq, 1, D_LRU), c.astype(F32),
                n.astype(F32), m_pad, ffn.astype(F32))

    def unpack_state(st):
        conv, h, c, n, m, ffn = st
        return (conv[None], h.reshape(1, -1, D_LRU), c[None], n[None],
                m[:, 0, :N_HEADS][None], ffn[None])

    zero = pack_state(jnp.zeros((1, LRU_CONV - 1, D_LRU), F32), jnp.zeros((1, D_LRU), F32),
                      jnp.zeros((1, N_HEADS, D_HEAD, D_HEAD), F32),
                      jnp.zeros((1, N_HEADS, D_HEAD), F32), jnp.zeros((1, N_HEADS), F32),
                      jnp.zeros((1, FFN_CONV - 1, 2 * D_FF), F32))
    _, meta_state, _ = _run_long_group(meta_tokens[None].astype(F32), zero, P, N_META)
    sample_state0 = pack_state(state_lru_conv[0], state_lru_h[0], state_mlstm_C[0],
                               state_mlstm_n[0], state_mlstm_m[0], state_ffn_conv[0])
    n_sample, l_sample, _ = x_sample.shape
    xs2 = x_sample.reshape(n_sample * l_sample, D_MODEL)
    short = dict(NSEQ=n_sample, L=l_sample, tm=SHORT_ROWS, ns=SHORT_ROWS // l_sample)
    side_in, front = _short_group_front(xs2, sample_state0, P, **short)
    y_prompt, prompt_state, side_out = _run_long_group(x_prompt, meta_state, P, LONG_TS,
                                                       side=(side_in, l_sample))
    y_sample, sample_state = _short_group_back(xs2, front, side_out, sample_state0[5], P, **short)
    return (y_prompt, y_sample) + unpack_state(prompt_state) + unpack_state(sample_state)
```

```python
import functools
from typing import Any, NamedTuple

import jax
import jax.numpy as jnp
from jax import lax
from jax.experimental import pallas as pl
from jax.experimental.pallas import tpu as pltpu

F32 = jnp.float32
BF16 = jnp.bfloat16

D_MODEL = 1024
D_LRU = 1024
LRU_CONV = 4
LRU_C = 8.0
N_HEADS = 4
D_HEAD = 256
D_FF = 2816
FFN_CONV = 3
N_META = 16
EPS = 1e-6

V7X_LANES = 128
V7X_SUBLANES = 8
V7X_MXU_DIM = 256
NEG_BIG = -1e30

LONG_TS = 256
SHORT_ROWS = 256
FFN_CHUNK = 256

N_MAIN = 7 * D_MODEL
N_W5 = 5 * D_MODEL
N_W2 = 2 * D_MODEL
W_PITCH_PAD = V7X_LANES
GATE_W = 2 * V7X_LANES
COL_U, COL_Q, COL_K, COL_V, COL_O, COL_GA, COL_GB = (j * D_MODEL for j in range(7))


def _resident(shape):
    return pl.BlockSpec(shape, lambda *_: (0,) * len(shape), pipeline_mode=pl.Buffered(1))


def _params(n_grid, vmem_mb, flags=None):
    return pltpu.CompilerParams(
        dimension_semantics=("arbitrary",) * n_grid,
        vmem_limit_bytes=vmem_mb * 1024 * 1024,
        flags=flags,
    )


def _rms(x, g):
    ms = jnp.mean(x * x, axis=-1, keepdims=True)
    return x * lax.rsqrt(ms + EPS) * g


def _in_proj(xn, w5_ref, w2_ref, b_ref, col, width):
    if col < N_W5:
        w = w5_ref[:, col:col + width]
    else:
        w = w2_ref[:, col - N_W5:col - N_W5 + width]
    return jnp.dot(xn, w, preferred_element_type=F32) + b_ref[:, col:col + width]


def _in_proj_specs():
    return [_resident((1, D_MODEL)), _resident((D_MODEL, N_W5 + W_PITCH_PAD)),
            _resident((D_MODEL, N_W2 + W_PITCH_PAD)),
            _resident((1, N_MAIN)), _resident((D_MODEL, GATE_W)), _resident((1, GATE_W))]


def _in_proj_weights(P):
    return (P["norm1_g"], P["w5"], P["w2"], P["b_main"], P["w_gate"], P["b_gate"])


def _conv_init(ext_ref, hist0_ref, taps):
    pad, hist = V7X_SUBLANES, taps - 1
    for g in range(ext_ref.shape[1]):
        ls = slice(g * V7X_LANES, (g + 1) * V7X_LANES)
        ext_ref[:, g, pad - hist:pad, :] = hist0_ref[:, :, ls]


def _causal_conv(ext_ref, x3, cw_ref, cb_ref, hist_out_ref, col0=0):
    taps = cw_ref.shape[0]
    ts = x3.shape[1]
    pad, hist = V7X_SUBLANES, taps - 1
    outs = []
    for k in range(x3.shape[2] // V7X_LANES):
        g = col0 // V7X_LANES + k
        ls = slice(g * V7X_LANES, (g + 1) * V7X_LANES)
        xg = x3[:, :, k * V7X_LANES:(k + 1) * V7X_LANES]
        ext_ref[:, g, pad:, :] = xg
        acc = cb_ref[:, ls] + cw_ref[taps - 1:taps, ls] * xg
        for j in range(hist):
            acc = acc + cw_ref[j:j + 1, ls] * ext_ref[:, g, pad - hist + j:pad - hist + j + ts, :]
        outs.append(acc)
        new_hist = ext_ref[:, g, pad + ts - hist:pad + ts, :]
        ext_ref[:, g, pad - hist:pad, :] = new_hist
        hist_out_ref[:, :, ls] = new_hist
    return jnp.concatenate(outs, axis=-1)


def _lru_body(NS, TS, u2, cw_ref, cb_ref, wr_ref, br, wi_ref, bi, lam, ext_ref, h_ref, convo_ref,
              ho_ref):
    R = NS * TS
    C = D_LRU
    SB = V7X_SUBLANES
    uc2 = _causal_conv(ext_ref, u2.reshape(NS, TS, C), cw_ref, cb_ref, convo_ref).reshape(R, C)
    ucb = uc2.astype(BF16)

    def block_diag(w_ref):
        W = V7X_MXU_DIM
        return jnp.concatenate(
            [jnp.dot(ucb[:, g * W:(g + 1) * W], w_ref[g], preferred_element_type=F32)
             for g in range(C // W)], axis=1)

    r = jax.nn.sigmoid(block_diag(wr_ref) + br)
    i = jax.nn.sigmoid(block_diag(wi_ref) + bi)
    log_a = -LRU_C * r * jax.nn.softplus(-lam)
    a = jnp.exp(log_a)
    hh = jnp.sqrt(-jnp.tanh(log_a) * (a * a + 1.0)) * (i * uc2)

    a = a.reshape(R // SB, SB, C)
    hh = hh.reshape(R // SB, SB, C)
    sub = lax.broadcasted_iota(jnp.int32, (1, SB, C), 1)
    for d in (1, 2, 4):
        keep = sub >= d
        a_sh = pltpu.roll(a, d, axis=1)
        h_sh = pltpu.roll(hh, d, axis=1)
        hh = hh + a * jnp.where(keep, h_sh, 0.0)
        a = a * jnp.where(keep, a_sh, 1.0)

    nb = TS // SB
    a = a.reshape(NS, nb, SB, C)
    hh = hh.reshape(NS, nb, SB, C)
    h = jnp.broadcast_to(h_ref[...], (NS, SB, C))
    blocks = []
    for j in range(nb):
        hj = hh[:, j] + a[:, j] * h
        blocks.append(hj)
        h = jnp.broadcast_to(hj[:, SB - 1:, :], (NS, SB, C))
    h_ref[...] = h[:, 0:1, :]
    ho_ref[...] = h[:, 0:1, :]
    return jnp.concatenate(blocks, axis=1).reshape(R, C)


def _seg_scan(x, tpos, TS, op, ident):
    d = 1
    while d < TS:
        sh = pltpu.roll(x, d, axis=0)
        x = op(x, jnp.where(tpos >= d, sh, ident))
        d *= 2
    return x


def _pad_rows(x, rows):
    if x.shape[0] >= rows:
        return x
    return jnp.concatenate([x, jnp.zeros((rows - x.shape[0],) + x.shape[1:], x.dtype)], axis=0)


class MlstmGates(NamedTuple):
    c4: Any
    big_m4: Any
    e4: Any
    dinv4: Any
    wk4: Any
    decay4: Any


def _mlstm_gates(NS, TS, gt, m_in, m_out):
    R = NS * TS
    LN = V7X_LANES
    ig4 = gt[:, :LN]
    lf4 = jax.nn.log_sigmoid(gt[:, LN:])
    tpos = lax.broadcasted_iota(jnp.int32, (R, LN), 0) & (TS - 1)
    b4 = _seg_scan(lf4, tpos, TS, jnp.add, 0.0)
    c4 = ig4 - b4
    cmax4 = _seg_scan(c4, tpos, TS, jnp.maximum, -jnp.inf)
    m_prev = [m_in[j] for j in range(NS)]
    m_rows = jnp.concatenate([jnp.broadcast_to(m, (TS, LN)) for m in m_prev], axis=0)
    big_m4 = jnp.maximum(cmax4, m_rows)
    e4 = jnp.exp(m_rows - big_m4)
    dinv4 = jnp.exp(-(b4 + big_m4))

    decay4, wk_parts = [], []
    for j in range(NS):
        b_last = b4[(j + 1) * TS - 1:(j + 1) * TS, :]
        g4 = b_last + c4[j * TS:(j + 1) * TS, :]
        mn = jnp.maximum(b_last + m_prev[j], jnp.max(g4, axis=0, keepdims=True))
        decay4.append(jnp.exp(b_last + m_prev[j] - mn))
        wk_parts.append(jnp.exp(g4 - mn))
        m_out[j] = mn
    return MlstmGates(c4, big_m4, e4, dinv4, jnp.concatenate(wk_parts, axis=0), decay4)


def _mlstm_heads(NS, TS, get_qkvo, gates: MlstmGates, hg_ref, st_in, st_out, first=None):
    R = NS * TS
    RC = max(R, V7X_LANES)
    shift = TS.bit_length() - 1
    c4, big_m4, e4, dinv4, wk4, decay4 = gates
    c_in, n_in, _ = st_in
    c_out, n_out, _ = st_out

    ri = lax.broadcasted_iota(jnp.int32, (R, RC), 0)
    ci = lax.broadcasted_iota(jnp.int32, (R, RC), 1)
    eye = ri == ci
    if NS == 1:
        causal = ci <= ri
    else:
        causal = (ci <= ri) & ((ri >> shift) == (ci >> shift))
    seq_of_row = lax.broadcasted_iota(jnp.int32, (R, D_HEAD), 0) >> shift

    outs = []
    ahead = get_qkvo(0) if first is None else first
    for h in range(N_HEADS):
        sl = slice(h * D_HEAD, (h + 1) * D_HEAD)
        qh, kh, vh, oh = ahead
        if h + 1 < N_HEADS:
            ahead = get_qkvo(h + 1)
        kh_p = _pad_rows(kh, RC)
        vh_p = _pad_rows(vh, RC)
        c_c = c4[:, h:h + 1]
        big_m_c = big_m4[:, h:h + 1]
        e_c = e4[:, h:h + 1]
        dinv_c = dinv4[:, h:h + 1]
        wk_c = wk4[:, h:h + 1]

        qk = lax.dot_general(qh, kh_p, (((1,), (1,)), ((), ())), preferred_element_type=F32)
        if NS == 1:
            q_c = jnp.dot(qh, c_in[0, h].astype(BF16), preferred_element_type=F32)
            n_rows = n_in[0, h:h + 1, :]
        else:
            q_c = jnp.zeros((R, D_HEAD), F32)
            n_rows = jnp.zeros((R, D_HEAD), F32)
            for j in range(NS):
                mine = seq_of_row == j
                q_c = jnp.where(mine, jnp.dot(qh, c_in[j, h].astype(BF16),
                                              preferred_element_type=F32), q_c)
                n_rows = jnp.where(mine, n_in[j, h:h + 1, :], n_rows)
        kw = kh.astype(F32) * wk_c
        kws = [kw if NS == 1 else jnp.where(seq_of_row == j, kw, 0.0) for j in range(NS)]
        upds = [lax.dot_general(_pad_rows(kwj, RC).astype(BF16), vh_p,
                                (((0,), (0,)), ((), ())), preferred_element_type=F32)
                for kwj in kws]

        c_r = jnp.sum(jnp.where(eye, c_c, 0.0), axis=0, keepdims=True)
        w = jnp.exp(jnp.where(causal, c_r - big_m_c, NEG_BIG))
        s = qk * w
        den = jnp.sum(s, axis=1, keepdims=True)
        num = jnp.dot(s.astype(BF16), vh_p, preferred_element_type=F32)
        q_n = jnp.sum(qh.astype(F32) * n_rows, axis=1, keepdims=True)
        num = num + e_c * q_c
        den = den + e_c * q_n
        hh = num * (1.0 / jnp.maximum(jnp.abs(den), dinv_c))
        hh = hh * lax.rsqrt(jnp.mean(hh * hh, axis=1, keepdims=True) + EPS)
        outs.append(((hh * hg_ref[:, sl]) * jax.nn.sigmoid(oh)).astype(BF16))

        for j in range(NS):
            dec = decay4[j][:, h:h + 1]
            c_out[j, h] = dec * c_in[j, h] + upds[j]
            n_out[j, h:h + 1, :] = (dec * n_in[j, h:h + 1, :]
                                    + jnp.sum(kws[j], axis=0, keepdims=True))
    return outs


def _merge_out(x, ha, hb, ga, gb, wa_ref, wb_ref, wo_ref):
    pa = jnp.dot(ha, wa_ref[:, :D_MODEL], preferred_element_type=F32)
    pb = jnp.dot(hb, wb_ref[:, :D_MODEL], preferred_element_type=F32)
    merged = jax.nn.sigmoid(ga) * pa + jax.nn.sigmoid(gb) * pb
    return x + jnp.dot(merged.astype(BF16), wo_ref[:, :D_MODEL], preferred_element_type=F32)


class MixerIn(NamedTuple):
    x: Any
    conv0: Any
    h0: Any
    c0: Any
    n0: Any
    m0: Any
    g: Any
    w5: Any
    w2: Any
    b: Any
    wg: Any
    bg: Any
    cw: Any
    cb: Any
    wr: Any
    br: Any
    wi: Any
    bi: Any
    lam: Any
    hg: Any
    wa: Any
    wb: Any
    wo: Any


class MixerOut(NamedTuple):
    x1: Any
    conv: Any
    h: Any
    c: Any
    n: Any
    m: Any


class MixerScratch(NamedTuple):
    ext: Any
    h: Any
    c: Any
    n: Any
    m: Any


class SideIn(NamedTuple):
    q: Any
    k: Any
    v: Any
    gt: Any
    o: Any
    c0: Any
    n0: Any
    m0: Any


class SideOut(NamedTuple):
    hb: Any
    c: Any
    n: Any
    m: Any


def _split_refs(refs, *kinds):
    out, pos = [], 0
    for kind in kinds:
        n = len(kind._fields)
        out.append(kind(*refs[pos:pos + n]))
        pos += n
    assert pos == len(refs)
    return out


def _mixer_step(TS, i: MixerIn, o: MixerOut, s: MixerScratch, side_job=None):
    ti = pl.program_id(1)

    @pl.when(ti == 0)
    def _():
        _conv_init(s.ext, i.conv0, LRU_CONV)
        s.h[...] = i.h0[...]
        s.c[...] = i.c0[...]
        s.n[...] = i.n0[...]
        s.m[...] = i.m0[...]

    if side_job is not None:
        side_job()

    x = i.x[...]
    xn = _rms(x, i.g[...]).astype(BF16)
    proj = functools.partial(_in_proj, xn, i.w5, i.w2, i.b)

    def get_qkvo(h):
        off = h * D_HEAD
        q = (proj(COL_Q + off, D_HEAD) * (D_HEAD ** -0.5)).astype(BF16)
        return (q, proj(COL_K + off, D_HEAD).astype(BF16), proj(COL_V + off, D_HEAD).astype(BF16),
                proj(COL_O + off, D_HEAD))

    gt = jnp.dot(xn, i.wg[...], preferred_element_type=F32) + i.bg[...]
    gates = _mlstm_gates(1, TS, gt, s.m, s.m)
    hs = _lru_body(1, TS, proj(COL_U, D_LRU), i.cw, i.cb, i.wr, i.br[...],
                   i.wi, i.bi[...], i.lam[...], s.ext, s.h, o.conv, o.h)
    state = (s.c, s.n, s.m)
    hb = jnp.concatenate(_mlstm_heads(1, TS, get_qkvo, gates, i.hg, state, state), axis=1)
    o.x1[...] = _merge_out(x, hs.astype(BF16), hb, proj(COL_GA, D_MODEL), proj(COL_GB, D_MODEL),
                           i.wa, i.wb, i.wo)

    @pl.when(ti == pl.num_programs(1) - 1)
    def _():
        o.c[...] = s.c[...]
        o.n[...] = s.n[...]
        o.m[...] = s.m[...]


def _mixer_kernel(TS, *refs):
    _mixer_step(TS, *_split_refs(refs, MixerIn, MixerOut, MixerScratch))


def _mixer_side_kernel(TS, SIDE_NS, SIDE_TS, *refs):
    i, si, o, so, s = _split_refs(refs, MixerIn, SideIn, MixerOut, SideOut, MixerScratch)

    def get_qkvo(h):
        sl = slice(h * D_HEAD, (h + 1) * D_HEAD)
        return si.q[:, sl], si.k[:, sl], si.v[:, sl], si.o[:, sl]

    def side_job():
        gates = _mlstm_gates(SIDE_NS, SIDE_TS, si.gt[...], si.m0, so.m)
        outs = _mlstm_heads(SIDE_NS, SIDE_TS, get_qkvo, gates, i.hg,
                            (si.c0, si.n0, si.m0), (so.c, so.n, so.m))
        for h, out in enumerate(outs):
            so.hb[:, h * D_HEAD:(h + 1) * D_HEAD] = out

    _mixer_step(TS, i, o, s, side_job)


def _lru_weight_specs():
    nb = D_LRU // V7X_MXU_DIM
    return [_resident((LRU_CONV, D_LRU)), _resident((1, D_LRU)),
            _resident((nb, V7X_MXU_DIM, V7X_MXU_DIM)), _resident((1, D_LRU)),
            _resident((nb, V7X_MXU_DIM, V7X_MXU_DIM)), _resident((1, D_LRU)),
            _resident((1, D_LRU))]


def _lru_weights(P):
    return (P["lru_conv_w"], P["lru_conv_b"], P["w_r"], P["lru_b_r"], P["w_i"], P["lru_b_i"],
            P["lru_lambda"])


def _state_specs(NS, shared=False):
    def spec(*tail):
        zeros = (0,) * len(tail)
        index = (lambda s, t: (0,) + zeros) if shared else (lambda s, t: (s,) + zeros)
        return pl.BlockSpec((NS,) + tail, index)

    return [spec(LRU_CONV - 1, D_LRU), spec(1, D_LRU), spec(N_HEADS, D_HEAD, D_HEAD),
            spec(N_HEADS, D_HEAD), spec(1, V7X_LANES)]


def _state_shapes(NSEQ):
    return [jax.ShapeDtypeStruct((NSEQ, LRU_CONV - 1, D_LRU), F32),
            jax.ShapeDtypeStruct((NSEQ, 1, D_LRU), F32),
            jax.ShapeDtypeStruct((NSEQ, N_HEADS, D_HEAD, D_HEAD), F32),
            jax.ShapeDtypeStruct((NSEQ, N_HEADS, D_HEAD), F32),
            jax.ShapeDtypeStruct((NSEQ, 1, V7X_LANES), F32)]


def _mixer(x2, state, P, NSEQ, L, TS, side=None):
    NT = L // TS
    rows = pl.BlockSpec((TS, D_MODEL), lambda s, t: (s * NT + t, 0))
    wsq = _resident((D_MODEL, D_MODEL + W_PITCH_PAD))
    in_specs = ([rows] + _state_specs(1, shared=True) + _in_proj_specs() + _lru_weight_specs()
                + [_resident((1, D_MODEL)), wsq, wsq, wsq])
    operands = (x2, *state, *_in_proj_weights(P), *_lru_weights(P), P["mlstm_head_g"],
                P["w_branch_a"], P["w_branch_b"], P["w_out"])
    out_specs = [rows] + _state_specs(1)
    out_shape = [jax.ShapeDtypeStruct((NSEQ * L, D_MODEL), F32)] + _state_shapes(NSEQ)
    body = functools.partial(_mixer_kernel, TS)
    if side is not None:
        side_in, side_len = side
        n_side = side_in.c0.shape[0]
        side_ns = n_side // (NSEQ * NT)
        assert side_ns * NSEQ * NT == n_side
        step = lambda s, t: s * NT + t
        srows = lambda w: pl.BlockSpec((side_ns * side_len, w), lambda s, t: (step(s, t), 0))
        sstate = [pl.BlockSpec((side_ns, N_HEADS, D_HEAD, D_HEAD), lambda s, t: (step(s, t), 0, 0, 0)),
                  pl.BlockSpec((side_ns, N_HEADS, D_HEAD), lambda s, t: (step(s, t), 0, 0)),
                  pl.BlockSpec((side_ns, 1, V7X_LANES), lambda s, t: (step(s, t), 0, 0))]
        in_specs = in_specs + [srows(D_MODEL)] * 3 + [srows(GATE_W), srows(D_MODEL)] + sstate
        operands = operands + tuple(side_in)
        out_specs = out_specs + [srows(D_MODEL)] + sstate
        out_shape = (out_shape + [jax.ShapeDtypeStruct((n_side * side_len, D_MODEL), BF16)]
                     + _state_shapes(n_side)[2:])
        body = functools.partial(_mixer_side_kernel, TS, side_ns, side_len)
    return pl.pallas_call(
        body,
        grid=(NSEQ, NT),
        in_specs=in_specs,
        out_specs=out_specs,
        out_shape=out_shape,
        scratch_shapes=[pltpu.VMEM((1, D_LRU // V7X_LANES, V7X_SUBLANES + TS, V7X_LANES), F32),
                        pltpu.VMEM((1, 1, D_LRU), F32),
                        pltpu.VMEM((1, N_HEADS, D_HEAD, D_HEAD), F32),
                        pltpu.VMEM((1, N_HEADS, D_HEAD), F32),
                        pltpu.VMEM((1, 1, V7X_LANES), F32)],
        compiler_params=_params(2, 56),
        name="mixer",
    )(*operands)


def _proj_kernel(x_ref, g_ref, w5_ref, w2_ref, b_ref, wg_ref, bg_ref,
                 u_ref, q_ref, k_ref, v_ref, o_ref, ga_ref, gb_ref, gt_ref):
    xn = _rms(x_ref[...], g_ref[...]).astype(BF16)
    proj = functools.partial(_in_proj, xn, w5_ref, w2_ref, b_ref)
    u_ref[...] = proj(COL_U, D_MODEL)
    q_ref[...] = (proj(COL_Q, D_MODEL) * (D_HEAD ** -0.5)).astype(BF16)
    k_ref[...] = proj(COL_K, D_MODEL).astype(BF16)
    v_ref[...] = proj(COL_V, D_MODEL).astype(BF16)
    o_ref[...] = proj(COL_O, D_MODEL)
    ga_ref[...] = proj(COL_GA, D_MODEL)
    gb_ref[...] = proj(COL_GB, D_MODEL)
    gt_ref[...] = jnp.dot(xn, wg_ref[...], preferred_element_type=F32) + bg_ref[...]


def _proj(x2, P, tm):
    M = x2.shape[0]
    row = lambda w: pl.BlockSpec((tm, w), lambda i: (i, 0))
    f32o = jax.ShapeDtypeStruct((M, D_MODEL), F32)
    bf16o = jax.ShapeDtypeStruct((M, D_MODEL), BF16)
    return pl.pallas_call(
        _proj_kernel,
        grid=(M // tm,),
        in_specs=[row(D_MODEL)] + _in_proj_specs(),
        out_specs=[row(D_MODEL)] * 7 + [row(GATE_W)],
        out_shape=[f32o, bf16o, bf16o, bf16o, f32o, f32o, f32o,
                   jax.ShapeDtypeStruct((M, GATE_W), F32)],
        compiler_params=_params(1, 48),
        name="proj",
    )(x2, *_in_proj_weights(P))


def _lru_kernel(NS, TS, u_ref, conv0_ref, h0_ref, cw_ref, cb_ref, wr_ref, br_ref, wi_ref,
                bi_ref, lam_ref, ha_ref, convo_ref, ho_ref, ext_ref, h_s):
    @pl.when(pl.program_id(1) == 0)
    def _():
        _conv_init(ext_ref, conv0_ref, LRU_CONV)
        h_s[...] = h0_ref[...]

    hs = _lru_body(NS, TS, u_ref[...], cw_ref, cb_ref, wr_ref, br_ref[...], wi_ref,
                   bi_ref[...], lam_ref[...], ext_ref, h_s, convo_ref, ho_ref)
    ha_ref[...] = hs.astype(BF16)


def _lru(u, conv0, h0, P, NSEQ, L, NS, TS):
    R = NS * TS
    NT = L // TS
    rows = pl.BlockSpec((R, D_LRU), lambda s, t: (s * NT + t, 0))
    st = _state_specs(NS)[:2]
    return pl.pallas_call(
        functools.partial(_lru_kernel, NS, TS),
        grid=(NSEQ // NS, NT),
        in_specs=[rows] + st + _lru_weight_specs(),
        out_specs=[rows] + st,
        out_shape=[jax.ShapeDtypeStruct((NSEQ * L, D_LRU), BF16)] + _state_shapes(NSEQ)[:2],
        scratch_shapes=[pltpu.VMEM((NS, D_LRU // V7X_LANES, V7X_SUBLANES + TS, V7X_LANES), F32),
                        pltpu.VMEM((NS, 1, D_LRU), F32)],
        compiler_params=_params(2, 48),
        name="lru",
    )(u, conv0, h0, *_lru_weights(P))


def _post_kernel(x_ref, ha_ref, hb_ref, ga_ref, gb_ref, wa_ref, wb_ref, wo_ref, x1_ref):
    x1_ref[...] = _merge_out(x_ref[...], ha_ref[...], hb_ref[...], ga_ref[...], gb_ref[...],
                             wa_ref, wb_ref, wo_ref)


def _post(x2, ha, hb, ga, gb, P, tm):
    M = x2.shape[0]
    row = pl.BlockSpec((tm, D_MODEL), lambda i: (i, 0))
    wsp = _resident((D_MODEL, D_MODEL + W_PITCH_PAD))
    return pl.pallas_call(
        _post_kernel,
        grid=(M // tm,),
        in_specs=[row] * 5 + [wsp] * 3,
        out_specs=row,
        out_shape=jax.ShapeDtypeStruct((M, D_MODEL), F32),
        compiler_params=_params(1, 48),
        name="post",
    )(x2, ha, hb, ga, gb, P["w_branch_a"], P["w_branch_b"], P["w_out"])


def _ffn_kernel(NS, TS, x1_ref, st0_ref, g2_ref, wup_ref, cw_ref, cb_ref, wdn_ref, gf_ref,
                y_ref, sto_ref, ext_ref):
    R = NS * TS
    W = 2 * D_FF

    @pl.when(pl.program_id(1) == 0)
    def _():
        _conv_init(ext_ref, st0_ref, FFN_CONV)

    x1 = x1_ref[...]
    xn = _rms(x1, g2_ref[...]).astype(BF16)

    def up_conv(col0):
        up = jnp.dot(xn, wup_ref[:, col0:col0 + FFN_CHUNK], preferred_element_type=F32)
        return _causal_conv(ext_ref, up.reshape(NS, TS, FFN_CHUNK), cw_ref, cb_ref, sto_ref,
                            col0).reshape(R, FFN_CHUNK)

    act = jnp.concatenate(
        [(jax.nn.gelu(up_conv(D_FF + c)) * up_conv(c)).astype(BF16)
         for c in range(0, D_FF, FFN_CHUNK)], axis=1)
    x2 = x1 + jnp.dot(act, wdn_ref[:, :D_MODEL], preferred_element_type=F32)
    y_ref[...] = _rms(x2, gf_ref[...])


def _ffn(x1, st0, P, NSEQ, L, NS, TS, shared=False):
    R = NS * TS
    NT = L // TS
    W = 2 * D_FF
    rows = pl.BlockSpec((R, D_MODEL), lambda s, t: (s * NT + t, 0))
    stspec = pl.BlockSpec((NS, FFN_CONV - 1, W), lambda s, t: (s, 0, 0))
    st0spec = pl.BlockSpec((NS, FFN_CONV - 1, W), lambda s, t: (0, 0, 0)) if shared else stspec
    return pl.pallas_call(
        functools.partial(_ffn_kernel, NS, TS),
        grid=(NSEQ // NS, NT),
        in_specs=[rows, st0spec, _resident((1, D_MODEL)), _resident((D_MODEL, W)),
                  _resident((FFN_CONV, W)), _resident((1, W)),
                  _resident((D_FF, D_MODEL + W_PITCH_PAD)),
                  _resident((1, D_MODEL))],
        out_specs=[rows, stspec],
        out_shape=[jax.ShapeDtypeStruct((NSEQ * L, D_MODEL), F32),
                   jax.ShapeDtypeStruct((NSEQ, FFN_CONV - 1, W), F32)],
        scratch_shapes=[pltpu.VMEM((NS, W // V7X_LANES, V7X_SUBLANES + TS, V7X_LANES), F32)],
        compiler_params=_params(2, 56),
        name="ffn",
    )(x1, st0, P["norm2_g"], P["w_up"], P["ffn_conv_w"], P["ffn_conv_b"], P["w_down"],
      P["final_g"])


def _block_diag(w):
    bw = w.shape[1]
    per = V7X_MXU_DIM // bw
    nb = w.shape[0] // per
    w4 = w.reshape(nb, per, bw, 1, bw)
    on_diag = jnp.eye(per, dtype=w.dtype).reshape(1, per, 1, per, 1)
    return (w4 * on_diag).reshape(nb, V7X_MXU_DIM, V7X_MXU_DIM)


def _pitch_padded(w):
    pad = jnp.zeros((w.shape[0], W_PITCH_PAD), BF16)
    return jnp.concatenate([w.astype(BF16), pad], axis=1)


def _prep_weights_kernel(wt_ref, wup_ref, wdn_ref, wa_ref, wb_ref, wo_ref,
                         w5_ref, w2_ref, wg_ref, wup_o, wdn_o, wa_o, wb_o, wo_o):
    wt = wt_ref[...]
    n_gate = 2 * N_HEADS
    w5_ref[...] = _pitch_padded(wt[:N_W5].T)
    w2_ref[...] = _pitch_padded(wt[N_W5 + n_gate:].T)
    g = jnp.concatenate([wt[N_W5:N_W5 + n_gate],
                         jnp.zeros((V7X_LANES - n_gate, wt.shape[1]), F32)], axis=0).T
    head_lane = lax.broadcasted_iota(jnp.int32, g.shape, 1) < N_HEADS
    ig = jnp.where(head_lane, g, 0.0)
    fg = jnp.where(head_lane, pltpu.roll(g, V7X_LANES - N_HEADS, axis=1), 0.0)
    wg_ref[...] = jnp.concatenate([ig, fg], axis=1).astype(BF16)

    wup_o[...] = wup_ref[0].astype(BF16)
    wdn_o[...] = _pitch_padded(wdn_ref[0])
    wa_o[...] = _pitch_padded(wa_ref[0])
    wb_o[...] = _pitch_padded(wb_ref[0])
    wo_o[...] = _pitch_padded(wo_ref[0])


def _prep_weights(w_in, w_up, w_down, w_branch_a, w_branch_b, w_out):
    n_in = w_in.shape[2]
    assert n_in == N_W5 + 2 * N_HEADS + N_W2
    steps = D_MODEL // V7X_LANES
    rows_dn = D_FF // steps
    wt = jnp.transpose(w_in[0])
    slab3 = lambda r, w: pl.BlockSpec((1, r, w), lambda i: (0, i, 0))
    slab = lambda r, w: pl.BlockSpec((r, w), lambda i: (i, 0))
    sq_pad = D_MODEL + W_PITCH_PAD
    bf = lambda r, w: jax.ShapeDtypeStruct((r, w), BF16)
    return pl.pallas_call(
        _prep_weights_kernel,
        grid=(steps,),
        in_specs=[pl.BlockSpec((n_in, V7X_LANES), lambda i: (0, i)),
                  slab3(V7X_LANES, 2 * D_FF), slab3(rows_dn, D_MODEL),
                  slab3(V7X_LANES, D_MODEL), slab3(V7X_LANES, D_MODEL), slab3(V7X_LANES, D_MODEL)],
        out_specs=[slab(V7X_LANES, N_W5 + W_PITCH_PAD), slab(V7X_LANES, N_W2 + W_PITCH_PAD),
                   slab(V7X_LANES, GATE_W), slab(V7X_LANES, 2 * D_FF), slab(rows_dn, sq_pad),
                   slab(V7X_LANES, sq_pad), slab(V7X_LANES, sq_pad), slab(V7X_LANES, sq_pad)],
        out_shape=[bf(D_MODEL, N_W5 + W_PITCH_PAD), bf(D_MODEL, N_W2 + W_PITCH_PAD),
                   bf(D_MODEL, GATE_W), bf(D_MODEL, 2 * D_FF), bf(D_FF, sq_pad),
                   bf(D_MODEL, sq_pad), bf(D_MODEL, sq_pad), bf(D_MODEL, sq_pad)],
        compiler_params=_params(1, 48),
        name="prep_weights",
    )(wt, w_up, w_down, w_branch_a, w_branch_b, w_out)


def _run_long_group(x3, state, P, TS, side=None):
    NSEQ, L, _ = x3.shape
    x2 = x3.reshape(NSEQ * L, D_MODEL)
    x1, conv1, h1, c1, n1, m1, *side_out = _mixer(x2, state[:5], P, NSEQ, L, TS, side)
    y, ffn1 = _ffn(x1, state[5], P, NSEQ, L, 1, TS, shared=True)
    return y.reshape(NSEQ, L, D_MODEL), (conv1, h1, c1, n1, m1, ffn1), side_out


def _short_group_front(x2, state, P, NSEQ, L, tm, ns):
    conv0, h0, c0, n0, m0, _ = state
    u, q, k, v, o, ga, gb, gt = _proj(x2, P, tm)
    ha, conv1, h1 = _lru(u, conv0, h0, P, NSEQ, L, ns, L)
    return SideIn(q, k, v, gt, o, c0, n0, m0), (ha, ga, gb, conv1, h1)


def _short_group_back(x2, front, side_out, ffn0, P, NSEQ, L, tm, ns):
    ha, ga, gb, conv1, h1 = front
    hb, c1, n1, m1 = side_out
    x1 = _post(x2, ha, hb, ga, gb, P, tm)
    y, ffn1 = _ffn(x1, ffn0, P, NSEQ, L, ns, L)
    return y.reshape(NSEQ, L, D_MODEL), (conv1, h1, c1, n1, m1, ffn1)


def kernel(x_prompt, x_sample, state_lru_conv, state_lru_h, state_mlstm_C, state_mlstm_n,
           state_mlstm_m, state_ffn_conv, meta_tokens, norm1_g, w_in, b_in, lru_conv_w,
           lru_conv_b, lru_w_r, lru_b_r, lru_w_i, lru_b_i, lru_lambda, mlstm_head_g,
           w_branch_a, w_branch_b, w_out, norm2_g, w_up, ffn_conv_w, ffn_conv_b, w_down, final_g):
    assert w_in.shape[0] == 1, "single-layer trunk"
    b0 = b_in[0]
    gate_pad = jnp.zeros((V7X_LANES - N_HEADS,), b0.dtype)
    row = lambda a: a.reshape(1, -1).astype(F32)
    w5, w2, w_gate, w_up_b, w_down_b, w_a_b, w_b_b, w_o_b = _prep_weights(
        w_in, w_up, w_down, w_branch_a, w_branch_b, w_out)
    P = {
        "norm1_g": row(norm1_g[0]),
        "w5": w5,
        "w2": w2,
        "b_main": row(jnp.concatenate([b0[:N_W5], b0[N_W5 + 2 * N_HEADS:]])),
        "w_gate": w_gate,
        "b_gate": row(jnp.concatenate([b0[N_W5:N_W5 + N_HEADS], gate_pad,
                                       b0[N_W5 + N_HEADS:N_W5 + 2 * N_HEADS], gate_pad])),
        "lru_conv_w": lru_conv_w[0],
        "lru_conv_b": row(lru_conv_b[0]),
        "w_r": _block_diag(lru_w_r[0]).astype(BF16),
        "lru_b_r": row(lru_b_r[0]),
        "w_i": _block_diag(lru_w_i[0]).astype(BF16),
        "lru_b_i": row(lru_b_i[0]),
        "lru_lambda": row(lru_lambda[0]),
        "mlstm_head_g": row(mlstm_head_g[0]),
        "w_branch_a": w_a_b,
        "w_branch_b": w_b_b,
        "w_out": w_o_b,
        "norm2_g": row(norm2_g[0]),
        "w_up": w_up_b,
        "ffn_conv_w": ffn_conv_w[0],
        "ffn_conv_b": row(ffn_conv_b[0]),
        "w_down": w_down_b,
        "final_g": row(final_g),
    }

    def pack_state(conv, h, c, n, m, ffn):
        nseq = h.shape[0]
        m_pad = jnp.pad(m.astype(F32)[:, None, :], ((0, 0), (0, 0), (0, V7X_LANES - N_HEADS)))
        return (conv.astype(F32), h.astype(F32).reshape(nseq, 1, D_LRU), c.astype(F32),
                n.astype(F32), m_pad, ffn.astype(F32))

    def unpack_state(st):
        conv, h, c, n, m, ffn = st
        return (conv[None], h.reshape(1, -1, D_LRU), c[None], n[None],
                m[:, 0, :N_HEADS][None], ffn[None])

    zero = pack_state(jnp.zeros((1, LRU_CONV - 1, D_LRU), F32), jnp.zeros((1, D_LRU), F32),
                      jnp.zeros((1, N_HEADS, D_HEAD, D_HEAD), F32),
                      jnp.zeros((1, N_HEADS, D_HEAD), F32), jnp.zeros((1, N_HEADS), F32),
                      jnp.zeros((1, FFN_CONV - 1, 2 * D_FF), F32))
    _, meta_state, _ = _run_long_group(meta_tokens[None].astype(F32), zero, P, N_META)
    sample_state0 = pack_state(state_lru_conv[0], state_lru_h[0], state_mlstm_C[0],
                               state_mlstm_n[0], state_mlstm_m[0], state_ffn_conv[0])
    n_sample, l_sample, _ = x_sample.shape
    xs2 = x_sample.reshape(n_sample * l_sample, D_MODEL)
    short = dict(NSEQ=n_sample, L=l_sample, tm=SHORT_ROWS, ns=SHORT_ROWS // l_sample)
    side_in, front = _short_group_front(xs2, sample_state0, P, **short)
    y_prompt, prompt_state, side_out = _run_long_group(x_prompt, meta_state, P, LONG_TS,
                                                       side=(side_in, l_sample))
    y_sample, sample_state = _short_group_back(xs2, front, side_out, sample_state0[5], P, **short)
    return (y_prompt, y_sample) + unpack_state(prompt_state) + unpack_state(sample_state)
```

```python
import functools
from typing import Any, NamedTuple

import jax
import jax.numpy as jnp
from jax import lax
from jax.experimental import pallas as pl
from jax.experimental.pallas import tpu as pltpu

F32 = jnp.float32
BF16 = jnp.bfloat16

D_MODEL = 1024
D_LRU = 1024
LRU_CONV = 4
LRU_C = 8.0
N_HEADS = 4
D_HEAD = 256
D_FF = 2816
FFN_CONV = 3
EPS = 1e-6

V7X_LANES = 128
V7X_SUBLANES = 8
V7X_MXU_DIM = 256
NEG_BIG = -1e30

LONG_TS = 256
SHORT_ROWS = 256
FFN_CHUNK = 256

N_MAIN = 7 * D_MODEL
N_W5 = 5 * D_MODEL
N_W2 = 2 * D_MODEL
W_PITCH_PAD = V7X_LANES
GATE_W = 2 * V7X_LANES
COL_U, COL_Q, COL_K, COL_V, COL_O, COL_GA, COL_GB = (j * D_MODEL for j in range(7))


def _resident(shape):
    return pl.BlockSpec(shape, lambda *_: (0,) * len(shape), pipeline_mode=pl.Buffered(1))


def _params(n_grid, vmem_mb, flags=None):
    return pltpu.CompilerParams(
        dimension_semantics=("arbitrary",) * n_grid,
        vmem_limit_bytes=vmem_mb * 1024 * 1024,
        flags=flags,
    )


def _rms(x, g):
    ms = jnp.mean(x * x, axis=-1, keepdims=True)
    return x * lax.rsqrt(ms + EPS) * g


def _in_proj(xn, w5_ref, w2_ref, b_ref, col, width):
    if col < N_W5:
        w = w5_ref[:, col:col + width]
    else:
        w = w2_ref[:, col - N_W5:col - N_W5 + width]
    return jnp.dot(xn, w, preferred_element_type=F32) + b_ref[:, col:col + width]


def _in_proj_specs():
    return [_resident((1, D_MODEL)), _resident((D_MODEL, N_W5 + W_PITCH_PAD)),
            _resident((D_MODEL, N_W2 + W_PITCH_PAD)),
            _resident((1, N_MAIN)), _resident((D_MODEL, GATE_W)), _resident((1, GATE_W))]


def _in_proj_weights(P):
    return (P["norm1_g"], P["w5"], P["w2"], P["b_main"], P["w_gate"], P["b_gate"])


def _conv_init(ext_ref, hist0_ref, taps):
    pad, hist = V7X_SUBLANES, taps - 1
    for g in range(ext_ref.shape[1]):
        ls = slice(g * V7X_LANES, (g + 1) * V7X_LANES)
        ext_ref[:, g, pad - hist:pad, :] = hist0_ref[:, :, ls]


def _causal_conv(ext_ref, x3, cw_ref, cb_ref, hist_out_ref, col0=0):
    taps = cw_ref.shape[0]
    ts = x3.shape[1]
    pad, hist = V7X_SUBLANES, taps - 1
    outs = []
    for k in range(x3.shape[2] // V7X_LANES):
        g = col0 // V7X_LANES + k
        ls = slice(g * V7X_LANES, (g + 1) * V7X_LANES)
        xg = x3[:, :, k * V7X_LANES:(k + 1) * V7X_LANES]
        ext_ref[:, g, pad:pad + ts, :] = xg
        acc = cb_ref[:, ls] + cw_ref[taps - 1:taps, ls] * xg
        for j in range(hist):
            acc = acc + cw_ref[j:j + 1, ls] * ext_ref[:, g, pad - hist + j:pad - hist + j + ts, :]
        outs.append(acc)
        new_hist = ext_ref[:, g, pad + ts - hist:pad + ts, :]
        ext_ref[:, g, pad - hist:pad, :] = new_hist
        hist_out_ref[:, :, ls] = new_hist
    return jnp.concatenate(outs, axis=-1)


def _lru_body(NS, TS, u2, cw_ref, cb_ref, wr_ref, br, wi_ref, bi, lam, ext_ref, h_ref, convo_ref,
              ho_ref):
    R = NS * TS
    C = D_LRU
    SB = V7X_SUBLANES
    uc2 = _causal_conv(ext_ref, u2.reshape(NS, TS, C), cw_ref, cb_ref, convo_ref).reshape(R, C)
    ucb = uc2.astype(BF16)

    def block_diag(w_ref):
        W = V7X_MXU_DIM
        return jnp.concatenate(
            [jnp.dot(ucb[:, g * W:(g + 1) * W], w_ref[g], preferred_element_type=F32)
             for g in range(C // W)], axis=1)

    r = jax.nn.sigmoid(block_diag(wr_ref) + br)
    i = jax.nn.sigmoid(block_diag(wi_ref) + bi)
    log_a = -LRU_C * r * jax.nn.softplus(-lam)
    a = jnp.exp(log_a)
    hh = jnp.sqrt(-jnp.tanh(log_a) * (a * a + 1.0)) * (i * uc2)

    a = a.reshape(R // SB, SB, C)
    hh = hh.reshape(R // SB, SB, C)
    sub = lax.broadcasted_iota(jnp.int32, (1, SB, C), 1)
    for d in (1, 2, 4):
        keep = sub >= d
        a_sh = pltpu.roll(a, d, axis=1)
        h_sh = pltpu.roll(hh, d, axis=1)
        hh = hh + a * jnp.where(keep, h_sh, 0.0)
        a = a * jnp.where(keep, a_sh, 1.0)

    nb = TS // SB
    a = a.reshape(NS, nb, SB, C)
    hh = hh.reshape(NS, nb, SB, C)
    h = jnp.broadcast_to(h_ref[...], (NS, SB, C))
    blocks = []
    for j in range(nb):
        hj = hh[:, j] + a[:, j] * h
        blocks.append(hj)
        h = jnp.broadcast_to(hj[:, SB - 1:, :], (NS, SB, C))
    h_ref[...] = h[:, 0:1, :]
    ho_ref[...] = h[:, 0:1, :]
    return jnp.concatenate(blocks, axis=1).reshape(R, C)


def _seg_scan(x, tpos, TS, op, ident):
    d = 1
    while d < TS:
        sh = pltpu.roll(x, d, axis=0)
        x = op(x, jnp.where(tpos >= d, sh, ident))
        d *= 2
    return x


def _pad_rows(x, rows):
    if x.shape[0] >= rows:
        return x
    return jnp.concatenate([x, jnp.zeros((rows - x.shape[0],) + x.shape[1:], x.dtype)], axis=0)


class MlstmGates(NamedTuple):
    c4: Any
    big_m4: Any
    e4: Any
    dinv4: Any
    wk4: Any
    decay4: Any


def _mlstm_gates(NS, TS, gt, m_in, m_out):
    R = NS * TS
    LN = V7X_LANES
    ig4 = gt[:, :LN]
    lf4 = jax.nn.log_sigmoid(gt[:, LN:])
    tpos = lax.broadcasted_iota(jnp.int32, (R, LN), 0) & (TS - 1)
    b4 = _seg_scan(lf4, tpos, TS, jnp.add, 0.0)
    c4 = ig4 - b4
    cmax4 = _seg_scan(c4, tpos, TS, jnp.maximum, -jnp.inf)
    m_prev = [m_in[j] for j in range(NS)]
    m_rows = jnp.concatenate([jnp.broadcast_to(m, (TS, LN)) for m in m_prev], axis=0)
    big_m4 = jnp.maximum(cmax4, m_rows)
    e4 = jnp.exp(m_rows - big_m4)
    dinv4 = jnp.exp(-(b4 + big_m4))

    decay4, wk_parts = [], []
    for j in range(NS):
        b_last = b4[(j + 1) * TS - 1:(j + 1) * TS, :]
        g4 = b_last + c4[j * TS:(j + 1) * TS, :]
        mn = jnp.maximum(b_last + m_prev[j], jnp.max(g4, axis=0, keepdims=True))
        decay4.append(jnp.exp(b_last + m_prev[j] - mn))
        wk_parts.append(jnp.exp(g4 - mn))
        m_out[j] = mn
    return MlstmGates(c4, big_m4, e4, dinv4, jnp.concatenate(wk_parts, axis=0), decay4)


def _mlstm_heads(NS, TS, get_qkvo, gates: MlstmGates, hg_ref, st_in, st_out, first=None):
    R = NS * TS
    RC = max(R, V7X_LANES)
    shift = TS.bit_length() - 1
    c4, big_m4, e4, dinv4, wk4, decay4 = gates
    c_in, n_in, _ = st_in
    c_out, n_out, _ = st_out

    ri = lax.broadcasted_iota(jnp.int32, (R, RC), 0)
    ci = lax.broadcasted_iota(jnp.int32, (R, RC), 1)
    eye = ri == ci
    if NS == 1:
        causal = ci <= ri
    else:
        causal = (ci <= ri) & ((ri >> shift) == (ci >> shift))
    seq_of_row = lax.broadcasted_iota(jnp.int32, (R, D_HEAD), 0) >> shift

    outs = []
    ahead = get_qkvo(0) if first is None else first
    for h in range(N_HEADS):
        sl = slice(h * D_HEAD, (h + 1) * D_HEAD)
        qh, kh, vh, oh = ahead
        if h + 1 < N_HEADS:
            ahead = get_qkvo(h + 1)
        kh_p = _pad_rows(kh, RC)
        vh_p = _pad_rows(vh, RC)
        c_c = c4[:, h:h + 1]
        big_m_c = big_m4[:, h:h + 1]
        e_c = e4[:, h:h + 1]
        dinv_c = dinv4[:, h:h + 1]
        wk_c = wk4[:, h:h + 1]

        qk = lax.dot_general(qh, kh_p, (((1,), (1,)), ((), ())), preferred_element_type=F32)
        if NS == 1:
            q_c = jnp.dot(qh, c_in[0, h].astype(BF16), preferred_element_type=F32)
            n_rows = n_in[0, h:h + 1, :]
        else:
            q_c = jnp.zeros((R, D_HEAD), F32)
            n_rows = jnp.zeros((R, D_HEAD), F32)
            for j in range(NS):
                mine = seq_of_row == j
                q_c = jnp.where(mine, jnp.dot(qh, c_in[j, h].astype(BF16),
                                              preferred_element_type=F32), q_c)
                n_rows = jnp.where(mine, n_in[j, h:h + 1, :], n_rows)
        kw = kh.astype(F32) * wk_c
        kws = [kw if NS == 1 else jnp.where(seq_of_row == j, kw, 0.0) for j in range(NS)]
        upds = [lax.dot_general(_pad_rows(kwj, RC).astype(BF16), vh_p,
                                (((0,), (0,)), ((), ())), preferred_element_type=F32)
                for kwj in kws]

        c_r = jnp.sum(jnp.where(eye, c_c, 0.0), axis=0, keepdims=True)
        w = jnp.exp(jnp.where(causal, c_r - big_m_c, NEG_BIG))
        s = qk * w
        den = jnp.sum(s, axis=1, keepdims=True)
        num = jnp.dot(s.astype(BF16), vh_p, preferred_element_type=F32)
        q_n = jnp.sum(qh.astype(F32) * n_rows, axis=1, keepdims=True)
        num = num + e_c * q_c
        den = den + e_c * q_n
        hh = num * (1.0 / jnp.maximum(jnp.abs(den), dinv_c))
        hh = hh * lax.rsqrt(jnp.mean(hh * hh, axis=1, keepdims=True) + EPS)
        outs.append(((hh * hg_ref[:, sl]) * jax.nn.sigmoid(oh)).astype(BF16))

        for j in range(NS):
            dec = decay4[j][:, h:h + 1]
            c_out[j, h] = dec * c_in[j, h] + upds[j]
            n_out[j, h:h + 1, :] = (dec * n_in[j, h:h + 1, :]
                                    + jnp.sum(kws[j], axis=0, keepdims=True))
    return outs


def _merge_out(x, ha, hb, ga, gb, wa_ref, wb_ref, wo_ref):
    pa = jnp.dot(ha, wa_ref[:, :D_MODEL], preferred_element_type=F32)
    pb = jnp.dot(hb, wb_ref[:, :D_MODEL], preferred_element_type=F32)
    merged = jax.nn.sigmoid(ga) * pa + jax.nn.sigmoid(gb) * pb
    return x + jnp.dot(merged.astype(BF16), wo_ref[:, :D_MODEL], preferred_element_type=F32)


class MixerIn(NamedTuple):
    x: Any
    meta: Any
    g: Any
    w5: Any
    w2: Any
    b: Any
    wg: Any
    bg: Any
    cw: Any
    cb: Any
    wr: Any
    br: Any
    wi: Any
    bi: Any
    lam: Any
    hg: Any
    wa: Any
    wb: Any
    wo: Any


class MixerOut(NamedTuple):
    x1: Any
    meta_x1: Any
    conv: Any
    h: Any
    c: Any
    n: Any
    m: Any


class MixerScratch(NamedTuple):
    ext: Any
    h: Any
    c: Any
    n: Any
    m: Any
    hist0: Any
    h0: Any
    c0: Any
    n0: Any
    m0: Any


class SideIn(NamedTuple):
    q: Any
    k: Any
    v: Any
    gt: Any
    o: Any
    c0: Any
    n0: Any
    m0: Any


class SideOut(NamedTuple):
    hb: Any
    c: Any
    n: Any
    m: Any


def _split_refs(refs, *kinds):
    out, pos = [], 0
    for kind in kinds:
        n = len(kind._fields)
        out.append(kind(*refs[pos:pos + n]))
        pos += n
    assert pos == len(refs)
    return out


def _mixer_tile(TS, x, i: MixerIn, o: MixerOut, s: MixerScratch):
    xn = _rms(x, i.g[...]).astype(BF16)
    proj = functools.partial(_in_proj, xn, i.w5, i.w2, i.b)

    def get_qkvo(h):
        off = h * D_HEAD
        q = (proj(COL_Q + off, D_HEAD) * (D_HEAD ** -0.5)).astype(BF16)
        return (q, proj(COL_K + off, D_HEAD).astype(BF16), proj(COL_V + off, D_HEAD).astype(BF16),
                proj(COL_O + off, D_HEAD))

    gt = jnp.dot(xn, i.wg[...], preferred_element_type=F32) + i.bg[...]
    gates = _mlstm_gates(1, TS, gt, s.m, s.m)
    hs = _lru_body(1, TS, proj(COL_U, D_LRU), i.cw, i.cb, i.wr, i.br[...],
                   i.wi, i.bi[...], i.lam[...], s.ext, s.h, o.conv, o.h)
    state = (s.c, s.n, s.m)
    hb = jnp.concatenate(_mlstm_heads(1, TS, get_qkvo, gates, i.hg, state, state), axis=1)
    return _merge_out(x, hs.astype(BF16), hb, proj(COL_GA, D_MODEL), proj(COL_GB, D_MODEL),
                      i.wa, i.wb, i.wo)


def _mixer_kernel(TS, SIDE_NS, SIDE_TS, *refs):
    i, si, o, so, s = _split_refs(refs, MixerIn, SideIn, MixerOut, SideOut, MixerScratch)
    seq, ti = pl.program_id(0), pl.program_id(1)
    hist_rows = slice(V7X_SUBLANES - (LRU_CONV - 1), V7X_SUBLANES)

    @pl.when((seq == 0) & (ti == 0))
    def _():
        s.ext[:, :, hist_rows, :] = jnp.zeros_like(s.hist0)
        for ref in (s.h, s.c, s.n, s.m):
            ref[...] = jnp.zeros_like(ref)
        o.meta_x1[...] = _mixer_tile(i.meta.shape[0], i.meta[...], i, o, s)
        s.hist0[...] = s.ext[:, :, hist_rows, :]
        for ref0, ref in ((s.h0, s.h), (s.c0, s.c), (s.n0, s.n), (s.m0, s.m)):
            ref0[...] = ref[...]

    @pl.when(ti == 0)
    def _():
        s.ext[:, :, hist_rows, :] = s.hist0[...]
        for ref0, ref in ((s.h0, s.h), (s.c0, s.c), (s.n0, s.n), (s.m0, s.m)):
            ref[...] = ref0[...]

    def get_qkvo(h):
        sl = slice(h * D_HEAD, (h + 1) * D_HEAD)
        return si.q[:, sl], si.k[:, sl], si.v[:, sl], si.o[:, sl]

    gates = _mlstm_gates(SIDE_NS, SIDE_TS, si.gt[...], si.m0, so.m)
    outs = _mlstm_heads(SIDE_NS, SIDE_TS, get_qkvo, gates, i.hg,
                        (si.c0, si.n0, si.m0), (so.c, so.n, so.m))
    for h, out in enumerate(outs):
        so.hb[:, h * D_HEAD:(h + 1) * D_HEAD] = out

    o.x1[...] = _mixer_tile(TS, i.x[...], i, o, s)

    @pl.when(ti == pl.num_programs(1) - 1)
    def _():
        o.c[...] = s.c[...]
        o.n[...] = s.n[...]
        o.m[...] = s.m[...]


def _lru_weight_specs():
    nb = D_LRU // V7X_MXU_DIM
    return [_resident((LRU_CONV, D_LRU)), _resident((1, D_LRU)),
            _resident((nb, V7X_MXU_DIM, V7X_MXU_DIM)), _resident((1, D_LRU)),
            _resident((nb, V7X_MXU_DIM, V7X_MXU_DIM)), _resident((1, D_LRU)),
            _resident((1, D_LRU))]


def _lru_weights(P):
    return (P["lru_conv_w"], P["lru_conv_b"], P["w_r"], P["lru_b_r"], P["w_i"], P["lru_b_i"],
            P["lru_lambda"])


def _state_specs(NS):
    def spec(*tail):
        zeros = (0,) * len(tail)
        return pl.BlockSpec((NS,) + tail, lambda s, t: (s,) + zeros)

    return [spec(LRU_CONV - 1, D_LRU), spec(1, D_LRU), spec(N_HEADS, D_HEAD, D_HEAD),
            spec(N_HEADS, D_HEAD), spec(1, V7X_LANES)]


def _state_shapes(NSEQ):
    return [jax.ShapeDtypeStruct((NSEQ, LRU_CONV - 1, D_LRU), F32),
            jax.ShapeDtypeStruct((NSEQ, 1, D_LRU), F32),
            jax.ShapeDtypeStruct((NSEQ, N_HEADS, D_HEAD, D_HEAD), F32),
            jax.ShapeDtypeStruct((NSEQ, N_HEADS, D_HEAD), F32),
            jax.ShapeDtypeStruct((NSEQ, 1, V7X_LANES), F32)]


def _mixer(x2, meta, side_in: SideIn, side_len, P, NSEQ, L, TS):
    NT = L // TS
    n_meta = meta.shape[0]
    rows = pl.BlockSpec((TS, D_MODEL), lambda s, t: (s * NT + t, 0))
    wsq = _resident((D_MODEL, D_MODEL + W_PITCH_PAD))
    n_side = side_in.c0.shape[0]
    side_ns = n_side // (NSEQ * NT)
    assert side_ns * NSEQ * NT == n_side
    step = lambda s, t: s * NT + t
    srows = lambda w: pl.BlockSpec((side_ns * side_len, w), lambda s, t: (step(s, t), 0))
    sstate = [pl.BlockSpec((side_ns, N_HEADS, D_HEAD, D_HEAD), lambda s, t: (step(s, t), 0, 0, 0)),
              pl.BlockSpec((side_ns, N_HEADS, D_HEAD), lambda s, t: (step(s, t), 0, 0)),
              pl.BlockSpec((side_ns, 1, V7X_LANES), lambda s, t: (step(s, t), 0, 0))]
    state_scratch = [pltpu.VMEM((1, 1, D_LRU), F32),
                     pltpu.VMEM((1, N_HEADS, D_HEAD, D_HEAD), F32),
                     pltpu.VMEM((1, N_HEADS, D_HEAD), F32),
                     pltpu.VMEM((1, 1, V7X_LANES), F32)]
    slabs = D_LRU // V7X_LANES
    return pl.pallas_call(
        functools.partial(_mixer_kernel, TS, side_ns, side_len),
        grid=(NSEQ, NT),
        in_specs=([rows, _resident((n_meta, D_MODEL))] + _in_proj_specs() + _lru_weight_specs()
                  + [_resident((1, D_MODEL)), wsq, wsq, wsq]
                  + [srows(D_MODEL)] * 3 + [srows(GATE_W), srows(D_MODEL)] + sstate),
        out_specs=([rows, pl.BlockSpec((n_meta, D_MODEL), lambda s, t: (0, 0))] + _state_specs(1)
                   + [srows(D_MODEL)] + sstate),
        out_shape=([jax.ShapeDtypeStruct((NSEQ * L, D_MODEL), F32),
                    jax.ShapeDtypeStruct((n_meta, D_MODEL), F32)] + _state_shapes(NSEQ)
                   + [jax.ShapeDtypeStruct((n_side * side_len, D_MODEL), BF16)]
                   + _state_shapes(n_side)[2:]),
        scratch_shapes=([pltpu.VMEM((1, slabs, V7X_SUBLANES + TS, V7X_LANES), F32)] + state_scratch
                        + [pltpu.VMEM((1, slabs, LRU_CONV - 1, V7X_LANES), F32)] + state_scratch),
        compiler_params=_params(2, 56),
        name="mixer",
    )(x2, meta, *_in_proj_weights(P), *_lru_weights(P), P["mlstm_head_g"],
      P["w_branch_a"], P["w_branch_b"], P["w_out"], *side_in)


def _proj_kernel(x_ref, g_ref, w5_ref, w2_ref, b_ref, wg_ref, bg_ref,
                 u_ref, q_ref, k_ref, v_ref, o_ref, ga_ref, gb_ref, gt_ref):
    xn = _rms(x_ref[...], g_ref[...]).astype(BF16)
    proj = functools.partial(_in_proj, xn, w5_ref, w2_ref, b_ref)
    u_ref[...] = proj(COL_U, D_MODEL)
    q_ref[...] = (proj(COL_Q, D_MODEL) * (D_HEAD ** -0.5)).astype(BF16)
    k_ref[...] = proj(COL_K, D_MODEL).astype(BF16)
    v_ref[...] = proj(COL_V, D_MODEL).astype(BF16)
    o_ref[...] = proj(COL_O, D_MODEL)
    ga_ref[...] = proj(COL_GA, D_MODEL)
    gb_ref[...] = proj(COL_GB, D_MODEL)
    gt_ref[...] = jnp.dot(xn, wg_ref[...], preferred_element_type=F32) + bg_ref[...]


def _proj(x2, P, tm):
    M = x2.shape[0]
    row = lambda w: pl.BlockSpec((tm, w), lambda i: (i, 0))
    f32o = jax.ShapeDtypeStruct((M, D_MODEL), F32)
    bf16o = jax.ShapeDtypeStruct((M, D_MODEL), BF16)
    return pl.pallas_call(
        _proj_kernel,
        grid=(M // tm,),
        in_specs=[row(D_MODEL)] + _in_proj_specs(),
        out_specs=[row(D_MODEL)] * 7 + [row(GATE_W)],
        out_shape=[f32o, bf16o, bf16o, bf16o, f32o, f32o, f32o,
                   jax.ShapeDtypeStruct((M, GATE_W), F32)],
        compiler_params=_params(1, 48),
        name="proj",
    )(x2, *_in_proj_weights(P))


def _lru_kernel(NS, TS, u_ref, conv0_ref, h0_ref, cw_ref, cb_ref, wr_ref, br_ref, wi_ref,
                bi_ref, lam_ref, ha_ref, convo_ref, ho_ref, ext_ref, h_s):
    @pl.when(pl.program_id(1) == 0)
    def _():
        _conv_init(ext_ref, conv0_ref, LRU_CONV)
        h_s[...] = h0_ref[...]

    hs = _lru_body(NS, TS, u_ref[...], cw_ref, cb_ref, wr_ref, br_ref[...], wi_ref,
                   bi_ref[...], lam_ref[...], ext_ref, h_s, convo_ref, ho_ref)
    ha_ref[...] = hs.astype(BF16)


def _lru(u, conv0, h0, P, NSEQ, L, NS, TS):
    R = NS * TS
    NT = L // TS
    rows = pl.BlockSpec((R, D_LRU), lambda s, t: (s * NT + t, 0))
    st = _state_specs(NS)[:2]
    return pl.pallas_call(
        functools.partial(_lru_kernel, NS, TS),
        grid=(NSEQ // NS, NT),
        in_specs=[rows] + st + _lru_weight_specs(),
        out_specs=[rows] + st,
        out_shape=[jax.ShapeDtypeStruct((NSEQ * L, D_LRU), BF16)] + _state_shapes(NSEQ)[:2],
        scratch_shapes=[pltpu.VMEM((NS, D_LRU // V7X_LANES, V7X_SUBLANES + TS, V7X_LANES), F32),
                        pltpu.VMEM((NS, 1, D_LRU), F32)],
        compiler_params=_params(2, 48),
        name="lru",
    )(u, conv0, h0, *_lru_weights(P))


def _post_kernel(x_ref, ha_ref, hb_ref, ga_ref, gb_ref, wa_ref, wb_ref, wo_ref, x1_ref):
    x1_ref[...] = _merge_out(x_ref[...], ha_ref[...], hb_ref[...], ga_ref[...], gb_ref[...],
                             wa_ref, wb_ref, wo_ref)


def _post(x2, ha, hb, ga, gb, P, tm):
    M = x2.shape[0]
    row = pl.BlockSpec((tm, D_MODEL), lambda i: (i, 0))
    wsp = _resident((D_MODEL, D_MODEL + W_PITCH_PAD))
    return pl.pallas_call(
        _post_kernel,
        grid=(M // tm,),
        in_specs=[row] * 5 + [wsp] * 3,
        out_specs=row,
        out_shape=jax.ShapeDtypeStruct((M, D_MODEL), F32),
        compiler_params=_params(1, 48),
        name="post",
    )(x2, ha, hb, ga, gb, P["w_branch_a"], P["w_branch_b"], P["w_out"])


def _ffn_kernel(NS, TS, x1_ref, st0_ref, g2_ref, wup_ref, cw_ref, cb_ref, wdn_ref, gf_ref,
                y_ref, sto_ref, ext_ref):
    @pl.when(pl.program_id(1) == 0)
    def _():
        _conv_init(ext_ref, st0_ref, FFN_CONV)

    y_ref[...] = _ffn_tile(NS, TS, x1_ref[...], g2_ref, wup_ref, cw_ref, cb_ref, wdn_ref, gf_ref,
                           sto_ref, ext_ref)


def _ffn_meta_kernel(TS, x1_ref, meta_ref, g2_ref, wup_ref, cw_ref, cb_ref, wdn_ref, gf_ref,
                     y_ref, sto_ref, ext_ref, hist0_ref):
    seq, ti = pl.program_id(0), pl.program_id(1)
    hist = FFN_CONV - 1
    hist_rows = slice(V7X_SUBLANES - hist, V7X_SUBLANES)

    @pl.when((seq == 0) & (ti == 0))
    def _():
        xm = _rms(meta_ref[...], g2_ref[...]).astype(BF16)
        up = jnp.dot(xm, wup_ref[...], preferred_element_type=F32)
        for g in range(hist0_ref.shape[1]):
            hist0_ref[:, g, :, :] = up[None, up.shape[0] - hist:, g * V7X_LANES:(g + 1) * V7X_LANES]

    @pl.when(ti == 0)
    def _():
        ext_ref[:, :, hist_rows, :] = hist0_ref[...]

    y_ref[...] = _ffn_tile(1, TS, x1_ref[...], g2_ref, wup_ref, cw_ref, cb_ref, wdn_ref, gf_ref,
                           sto_ref, ext_ref)


def _ffn_tile(NS, TS, x1, g2_ref, wup_ref, cw_ref, cb_ref, wdn_ref, gf_ref, sto_ref, ext_ref):
    R = NS * TS
    xn = _rms(x1, g2_ref[...]).astype(BF16)

    def up_conv(col0):
        up = jnp.dot(xn, wup_ref[:, col0:col0 + FFN_CHUNK], preferred_element_type=F32)
        return _causal_conv(ext_ref, up.reshape(NS, TS, FFN_CHUNK), cw_ref, cb_ref, sto_ref,
                            col0).reshape(R, FFN_CHUNK)

    act = jnp.concatenate(
        [(jax.nn.gelu(up_conv(D_FF + c)) * up_conv(c)).astype(BF16)
         for c in range(0, D_FF, FFN_CHUNK)], axis=1)
    x2 = x1 + jnp.dot(act, wdn_ref[:, :D_MODEL], preferred_element_type=F32)
    return _rms(x2, gf_ref[...])


def _ffn(x1, st0, P, NSEQ, L, NS, TS, meta_x1=None):
    R = NS * TS
    NT = L // TS
    W = 2 * D_FF
    slabs = W // V7X_LANES
    rows = pl.BlockSpec((R, D_MODEL), lambda s, t: (s * NT + t, 0))
    stspec = pl.BlockSpec((NS, FFN_CONV - 1, W), lambda s, t: (s, 0, 0))
    weights = (P["norm2_g"], P["w_up"], P["ffn_conv_w"], P["ffn_conv_b"], P["w_down"], P["final_g"])
    wspecs = [_resident((1, D_MODEL)), _resident((D_MODEL, W)), _resident((FFN_CONV, W)),
              _resident((1, W)), _resident((D_FF, D_MODEL + W_PITCH_PAD)), _resident((1, D_MODEL))]
    scratch = [pltpu.VMEM((NS, slabs, V7X_SUBLANES + TS, V7X_LANES), F32)]
    if meta_x1 is None:
        body, first, first_spec = functools.partial(_ffn_kernel, NS, TS), st0, stspec
    else:
        assert NS == 1
        body, first = functools.partial(_ffn_meta_kernel, TS), meta_x1
        first_spec = _resident(meta_x1.shape)
        scratch = scratch + [pltpu.VMEM((1, slabs, FFN_CONV - 1, V7X_LANES), F32)]
    return pl.pallas_call(
        body,
        grid=(NSEQ // NS, NT),
        in_specs=[rows, first_spec] + wspecs,
        out_specs=[rows, stspec],
        out_shape=[jax.ShapeDtypeStruct((NSEQ * L, D_MODEL), F32),
                   jax.ShapeDtypeStruct((NSEQ, FFN_CONV - 1, W), F32)],
        scratch_shapes=scratch,
        compiler_params=_params(2, 56),
        name="ffn",
    )(x1, first, *weights)


def _block_diag(w):
    bw = w.shape[1]
    per = V7X_MXU_DIM // bw
    nb = w.shape[0] // per
    w4 = w.reshape(nb, per, bw, 1, bw)
    on_diag = jnp.eye(per, dtype=w.dtype).reshape(1, per, 1, per, 1)
    return (w4 * on_diag).reshape(nb, V7X_MXU_DIM, V7X_MXU_DIM)


def _pitch_padded(w):
    pad = jnp.zeros((w.shape[0], W_PITCH_PAD), BF16)
    return jnp.concatenate([w.astype(BF16), pad], axis=1)


def _prep_weights_kernel(wt_ref, wup_ref, wdn_ref, wa_ref, wb_ref, wo_ref,
                         w5_ref, w2_ref, wg_ref, wup_o, wdn_o, wa_o, wb_o, wo_o):
    wt = wt_ref[...]
    n_gate = 2 * N_HEADS
    w5_ref[...] = _pitch_padded(wt[:N_W5].T)
    w2_ref[...] = _pitch_padded(wt[N_W5 + n_gate:].T)
    g = jnp.concatenate([wt[N_W5:N_W5 + n_gate],
                         jnp.zeros((V7X_LANES - n_gate, wt.shape[1]), F32)], axis=0).T
    head_lane = lax.broadcasted_iota(jnp.int32, g.shape, 1) < N_HEADS
    ig = jnp.where(head_lane, g, 0.0)
    fg = jnp.where(head_lane, pltpu.roll(g, V7X_LANES - N_HEADS, axis=1), 0.0)
    wg_ref[...] = jnp.concatenate([ig, fg], axis=1).astype(BF16)

    wup_o[...] = wup_ref[0].astype(BF16)
    wdn_o[...] = _pitch_padded(wdn_ref[0])
    wa_o[...] = _pitch_padded(wa_ref[0])
    wb_o[...] = _pitch_padded(wb_ref[0])
    wo_o[...] = _pitch_padded(wo_ref[0])


def _prep_weights(w_in, w_up, w_down, w_branch_a, w_branch_b, w_out):
    n_in = w_in.shape[2]
    assert n_in == N_W5 + 2 * N_HEADS + N_W2
    steps = D_MODEL // V7X_LANES
    rows_dn = D_FF // steps
    wt = jnp.transpose(w_in[0])
    slab3 = lambda r, w: pl.BlockSpec((1, r, w), lambda i: (0, i, 0))
    slab = lambda r, w: pl.BlockSpec((r, w), lambda i: (i, 0))
    sq_pad = D_MODEL + W_PITCH_PAD
    bf = lambda r, w: jax.ShapeDtypeStruct((r, w), BF16)
    return pl.pallas_call(
        _prep_weights_kernel,
        grid=(steps,),
        in_specs=[pl.BlockSpec((n_in, V7X_LANES), lambda i: (0, i)),
                  slab3(V7X_LANES, 2 * D_FF), slab3(rows_dn, D_MODEL),
                  slab3(V7X_LANES, D_MODEL), slab3(V7X_LANES, D_MODEL), slab3(V7X_LANES, D_MODEL)],
        out_specs=[slab(V7X_LANES, N_W5 + W_PITCH_PAD), slab(V7X_LANES, N_W2 + W_PITCH_PAD),
                   slab(V7X_LANES, GATE_W), slab(V7X_LANES, 2 * D_FF), slab(rows_dn, sq_pad),
                   slab(V7X_LANES, sq_pad), slab(V7X_LANES, sq_pad), slab(V7X_LANES, sq_pad)],
        out_shape=[bf(D_MODEL, N_W5 + W_PITCH_PAD), bf(D_MODEL, N_W2 + W_PITCH_PAD),
                   bf(D_MODEL, GATE_W), bf(D_MODEL, 2 * D_FF), bf(D_FF, sq_pad),
                   bf(D_MODEL, sq_pad), bf(D_MODEL, sq_pad), bf(D_MODEL, sq_pad)],
        compiler_params=_params(1, 48),
        name="prep_weights",
    )(wt, w_up, w_down, w_branch_a, w_branch_b, w_out)


def _run_long_group(x3, meta, side_in, side_len, P, TS):
    NSEQ, L, _ = x3.shape
    x2 = x3.reshape(NSEQ * L, D_MODEL)
    x1, meta_x1, conv1, h1, c1, n1, m1, *side_out = _mixer(x2, meta, side_in, side_len, P,
                                                           NSEQ, L, TS)
    y, ffn1 = _ffn(x1, None, P, NSEQ, L, 1, TS, meta_x1=meta_x1)
    return y.reshape(NSEQ, L, D_MODEL), (conv1, h1, c1, n1, m1, ffn1), side_out


def _short_group_front(x2, state, P, NSEQ, L, tm, ns):
    conv0, h0, c0, n0, m0, _ = state
    u, q, k, v, o, ga, gb, gt = _proj(x2, P, tm)
    ha, conv1, h1 = _lru(u, conv0, h0, P, NSEQ, L, ns, L)
    return SideIn(q, k, v, gt, o, c0, n0, m0), (ha, ga, gb, conv1, h1)


def _short_group_back(x2, front, side_out, ffn0, P, NSEQ, L, tm, ns):
    ha, ga, gb, conv1, h1 = front
    hb, c1, n1, m1 = side_out
    x1 = _post(x2, ha, hb, ga, gb, P, tm)
    y, ffn1 = _ffn(x1, ffn0, P, NSEQ, L, ns, L)
    return y.reshape(NSEQ, L, D_MODEL), (conv1, h1, c1, n1, m1, ffn1)


def kernel(x_prompt, x_sample, state_lru_conv, state_lru_h, state_mlstm_C, state_mlstm_n,
           state_mlstm_m, state_ffn_conv, meta_tokens, norm1_g, w_in, b_in, lru_conv_w,
           lru_conv_b, lru_w_r, lru_b_r, lru_w_i, lru_b_i, lru_lambda, mlstm_head_g,
           w_branch_a, w_branch_b, w_out, norm2_g, w_up, ffn_conv_w, ffn_conv_b, w_down, final_g):
    assert w_in.shape[0] == 1, "single-layer trunk"
    b0 = b_in[0]
    gate_pad = jnp.zeros((V7X_LANES - N_HEADS,), b0.dtype)
    row = lambda a: a.reshape(1, -1).astype(F32)
    w5, w2, w_gate, w_up_b, w_down_b, w_a_b, w_b_b, w_o_b = _prep_weights(
        w_in, w_up, w_down, w_branch_a, w_branch_b, w_out)
    P = {
        "norm1_g": row(norm1_g[0]),
        "w5": w5,
        "w2": w2,
        "b_main": row(jnp.concatenate([b0[:N_W5], b0[N_W5 + 2 * N_HEADS:]])),
        "w_gate": w_gate,
        "b_gate": row(jnp.concatenate([b0[N_W5:N_W5 + N_HEADS], gate_pad,
                                       b0[N_W5 + N_HEADS:N_W5 + 2 * N_HEADS], gate_pad])),
        "lru_conv_w": lru_conv_w[0],
        "lru_conv_b": row(lru_conv_b[0]),
        "w_r": _block_diag(lru_w_r[0]).astype(BF16),
        "lru_b_r": row(lru_b_r[0]),
        "w_i": _block_diag(lru_w_i[0]).astype(BF16),
        "lru_b_i": row(lru_b_i[0]),
        "lru_lambda": row(lru_lambda[0]),
        "mlstm_head_g": row(mlstm_head_g[0]),
        "w_branch_a": w_a_b,
        "w_branch_b": w_b_b,
        "w_out": w_o_b,
        "norm2_g": row(norm2_g[0]),
        "w_up": w_up_b,
        "ffn_conv_w": ffn_conv_w[0],
        "ffn_conv_b": row(ffn_conv_b[0]),
        "w_down": w_down_b,
        "final_g": row(final_g),
    }

    def pack_state(conv, h, c, n, m, ffn):
        nseq = h.shape[0]
        m_pad = jnp.pad(m.astype(F32)[:, None, :], ((0, 0), (0, 0), (0, V7X_LANES - N_HEADS)))
        return (conv.astype(F32), h.astype(F32).reshape(nseq, 1, D_LRU), c.astype(F32),
                n.astype(F32), m_pad, ffn.astype(F32))

    def unpack_state(st):
        conv, h, c, n, m, ffn = st
        return (conv[None], h.reshape(1, -1, D_LRU), c[None], n[None],
                m[:, 0, :N_HEADS][None], ffn[None])

    sample_state0 = pack_state(state_lru_conv[0], state_lru_h[0], state_mlstm_C[0],
                               state_mlstm_n[0], state_mlstm_m[0], state_ffn_conv[0])
    n_sample, l_sample, _ = x_sample.shape
    xs2 = x_sample.reshape(n_sample * l_sample, D_MODEL)
    short = dict(NSEQ=n_sample, L=l_sample, tm=SHORT_ROWS, ns=SHORT_ROWS // l_sample)
    side_in, front = _short_group_front(xs2, sample_state0, P, **short)
    y_prompt, prompt_state, side_out = _run_long_group(x_prompt, meta_tokens.astype(F32), side_in,
                                                       l_sample, P, LONG_TS)
    y_sample, sample_state = _short_group_back(xs2, front, side_out, sample_state0[5], P, **short)
    return (y_prompt, y_sample) + unpack_state(prompt_state) + unpack_state(sample_state)
```

```python
import functools
from typing import Any, NamedTuple

import jax
import jax.numpy as jnp
from jax import lax
from jax.experimental import pallas as pl
from jax.experimental.pallas import tpu as pltpu

F32 = jnp.float32
BF16 = jnp.bfloat16

D_MODEL = 1024
D_LRU = 1024
LRU_CONV = 4
LRU_C = 8.0
N_HEADS = 4
D_HEAD = 256
D_FF = 2816
FFN_CONV = 3
EPS = 1e-6

V7X_LANES = 128
V7X_SUBLANES = 8
V7X_MXU_DIM = 256
VMEM_MB_RESIDENT_HALF = 56
VMEM_MB_ROW_TILED = 48
NEG_BIG = -1e30

LONG_TS = 256
SHORT_ROWS = 256
FFN_CHUNK = 256

N_MAIN = 7 * D_MODEL
N_W5 = 5 * D_MODEL
N_W2 = 2 * D_MODEL
W_PITCH_PAD = V7X_LANES
GATE_W = 2 * V7X_LANES
COL_U, COL_Q, COL_K, COL_V, COL_O, COL_GA, COL_GB = (j * D_MODEL for j in range(7))


def _resident(shape):
    return pl.BlockSpec(shape, lambda *_: (0,) * len(shape), pipeline_mode=pl.Buffered(1))


def _params(n_grid, vmem_mb):
    return pltpu.CompilerParams(
        dimension_semantics=("arbitrary",) * n_grid,
        vmem_limit_bytes=vmem_mb * 1024 * 1024,
    )


def _rms(x, g):
    ms = jnp.mean(x * x, axis=-1, keepdims=True)
    return x * lax.rsqrt(ms + EPS) * g


def _in_proj(xn, w5_ref, w2_ref, b_ref, col, width):
    if col < N_W5:
        w = w5_ref[:, col:col + width]
    else:
        w = w2_ref[:, col - N_W5:col - N_W5 + width]
    return jnp.dot(xn, w, preferred_element_type=F32) + b_ref[:, col:col + width]


def _in_proj_specs():
    return [_resident((1, D_MODEL)), _resident((D_MODEL, N_W5 + W_PITCH_PAD)),
            _resident((D_MODEL, N_W2 + W_PITCH_PAD)),
            _resident((1, N_MAIN)), _resident((D_MODEL, GATE_W)), _resident((1, GATE_W))]


def _in_proj_weights(P):
    return (P["norm1_g"], P["w5"], P["w2"], P["b_main"], P["w_gate"], P["b_gate"])


def _conv_init(ext_ref, hist0_ref, taps):
    pad, hist = V7X_SUBLANES, taps - 1
    for g in range(ext_ref.shape[1]):
        ls = slice(g * V7X_LANES, (g + 1) * V7X_LANES)
        ext_ref[:, g, pad - hist:pad, :] = hist0_ref[:, :, ls]


def _causal_conv(ext_ref, x3, cw_ref, cb_ref, hist_out_ref, col0=0):
    taps = cw_ref.shape[0]
    ts = x3.shape[1]
    pad, hist = V7X_SUBLANES, taps - 1
    outs = []
    for k in range(x3.shape[2] // V7X_LANES):
        g = col0 // V7X_LANES + k
        ls = slice(g * V7X_LANES, (g + 1) * V7X_LANES)
        xg = x3[:, :, k * V7X_LANES:(k + 1) * V7X_LANES]
        ext_ref[:, g, pad:pad + ts, :] = xg
        acc = cb_ref[:, ls] + cw_ref[taps - 1:taps, ls] * xg
        for j in range(hist):
            acc = acc + cw_ref[j:j + 1, ls] * ext_ref[:, g, pad - hist + j:pad - hist + j + ts, :]
        outs.append(acc)
        new_hist = ext_ref[:, g, pad + ts - hist:pad + ts, :]
        ext_ref[:, g, pad - hist:pad, :] = new_hist
        hist_out_ref[:, :, ls] = new_hist
    return jnp.concatenate(outs, axis=-1)


def _lru_body(NS, TS, u2, cw_ref, cb_ref, wr_ref, br, wi_ref, bi, lam, ext_ref, h_ref, convo_ref,
              ho_ref):
    R = NS * TS
    C = D_LRU
    SB = V7X_SUBLANES
    uc2 = _causal_conv(ext_ref, u2.reshape(NS, TS, C), cw_ref, cb_ref, convo_ref).reshape(R, C)
    ucb = uc2.astype(BF16)

    def block_diag(w_ref):
        W = V7X_MXU_DIM
        return jnp.concatenate(
            [jnp.dot(ucb[:, g * W:(g + 1) * W], w_ref[g], preferred_element_type=F32)
             for g in range(C // W)], axis=1)

    r = jax.nn.sigmoid(block_diag(wr_ref) + br)
    i = jax.nn.sigmoid(block_diag(wi_ref) + bi)
    log_a = -LRU_C * r * jax.nn.softplus(-lam)
    a = jnp.exp(log_a)
    hh = jnp.sqrt(-jnp.tanh(log_a) * (a * a + 1.0)) * (i * uc2)

    a = a.reshape(R // SB, SB, C)
    hh = hh.reshape(R // SB, SB, C)
    sub = lax.broadcasted_iota(jnp.int32, (1, SB, C), 1)
    for d in (1 << p for p in range(SB.bit_length() - 1)):
        keep = sub >= d
        a_sh = pltpu.roll(a, d, axis=1)
        h_sh = pltpu.roll(hh, d, axis=1)
        hh = hh + a * jnp.where(keep, h_sh, 0.0)
        a = a * jnp.where(keep, a_sh, 1.0)

    nb = TS // SB
    a = a.reshape(NS, nb, SB, C)
    hh = hh.reshape(NS, nb, SB, C)
    h = jnp.broadcast_to(h_ref[...], (NS, SB, C))
    blocks = []
    for j in range(nb):
        hj = hh[:, j] + a[:, j] * h
        blocks.append(hj)
        h = jnp.broadcast_to(hj[:, SB - 1:, :], (NS, SB, C))
    h_ref[...] = h[:, 0:1, :]
    ho_ref[...] = h[:, 0:1, :]
    return jnp.concatenate(blocks, axis=1).reshape(R, C)


def _seg_scan(x, tpos, TS, op, ident):
    d = 1
    while d < TS:
        sh = pltpu.roll(x, d, axis=0)
        x = op(x, jnp.where(tpos >= d, sh, ident))
        d *= 2
    return x


def _pad_rows(x, rows):
    if x.shape[0] >= rows:
        return x
    return jnp.concatenate([x, jnp.zeros((rows - x.shape[0],) + x.shape[1:], x.dtype)], axis=0)


class MlstmGates(NamedTuple):
    c4: Any
    big_m4: Any
    e4: Any
    dinv4: Any
    wk4: Any
    decay4: Any


def _mlstm_gates(NS, TS, gt, m_in, m_out):
    R = NS * TS
    LN = V7X_LANES
    ig4 = gt[:, :LN]
    lf4 = jax.nn.log_sigmoid(gt[:, LN:])
    tpos = lax.broadcasted_iota(jnp.int32, (R, LN), 0) & (TS - 1)
    b4 = _seg_scan(lf4, tpos, TS, jnp.add, 0.0)
    c4 = ig4 - b4
    cmax4 = _seg_scan(c4, tpos, TS, jnp.maximum, -jnp.inf)
    m_prev = [m_in[j] for j in range(NS)]
    m_rows = jnp.concatenate([jnp.broadcast_to(m, (TS, LN)) for m in m_prev], axis=0)
    big_m4 = jnp.maximum(cmax4, m_rows)
    e4 = jnp.exp(m_rows - big_m4)
    dinv4 = jnp.exp(-(b4 + big_m4))

    decay4, wk_parts = [], []
    for j in range(NS):
        b_last = b4[(j + 1) * TS - 1:(j + 1) * TS, :]
        g4 = b_last + c4[j * TS:(j + 1) * TS, :]
        mn = jnp.maximum(b_last + m_prev[j], jnp.max(g4, axis=0, keepdims=True))
        decay4.append(jnp.exp(b_last + m_prev[j] - mn))
        wk_parts.append(jnp.exp(g4 - mn))
        m_out[j] = mn
    return MlstmGates(c4, big_m4, e4, dinv4, jnp.concatenate(wk_parts, axis=0), decay4)


def _mlstm_heads(NS, TS, get_qkvo, gates: MlstmGates, hg_ref, st_in, st_out):
    R = NS * TS
    RC = max(R, V7X_LANES)
    shift = TS.bit_length() - 1
    c4, big_m4, e4, dinv4, wk4, decay4 = gates
    c_in, n_in, _ = st_in
    c_out, n_out, _ = st_out

    ri = lax.broadcasted_iota(jnp.int32, (R, RC), 0)
    ci = lax.broadcasted_iota(jnp.int32, (R, RC), 1)
    eye = ri == ci
    if NS == 1:
        causal = ci <= ri
    else:
        causal = (ci <= ri) & ((ri >> shift) == (ci >> shift))
    seq_of_row = lax.broadcasted_iota(jnp.int32, (R, D_HEAD), 0) >> shift

    outs = []
    ahead = get_qkvo(0)
    for h in range(N_HEADS):
        sl = slice(h * D_HEAD, (h + 1) * D_HEAD)
        qh, kh, vh, oh = ahead
        if h + 1 < N_HEADS:
            ahead = get_qkvo(h + 1)
        kh_p = _pad_rows(kh, RC)
        vh_p = _pad_rows(vh, RC)
        c_c = c4[:, h:h + 1]
        big_m_c = big_m4[:, h:h + 1]
        e_c = e4[:, h:h + 1]
        dinv_c = dinv4[:, h:h + 1]
        wk_c = wk4[:, h:h + 1]

        qk = lax.dot_general(qh, kh_p, (((1,), (1,)), ((), ())), preferred_element_type=F32)
        if NS == 1:
            q_c = jnp.dot(qh, c_in[0, h].astype(BF16), preferred_element_type=F32)
            n_rows = n_in[0, h:h + 1, :]
        else:
            q_c = jnp.zeros((R, D_HEAD), F32)
            n_rows = jnp.zeros((R, D_HEAD), F32)
            for j in range(NS):
                mine = seq_of_row == j
                q_c = jnp.where(mine, jnp.dot(qh, c_in[j, h].astype(BF16),
                                              preferred_element_type=F32), q_c)
                n_rows = jnp.where(mine, n_in[j, h:h + 1, :], n_rows)
        kw = kh.astype(F32) * wk_c
        kws = [kw if NS == 1 else jnp.where(seq_of_row == j, kw, 0.0) for j in range(NS)]
        upds = [lax.dot_general(_pad_rows(kwj, RC).astype(BF16), vh_p,
                                (((0,), (0,)), ((), ())), preferred_element_type=F32)
                for kwj in kws]

        c_r = jnp.sum(jnp.where(eye, c_c, 0.0), axis=0, keepdims=True)
        w = jnp.exp(jnp.where(causal, c_r - big_m_c, NEG_BIG))
        s = qk * w
        den = jnp.sum(s, axis=1, keepdims=True)
        num = jnp.dot(s.astype(BF16), vh_p, preferred_element_type=F32)
        q_n = jnp.sum(qh.astype(F32) * n_rows, axis=1, keepdims=True)
        num = num + e_c * q_c
        den = den + e_c * q_n
        hh = num * (1.0 / jnp.maximum(jnp.abs(den), dinv_c))
        hh = hh * lax.rsqrt(jnp.mean(hh * hh, axis=1, keepdims=True) + EPS)
        outs.append(((hh * hg_ref[:, sl]) * jax.nn.sigmoid(oh)).astype(BF16))

        for j in range(NS):
            dec = decay4[j][:, h:h + 1]
            c_out[j, h] = dec * c_in[j, h] + upds[j]
            n_out[j, h:h + 1, :] = (dec * n_in[j, h:h + 1, :]
                                    + jnp.sum(kws[j], axis=0, keepdims=True))
    return outs


def _merge_out(x, ha, hb, ga, gb, wa_ref, wb_ref, wo_ref):
    pa = jnp.dot(ha, wa_ref[:, :D_MODEL], preferred_element_type=F32)
    pb = jnp.dot(hb, wb_ref[:, :D_MODEL], preferred_element_type=F32)
    merged = jax.nn.sigmoid(ga) * pa + jax.nn.sigmoid(gb) * pb
    return x + jnp.dot(merged.astype(BF16), wo_ref[:, :D_MODEL], preferred_element_type=F32)


class MixerIn(NamedTuple):
    x: Any
    meta: Any
    g: Any
    w5: Any
    w2: Any
    b: Any
    wg: Any
    bg: Any
    cw: Any
    cb: Any
    wr: Any
    br: Any
    wi: Any
    bi: Any
    lam: Any
    hg: Any
    wa: Any
    wb: Any
    wo: Any


class MixerOut(NamedTuple):
    x1: Any
    meta_x1: Any
    conv: Any
    h: Any
    c: Any
    n: Any
    m: Any


class MixerScratch(NamedTuple):
    ext: Any
    h: Any
    c: Any
    n: Any
    m: Any
    hist0: Any
    h0: Any
    c0: Any
    n0: Any
    m0: Any


class SideIn(NamedTuple):
    q: Any
    k: Any
    v: Any
    gt: Any
    o: Any
    c0: Any
    n0: Any
    m0: Any


class SideOut(NamedTuple):
    hb: Any
    c: Any
    n: Any
    m: Any


def _split_refs(refs, *kinds):
    out, pos = [], 0
    for kind in kinds:
        n = len(kind._fields)
        out.append(kind(*refs[pos:pos + n]))
        pos += n
    assert pos == len(refs)
    return out


def _mixer_tile(TS, x, i: MixerIn, o: MixerOut, s: MixerScratch):
    xn = _rms(x, i.g[...]).astype(BF16)
    proj = functools.partial(_in_proj, xn, i.w5, i.w2, i.b)

    def get_qkvo(h):
        off = h * D_HEAD
        q = (proj(COL_Q + off, D_HEAD) * (D_HEAD ** -0.5)).astype(BF16)
        return (q, proj(COL_K + off, D_HEAD).astype(BF16), proj(COL_V + off, D_HEAD).astype(BF16),
                proj(COL_O + off, D_HEAD))

    gt = jnp.dot(xn, i.wg[...], preferred_element_type=F32) + i.bg[...]
    gates = _mlstm_gates(1, TS, gt, s.m, s.m)
    hs = _lru_body(1, TS, proj(COL_U, D_LRU), i.cw, i.cb, i.wr, i.br[...],
                   i.wi, i.bi[...], i.lam[...], s.ext, s.h, o.conv, o.h)
    state = (s.c, s.n, s.m)
    hb = jnp.concatenate(_mlstm_heads(1, TS, get_qkvo, gates, i.hg, state, state), axis=1)
    return _merge_out(x, hs.astype(BF16), hb, proj(COL_GA, D_MODEL), proj(COL_GB, D_MODEL),
                      i.wa, i.wb, i.wo)


def _mixer_kernel(TS, SIDE_NS, SIDE_TS, *refs):
    i, si, o, so, s = _split_refs(refs, MixerIn, SideIn, MixerOut, SideOut, MixerScratch)
    seq, ti = pl.program_id(0), pl.program_id(1)
    hist_rows = slice(V7X_SUBLANES - (LRU_CONV - 1), V7X_SUBLANES)

    @pl.when((seq == 0) & (ti == 0))
    def _():
        s.ext[:, :, hist_rows, :] = jnp.zeros_like(s.hist0)
        for ref in (s.h, s.c, s.n, s.m):
            ref[...] = jnp.zeros_like(ref)
        o.meta_x1[...] = _mixer_tile(i.meta.shape[0], i.meta[...], i, o, s)
        s.hist0[...] = s.ext[:, :, hist_rows, :]
        for ref0, ref in ((s.h0, s.h), (s.c0, s.c), (s.n0, s.n), (s.m0, s.m)):
            ref0[...] = ref[...]

    @pl.when(ti == 0)
    def _():
        s.ext[:, :, hist_rows, :] = s.hist0[...]
        for ref0, ref in ((s.h0, s.h), (s.c0, s.c), (s.n0, s.n), (s.m0, s.m)):
            ref[...] = ref0[...]

    def get_qkvo(h):
        sl = slice(h * D_HEAD, (h + 1) * D_HEAD)
        return si.q[:, sl], si.k[:, sl], si.v[:, sl], si.o[:, sl]

    gates = _mlstm_gates(SIDE_NS, SIDE_TS, si.gt[...], si.m0, so.m)
    outs = _mlstm_heads(SIDE_NS, SIDE_TS, get_qkvo, gates, i.hg,
                        (si.c0, si.n0, si.m0), (so.c, so.n, so.m))
    for h, out in enumerate(outs):
        so.hb[:, h * D_HEAD:(h + 1) * D_HEAD] = out

    o.x1[...] = _mixer_tile(TS, i.x[...], i, o, s)

    @pl.when(ti == pl.num_programs(1) - 1)
    def _():
        o.c[...] = s.c[...]
        o.n[...] = s.n[...]
        o.m[...] = s.m[...]


def _lru_weight_specs():
    nb = D_LRU // V7X_MXU_DIM
    return [_resident((LRU_CONV, D_LRU)), _resident((1, D_LRU)),
            _resident((nb, V7X_MXU_DIM, V7X_MXU_DIM)), _resident((1, D_LRU)),
            _resident((nb, V7X_MXU_DIM, V7X_MXU_DIM)), _resident((1, D_LRU)),
            _resident((1, D_LRU))]


def _lru_weights(P):
    return (P["lru_conv_w"], P["lru_conv_b"], P["w_r"], P["lru_b_r"], P["w_i"], P["lru_b_i"],
            P["lru_lambda"])


def _state_specs(NS):
    def spec(*tail):
        zeros = (0,) * len(tail)
        return pl.BlockSpec((NS,) + tail, lambda s, t: (s,) + zeros)

    return [spec(LRU_CONV - 1, D_LRU), spec(1, D_LRU), spec(N_HEADS, D_HEAD, D_HEAD),
            spec(N_HEADS, D_HEAD), spec(1, V7X_LANES)]


def _state_shapes(NSEQ):
    return [jax.ShapeDtypeStruct((NSEQ, LRU_CONV - 1, D_LRU), F32),
            jax.ShapeDtypeStruct((NSEQ, 1, D_LRU), F32),
            jax.ShapeDtypeStruct((NSEQ, N_HEADS, D_HEAD, D_HEAD), F32),
            jax.ShapeDtypeStruct((NSEQ, N_HEADS, D_HEAD), F32),
            jax.ShapeDtypeStruct((NSEQ, 1, V7X_LANES), F32)]


def _mixer(x2, meta, side_in: SideIn, side_len, P, NSEQ, L, TS):
    NT = L // TS
    n_meta = meta.shape[0]
    rows = pl.BlockSpec((TS, D_MODEL), lambda s, t: (s * NT + t, 0))
    wsq = _resident((D_MODEL, D_MODEL + W_PITCH_PAD))
    n_side = side_in.c0.shape[0]
    side_ns = n_side // (NSEQ * NT)
    assert side_ns * NSEQ * NT == n_side
    step = lambda s, t: s * NT + t
    srows = lambda w: pl.BlockSpec((side_ns * side_len, w), lambda s, t: (step(s, t), 0))
    sstate = [pl.BlockSpec((side_ns, N_HEADS, D_HEAD, D_HEAD), lambda s, t: (step(s, t), 0, 0, 0)),
              pl.BlockSpec((side_ns, N_HEADS, D_HEAD), lambda s, t: (step(s, t), 0, 0)),
              pl.BlockSpec((side_ns, 1, V7X_LANES), lambda s, t: (step(s, t), 0, 0))]
    state_scratch = [pltpu.VMEM((1, 1, D_LRU), F32),
                     pltpu.VMEM((1, N_HEADS, D_HEAD, D_HEAD), F32),
                     pltpu.VMEM((1, N_HEADS, D_HEAD), F32),
                     pltpu.VMEM((1, 1, V7X_LANES), F32)]
    slabs = D_LRU // V7X_LANES
    return pl.pallas_call(
        functools.partial(_mixer_kernel, TS, side_ns, side_len),
        grid=(NSEQ, NT),
        in_specs=([rows, _resident((n_meta, D_MODEL))] + _in_proj_specs() + _lru_weight_specs()
                  + [_resident((1, D_MODEL)), wsq, wsq, wsq]
                  + [srows(D_MODEL)] * 3 + [srows(GATE_W), srows(D_MODEL)] + sstate),
        out_specs=([rows, pl.BlockSpec((n_meta, D_MODEL), lambda s, t: (0, 0))] + _state_specs(1)
                   + [srows(D_MODEL)] + sstate),
        out_shape=([jax.ShapeDtypeStruct((NSEQ * L, D_MODEL), F32),
                    jax.ShapeDtypeStruct((n_meta, D_MODEL), F32)] + _state_shapes(NSEQ)
                   + [jax.ShapeDtypeStruct((n_side * side_len, D_MODEL), BF16)]
                   + _state_shapes(n_side)[2:]),
        scratch_shapes=([pltpu.VMEM((1, slabs, V7X_SUBLANES + TS, V7X_LANES), F32)] + state_scratch
                        + [pltpu.VMEM((1, slabs, LRU_CONV - 1, V7X_LANES), F32)] + state_scratch),
        compiler_params=_params(2, VMEM_MB_RESIDENT_HALF),
        name="mixer",
    )(x2, meta, *_in_proj_weights(P), *_lru_weights(P), P["mlstm_head_g"],
      P["w_branch_a"], P["w_branch_b"], P["w_out"], *side_in)


def _front_kernel(NS, TS, x_ref, conv0_ref, h0_ref, g_ref, w5_ref, w2_ref, b_ref, wg_ref, bg_ref,
                  cw_ref, cb_ref, wr_ref, br_ref, wi_ref, bi_ref, lam_ref,
                  ha_ref, convo_ref, ho_ref, q_ref, k_ref, v_ref, o_ref, ga_ref, gb_ref, gt_ref,
                  ext_ref, h_s):
    _conv_init(ext_ref, conv0_ref, LRU_CONV)
    h_s[...] = h0_ref[...]
    xn = _rms(x_ref[...], g_ref[...]).astype(BF16)
    proj = functools.partial(_in_proj, xn, w5_ref, w2_ref, b_ref)
    gt_ref[...] = jnp.dot(xn, wg_ref[...], preferred_element_type=F32) + bg_ref[...]
    hs = _lru_body(NS, TS, proj(COL_U, D_LRU), cw_ref, cb_ref, wr_ref, br_ref[...], wi_ref,
                   bi_ref[...], lam_ref[...], ext_ref, h_s, convo_ref, ho_ref)
    ha_ref[...] = hs.astype(BF16)
    q_ref[...] = (proj(COL_Q, D_MODEL) * (D_HEAD ** -0.5)).astype(BF16)
    k_ref[...] = proj(COL_K, D_MODEL).astype(BF16)
    v_ref[...] = proj(COL_V, D_MODEL).astype(BF16)
    o_ref[...] = proj(COL_O, D_MODEL)
    ga_ref[...] = proj(COL_GA, D_MODEL)
    gb_ref[...] = proj(COL_GB, D_MODEL)


def _front(x2, conv0, h0, P, NSEQ, TS, NS):
    R = NS * TS
    M = NSEQ * TS
    row = lambda w: pl.BlockSpec((R, w), lambda s: (s, 0))
    st = [pl.BlockSpec((NS, LRU_CONV - 1, D_LRU), lambda s: (s, 0, 0)),
          pl.BlockSpec((NS, 1, D_LRU), lambda s: (s, 0, 0))]
    f32o = jax.ShapeDtypeStruct((M, D_MODEL), F32)
    bf16o = jax.ShapeDtypeStruct((M, D_MODEL), BF16)
    return pl.pallas_call(
        functools.partial(_front_kernel, NS, TS),
        grid=(NSEQ // NS,),
        in_specs=[row(D_MODEL)] + st + _in_proj_specs() + _lru_weight_specs(),
        out_specs=[row(D_LRU)] + st + [row(D_MODEL)] * 6 + [row(GATE_W)],
        out_shape=([bf16o] + _state_shapes(NSEQ)[:2] + [bf16o, bf16o, bf16o, f32o, f32o, f32o]
                   + [jax.ShapeDtypeStruct((M, GATE_W), F32)]),
        scratch_shapes=[pltpu.VMEM((NS, D_LRU // V7X_LANES, V7X_SUBLANES + TS, V7X_LANES), F32),
                        pltpu.VMEM((NS, 1, D_LRU), F32)],
        compiler_params=_params(1, VMEM_MB_ROW_TILED),
        name="front",
    )(x2, conv0, h0, *_in_proj_weights(P), *_lru_weights(P))


def _back_kernel(NS, TS, x_ref, ha_ref, hb_ref, ga_ref, gb_ref, st0_ref, wa_ref, wb_ref, wo_ref,
                 g2_ref, wup_ref, cw_ref, cb_ref, wdn_ref, gf_ref, y_ref, sto_ref, ext_ref):
    _conv_init(ext_ref, st0_ref, FFN_CONV)
    x1 = _merge_out(x_ref[...], ha_ref[...], hb_ref[...], ga_ref[...], gb_ref[...],
                    wa_ref, wb_ref, wo_ref)
    y_ref[...] = _ffn_tile(NS, TS, x1, g2_ref, wup_ref, cw_ref, cb_ref, wdn_ref, gf_ref,
                           sto_ref, ext_ref)


def _ffn_meta_kernel(TS, x1_ref, meta_ref, g2_ref, wup_ref, cw_ref, cb_ref, wdn_ref, gf_ref,
                     y_ref, sto_ref, ext_ref, hist0_ref):
    seq, ti = pl.program_id(0), pl.program_id(1)
    hist = FFN_CONV - 1
    hist_rows = slice(V7X_SUBLANES - hist, V7X_SUBLANES)

    @pl.when((seq == 0) & (ti == 0))
    def _():
        xm = _rms(meta_ref[...], g2_ref[...]).astype(BF16)
        up = jnp.dot(xm, wup_ref[...], preferred_element_type=F32)
        for g in range(hist0_ref.shape[1]):
            hist0_ref[:, g, :, :] = up[None, up.shape[0] - hist:, g * V7X_LANES:(g + 1) * V7X_LANES]

    @pl.when(ti == 0)
    def _():
        ext_ref[:, :, hist_rows, :] = hist0_ref[...]

    y_ref[...] = _ffn_tile(1, TS, x1_ref[...], g2_ref, wup_ref, cw_ref, cb_ref, wdn_ref, gf_ref,
                           sto_ref, ext_ref)


def _ffn_tile(NS, TS, x1, g2_ref, wup_ref, cw_ref, cb_ref, wdn_ref, gf_ref, sto_ref, ext_ref):
    R = NS * TS
    xn = _rms(x1, g2_ref[...]).astype(BF16)

    def up_conv(col0):
        up = jnp.dot(xn, wup_ref[:, col0:col0 + FFN_CHUNK], preferred_element_type=F32)
        return _causal_conv(ext_ref, up.reshape(NS, TS, FFN_CHUNK), cw_ref, cb_ref, sto_ref,
                            col0).reshape(R, FFN_CHUNK)

    act = jnp.concatenate(
        [(jax.nn.gelu(up_conv(D_FF + c)) * up_conv(c)).astype(BF16)
         for c in range(0, D_FF, FFN_CHUNK)], axis=1)
    x2 = x1 + jnp.dot(act, wdn_ref[:, :D_MODEL], preferred_element_type=F32)
    return _rms(x2, gf_ref[...])


def _ffn_weight_specs():
    W = 2 * D_FF
    return [_resident((1, D_MODEL)), _resident((D_MODEL, W)), _resident((FFN_CONV, W)),
            _resident((1, W)), _resident((D_FF, D_MODEL + W_PITCH_PAD)), _resident((1, D_MODEL))]


def _ffn_weights(P):
    return (P["norm2_g"], P["w_up"], P["ffn_conv_w"], P["ffn_conv_b"], P["w_down"], P["final_g"])


def _ffn_long(x1, meta_x1, P, NSEQ, L, TS):
    NT = L // TS
    W = 2 * D_FF
    slabs = W // V7X_LANES
    rows = pl.BlockSpec((TS, D_MODEL), lambda s, t: (s * NT + t, 0))
    return pl.pallas_call(
        functools.partial(_ffn_meta_kernel, TS),
        grid=(NSEQ, NT),
        in_specs=[rows, _resident(meta_x1.shape)] + _ffn_weight_specs(),
        out_specs=[rows, pl.BlockSpec((1, FFN_CONV - 1, W), lambda s, t: (s, 0, 0))],
        out_shape=[jax.ShapeDtypeStruct((NSEQ * L, D_MODEL), F32),
                   jax.ShapeDtypeStruct((NSEQ, FFN_CONV - 1, W), F32)],
        scratch_shapes=[pltpu.VMEM((1, slabs, V7X_SUBLANES + TS, V7X_LANES), F32),
                        pltpu.VMEM((1, slabs, FFN_CONV - 1, V7X_LANES), F32)],
        compiler_params=_params(2, VMEM_MB_RESIDENT_HALF),
        name="ffn",
    )(x1, meta_x1, *_ffn_weights(P))


def _back(x2, ha, hb, ga, gb, st0, P, NSEQ, TS, NS):
    R = NS * TS
    W = 2 * D_FF
    row = pl.BlockSpec((R, D_MODEL), lambda s: (s, 0))
    stspec = pl.BlockSpec((NS, FFN_CONV - 1, W), lambda s: (s, 0, 0))
    wsq = _resident((D_MODEL, D_MODEL + W_PITCH_PAD))
    return pl.pallas_call(
        functools.partial(_back_kernel, NS, TS),
        grid=(NSEQ // NS,),
        in_specs=[row] * 5 + [stspec, wsq, wsq, wsq] + _ffn_weight_specs(),
        out_specs=[row, stspec],
        out_shape=[jax.ShapeDtypeStruct((NSEQ * TS, D_MODEL), F32),
                   jax.ShapeDtypeStruct((NSEQ, FFN_CONV - 1, W), F32)],
        scratch_shapes=[pltpu.VMEM((NS, W // V7X_LANES, V7X_SUBLANES + TS, V7X_LANES), F32)],
        compiler_params=_params(1, VMEM_MB_RESIDENT_HALF),
        name="back",
    )(x2, ha, hb, ga, gb, st0, P["w_branch_a"], P["w_branch_b"], P["w_out"], *_ffn_weights(P))


def _block_diag(w):
    bw = w.shape[1]
    per = V7X_MXU_DIM // bw
    nb = w.shape[0] // per
    w4 = w.reshape(nb, per, bw, 1, bw)
    on_diag = jnp.eye(per, dtype=w.dtype).reshape(1, per, 1, per, 1)
    return (w4 * on_diag).reshape(nb, V7X_MXU_DIM, V7X_MXU_DIM)


def _pitch_padded(w):
    pad = jnp.zeros((w.shape[0], W_PITCH_PAD), BF16)
    return jnp.concatenate([w.astype(BF16), pad], axis=1)


def _prep_weights_kernel(wt_ref, wup_ref, wdn_ref, wa_ref, wb_ref, wo_ref,
                         w5_ref, w2_ref, wg_ref, wup_o, wdn_o, wa_o, wb_o, wo_o):
    wt = wt_ref[...]
    n_gate = 2 * N_HEADS
    w5_ref[...] = _pitch_padded(wt[:N_W5].T)
    w2_ref[...] = _pitch_padded(wt[N_W5 + n_gate:].T)
    g = jnp.concatenate([wt[N_W5:N_W5 + n_gate],
                         jnp.zeros((V7X_LANES - n_gate, wt.shape[1]), F32)], axis=0).T
    head_lane = lax.broadcasted_iota(jnp.int32, g.shape, 1) < N_HEADS
    ig = jnp.where(head_lane, g, 0.0)
    fg = jnp.where(head_lane, pltpu.roll(g, V7X_LANES - N_HEADS, axis=1), 0.0)
    wg_ref[...] = jnp.concatenate([ig, fg], axis=1).astype(BF16)

    wup_o[...] = wup_ref[0].astype(BF16)
    wdn_o[...] = _pitch_padded(wdn_ref[0])
    wa_o[...] = _pitch_padded(wa_ref[0])
    wb_o[...] = _pitch_padded(wb_ref[0])
    wo_o[...] = _pitch_padded(wo_ref[0])


def _prep_weights(w_in, w_up, w_down, w_branch_a, w_branch_b, w_out):
    n_in = w_in.shape[2]
    assert n_in == N_W5 + 2 * N_HEADS + N_W2
    steps = D_MODEL // V7X_LANES
    rows_dn = D_FF // steps
    wt = jnp.transpose(w_in[0])
    slab3 = lambda r, w: pl.BlockSpec((1, r, w), lambda i: (0, i, 0))
    slab = lambda r, w: pl.BlockSpec((r, w), lambda i: (i, 0))
    sq_pad = D_MODEL + W_PITCH_PAD
    bf = lambda r, w: jax.ShapeDtypeStruct((r, w), BF16)
    return pl.pallas_call(
        _prep_weights_kernel,
        grid=(steps,),
        in_specs=[pl.BlockSpec((n_in, V7X_LANES), lambda i: (0, i)),
                  slab3(V7X_LANES, 2 * D_FF), slab3(rows_dn, D_MODEL),
                  slab3(V7X_LANES, D_MODEL), slab3(V7X_LANES, D_MODEL), slab3(V7X_LANES, D_MODEL)],
        out_specs=[slab(V7X_LANES, N_W5 + W_PITCH_PAD), slab(V7X_LANES, N_W2 + W_PITCH_PAD),
                   slab(V7X_LANES, GATE_W), slab(V7X_LANES, 2 * D_FF), slab(rows_dn, sq_pad),
                   slab(V7X_LANES, sq_pad), slab(V7X_LANES, sq_pad), slab(V7X_LANES, sq_pad)],
        out_shape=[bf(D_MODEL, N_W5 + W_PITCH_PAD), bf(D_MODEL, N_W2 + W_PITCH_PAD),
                   bf(D_MODEL, GATE_W), bf(D_MODEL, 2 * D_FF), bf(D_FF, sq_pad),
                   bf(D_MODEL, sq_pad), bf(D_MODEL, sq_pad), bf(D_MODEL, sq_pad)],
        compiler_params=_params(1, VMEM_MB_ROW_TILED),
        name="prep_weights",
    )(wt, w_up, w_down, w_branch_a, w_branch_b, w_out)


def _run_long_group(x3, meta, side_in, side_len, P, TS):
    NSEQ, L, _ = x3.shape
    x2 = x3.reshape(NSEQ * L, D_MODEL)
    x1, meta_x1, conv1, h1, c1, n1, m1, *side_out = _mixer(x2, meta, side_in, side_len, P,
                                                           NSEQ, L, TS)
    y, ffn1 = _ffn_long(x1, meta_x1, P, NSEQ, L, TS)
    return y.reshape(NSEQ, L, D_MODEL), (conv1, h1, c1, n1, m1, ffn1), side_out


def _short_group_front(x2, state, P, NSEQ, L, ns):
    conv0, h0, c0, n0, m0, _ = state
    ha, conv1, h1, q, k, v, o, ga, gb, gt = _front(x2, conv0, h0, P, NSEQ, L, ns)
    return SideIn(q, k, v, gt, o, c0, n0, m0), (ha, ga, gb, conv1, h1)


def _short_group_back(x2, front, side_out, ffn0, P, NSEQ, L, ns):
    ha, ga, gb, conv1, h1 = front
    hb, c1, n1, m1 = side_out
    y, ffn1 = _back(x2, ha, hb, ga, gb, ffn0, P, NSEQ, L, ns)
    return y.reshape(NSEQ, L, D_MODEL), (conv1, h1, c1, n1, m1, ffn1)


def kernel(x_prompt, x_sample, state_lru_conv, state_lru_h, state_mlstm_C, state_mlstm_n,
           state_mlstm_m, state_ffn_conv, meta_tokens, norm1_g, w_in, b_in, lru_conv_w,
           lru_conv_b, lru_w_r, lru_b_r, lru_w_i, lru_b_i, lru_lambda, mlstm_head_g,
           w_branch_a, w_branch_b, w_out, norm2_g, w_up, ffn_conv_w, ffn_conv_b, w_down, final_g):
    assert w_in.shape[0] == 1, "single-layer trunk"
    b0 = b_in[0]
    gate_pad = jnp.zeros((V7X_LANES - N_HEADS,), b0.dtype)
    row = lambda a: a.reshape(1, -1).astype(F32)
    w5, w2, w_gate, w_up_b, w_down_b, w_a_b, w_b_b, w_o_b = _prep_weights(
        w_in, w_up, w_down, w_branch_a, w_branch_b, w_out)
    P = {
        "norm1_g": row(norm1_g[0]),
        "w5": w5,
        "w2": w2,
        "b_main": row(jnp.concatenate([b0[:N_W5], b0[N_W5 + 2 * N_HEADS:]])),
        "w_gate": w_gate,
        "b_gate": row(jnp.concatenate([b0[N_W5:N_W5 + N_HEADS], gate_pad,
                                       b0[N_W5 + N_HEADS:N_W5 + 2 * N_HEADS], gate_pad])),
        "lru_conv_w": lru_conv_w[0],
        "lru_conv_b": row(lru_conv_b[0]),
        "w_r": _block_diag(lru_w_r[0]).astype(BF16),
        "lru_b_r": row(lru_b_r[0]),
        "w_i": _block_diag(lru_w_i[0]).astype(BF16),
        "lru_b_i": row(lru_b_i[0]),
        "lru_lambda": row(lru_lambda[0]),
        "mlstm_head_g": row(mlstm_head_g[0]),
        "w_branch_a": w_a_b,
        "w_branch_b": w_b_b,
        "w_out": w_o_b,
        "norm2_g": row(norm2_g[0]),
        "w_up": w_up_b,
        "ffn_conv_w": ffn_conv_w[0],
        "ffn_conv_b": row(ffn_conv_b[0]),
        "w_down": w_down_b,
        "final_g": row(final_g),
    }

    def pack_state(conv, h, c, n, m, ffn):
        nseq = h.shape[0]
        m_pad = jnp.pad(m.astype(F32)[:, None, :], ((0, 0), (0, 0), (0, V7X_LANES - N_HEADS)))
        return (conv.astype(F32), h.astype(F32).reshape(nseq, 1, D_LRU), c.astype(F32),
                n.astype(F32), m_pad, ffn.astype(F32))

    def unpack_state(st):
        conv, h, c, n, m, ffn = st
        return (conv[None], h.reshape(1, -1, D_LRU), c[None], n[None],
                m[:, 0, :N_HEADS][None], ffn[None])

    sample_state0 = pack_state(state_lru_conv[0], state_lru_h[0], state_mlstm_C[0],
                               state_mlstm_n[0], state_mlstm_m[0], state_ffn_conv[0])
    n_sample, l_sample, _ = x_sample.shape
    xs2 = x_sample.reshape(n_sample * l_sample, D_MODEL)
    short = dict(NSEQ=n_sample, L=l_sample, ns=SHORT_ROWS // l_sample)
    side_in, front = _short_group_front(xs2, sample_state0, P, **short)
    y_prompt, prompt_state, side_out = _run_long_group(x_prompt, meta_tokens.astype(F32), side_in,
                                                       l_sample, P, LONG_TS)
    y_sample, sample_state = _short_group_back(xs2, front, side_out, sample_state0[5], P, **short)
    return (y_prompt, y_sample) + unpack_state(prompt_state) + unpack_state(sample_state)
```

```python
import functools
from typing import Any, NamedTuple

import jax
import jax.numpy as jnp
from jax import lax
from jax.experimental import pallas as pl
from jax.experimental.pallas import tpu as pltpu

F32 = jnp.float32
BF16 = jnp.bfloat16

D_MODEL = 1024
D_LRU = 1024
LRU_CONV = 4
LRU_C = 8.0
N_HEADS = 4
D_HEAD = 256
D_FF = 2816
FFN_CONV = 3
EPS = 1e-6

V7X_LANES = 128
V7X_SUBLANES = 8
V7X_MXU_DIM = 256
VMEM_MB_RESIDENT_HALF = 56
VMEM_MB_ROW_TILED = 48
NEG_BIG = -1e30

LONG_TS = 256
SHORT_ROWS = 256
FFN_CHUNK = 256

N_MAIN = 7 * D_MODEL
N_W5 = 5 * D_MODEL
N_W2 = 2 * D_MODEL
W_PITCH_PAD = V7X_LANES
GATE_W = 2 * V7X_LANES
COL_U, COL_Q, COL_K, COL_V, COL_O, COL_GA, COL_GB = (j * D_MODEL for j in range(7))


def _resident(shape):
    return pl.BlockSpec(shape, lambda *_: (0,) * len(shape), pipeline_mode=pl.Buffered(1))


def _params(n_grid, vmem_mb):
    return pltpu.CompilerParams(
        dimension_semantics=("arbitrary",) * n_grid,
        vmem_limit_bytes=vmem_mb * 1024 * 1024,
    )


def _rms(x, g):
    ms = jnp.mean(x * x, axis=-1, keepdims=True)
    return x * lax.rsqrt(ms + EPS) * g


def _in_proj(xn, w5_ref, w2_ref, b_ref, col, width):
    if col < N_W5:
        w = w5_ref[:, col:col + width]
    else:
        w = w2_ref[:, col - N_W5:col - N_W5 + width]
    return jnp.dot(xn, w, preferred_element_type=F32) + b_ref[:, col:col + width]


def _in_proj_specs():
    return [_resident((1, D_MODEL)), _resident((D_MODEL, N_W5 + W_PITCH_PAD)),
            _resident((D_MODEL, N_W2 + W_PITCH_PAD)),
            _resident((1, N_MAIN)), _resident((D_MODEL, GATE_W)), _resident((1, GATE_W))]


def _in_proj_weights(P):
    return (P["norm1_g"], P["w5"], P["w2"], P["b_main"], P["w_gate"], P["b_gate"])


def _conv_init(ext_ref, hist0_ref, taps, hist_major=False):
    pad, hist = V7X_SUBLANES, taps - 1
    for g in range(ext_ref.shape[1]):
        ls = slice(g * V7X_LANES, (g + 1) * V7X_LANES)
        if hist_major:
            for k in range(hist):
                ext_ref[:, g, pad - hist + k, :] = hist0_ref[k, :, ls]
        else:
            ext_ref[:, g, pad - hist:pad, :] = hist0_ref[:, :, ls]


def _causal_conv(ext_ref, x3, cw_ref, cb_ref, hist_out_ref, col0=0, hist_major=False):
    taps = cw_ref.shape[0]
    ts = x3.shape[1]
    pad, hist = V7X_SUBLANES, taps - 1
    outs = []
    for k in range(x3.shape[2] // V7X_LANES):
        g = col0 // V7X_LANES + k
        ls = slice(g * V7X_LANES, (g + 1) * V7X_LANES)
        xg = x3[:, :, k * V7X_LANES:(k + 1) * V7X_LANES]
        ext_ref[:, g, pad:pad + ts, :] = xg
        acc = cb_ref[:, ls] + cw_ref[taps - 1:taps, ls] * xg
        for j in range(hist):
            acc = acc + cw_ref[j:j + 1, ls] * ext_ref[:, g, pad - hist + j:pad - hist + j + ts, :]
        outs.append(acc)
        new_hist = ext_ref[:, g, pad + ts - hist:pad + ts, :]
        ext_ref[:, g, pad - hist:pad, :] = new_hist
        if hist_major:
            for j in range(hist):
                hist_out_ref[j, :, ls] = new_hist[:, j, :]
        else:
            hist_out_ref[:, :, ls] = new_hist
    return jnp.concatenate(outs, axis=-1)


def _lru_body(NS, TS, u2, cw_ref, cb_ref, wr_ref, br, wi_ref, bi, lam, ext_ref, h_ref, convo_ref,
              ho_ref, hist_major=False):
    R = NS * TS
    C = D_LRU
    SB = V7X_SUBLANES
    uc2 = _causal_conv(ext_ref, u2.reshape(NS, TS, C), cw_ref, cb_ref, convo_ref,
                       hist_major=hist_major).reshape(R, C)
    ucb = uc2.astype(BF16)

    def block_diag(w_ref):
        W = V7X_MXU_DIM
        return jnp.concatenate(
            [jnp.dot(ucb[:, g * W:(g + 1) * W], w_ref[g], preferred_element_type=F32)
             for g in range(C // W)], axis=1)

    r = jax.nn.sigmoid(block_diag(wr_ref) + br)
    i = jax.nn.sigmoid(block_diag(wi_ref) + bi)
    log_a = -LRU_C * r * jax.nn.softplus(-lam)
    a = jnp.exp(log_a)
    hh = jnp.sqrt(-jnp.tanh(log_a) * (a * a + 1.0)) * (i * uc2)

    a = a.reshape(R // SB, SB, C)
    hh = hh.reshape(R // SB, SB, C)
    sub = lax.broadcasted_iota(jnp.int32, (1, SB, C), 1)
    for d in (1 << p for p in range(SB.bit_length() - 1)):
        keep = sub >= d
        a_sh = pltpu.roll(a, d, axis=1)
        h_sh = pltpu.roll(hh, d, axis=1)
        hh = hh + a * jnp.where(keep, h_sh, 0.0)
        a = a * jnp.where(keep, a_sh, 1.0)

    nb = TS // SB
    a = a.reshape(NS, nb, SB, C)
    hh = hh.reshape(NS, nb, SB, C)
    h = jnp.broadcast_to(h_ref[...], (NS, SB, C))
    blocks = []
    for j in range(nb):
        hj = hh[:, j] + a[:, j] * h
        blocks.append(hj)
        h = jnp.broadcast_to(hj[:, SB - 1:, :], (NS, SB, C))
    h_ref[...] = h[:, 0:1, :]
    ho_ref[...] = h[:, 0:1, :]
    return jnp.concatenate(blocks, axis=1).reshape(R, C)


def _seg_scan(x, tpos, TS, op, ident):
    d = 1
    while d < TS:
        sh = pltpu.roll(x, d, axis=0)
        x = op(x, jnp.where(tpos >= d, sh, ident))
        d *= 2
    return x


def _pad_rows(x, rows):
    if x.shape[0] >= rows:
        return x
    return jnp.concatenate([x, jnp.zeros((rows - x.shape[0],) + x.shape[1:], x.dtype)], axis=0)


class MlstmGates(NamedTuple):
    c4: Any
    big_m4: Any
    e4: Any
    dinv4: Any
    wk4: Any
    decay4: Any


def _mlstm_gates(NS, TS, gt, m_in, m_out):
    R = NS * TS
    LN = V7X_LANES
    ig4 = gt[:, :LN]
    lf4 = jax.nn.log_sigmoid(gt[:, LN:])
    tpos = lax.broadcasted_iota(jnp.int32, (R, LN), 0) & (TS - 1)
    b4 = _seg_scan(lf4, tpos, TS, jnp.add, 0.0)
    c4 = ig4 - b4
    cmax4 = _seg_scan(c4, tpos, TS, jnp.maximum, -jnp.inf)
    m_prev = [m_in[j] for j in range(NS)]
    m_rows = jnp.concatenate([jnp.broadcast_to(m, (TS, LN)) for m in m_prev], axis=0)
    big_m4 = jnp.maximum(cmax4, m_rows)
    e4 = jnp.exp(m_rows - big_m4)
    dinv4 = jnp.exp(-(b4 + big_m4))

    decay4, wk_parts = [], []
    for j in range(NS):
        b_last = b4[(j + 1) * TS - 1:(j + 1) * TS, :]
        g4 = b_last + c4[j * TS:(j + 1) * TS, :]
        mn = jnp.maximum(b_last + m_prev[j], jnp.max(g4, axis=0, keepdims=True))
        decay4.append(jnp.exp(b_last + m_prev[j] - mn))
        wk_parts.append(jnp.exp(g4 - mn))
        m_out[j] = mn
    return MlstmGates(c4, big_m4, e4, dinv4, jnp.concatenate(wk_parts, axis=0), decay4)


def _mlstm_heads(NS, TS, get_qkvo, gates: MlstmGates, hg_ref, st_in, st_out):
    R = NS * TS
    RC = max(R, V7X_LANES)
    shift = TS.bit_length() - 1
    c4, big_m4, e4, dinv4, wk4, decay4 = gates
    c_in, n_in, _ = st_in
    c_out, n_out, _ = st_out

    ri = lax.broadcasted_iota(jnp.int32, (R, RC), 0)
    ci = lax.broadcasted_iota(jnp.int32, (R, RC), 1)
    eye = ri == ci
    if NS == 1:
        causal = ci <= ri
    else:
        causal = (ci <= ri) & ((ri >> shift) == (ci >> shift))
    seq_of_row = lax.broadcasted_iota(jnp.int32, (R, D_HEAD), 0) >> shift

    outs = []
    ahead = get_qkvo(0)
    for h in range(N_HEADS):
        sl = slice(h * D_HEAD, (h + 1) * D_HEAD)
        qh, kh, vh, oh = ahead
        if h + 1 < N_HEADS:
            ahead = get_qkvo(h + 1)
        kh_p = _pad_rows(kh, RC)
        vh_p = _pad_rows(vh, RC)
        c_c = c4[:, h:h + 1]
        big_m_c = big_m4[:, h:h + 1]
        e_c = e4[:, h:h + 1]
        dinv_c = dinv4[:, h:h + 1]
        wk_c = wk4[:, h:h + 1]

        qk = lax.dot_general(qh, kh_p, (((1,), (1,)), ((), ())), preferred_element_type=F32)
        if NS == 1:
            q_c = jnp.dot(qh, c_in[0, h].astype(BF16), preferred_element_type=F32)
            n_rows = n_in[0, h:h + 1, :]
        else:
            q_c = jnp.zeros((R, D_HEAD), F32)
            n_rows = jnp.zeros((R, D_HEAD), F32)
            for j in range(NS):
                mine = seq_of_row == j
                q_c = jnp.where(mine, jnp.dot(qh, c_in[j, h].astype(BF16),
                                              preferred_element_type=F32), q_c)
                n_rows = jnp.where(mine, n_in[j, h:h + 1, :], n_rows)
        kw = kh.astype(F32) * wk_c
        kws = [kw if NS == 1 else jnp.where(seq_of_row == j, kw, 0.0) for j in range(NS)]
        upds = [lax.dot_general(_pad_rows(kwj, RC).astype(BF16), vh_p,
                                (((0,), (0,)), ((), ())), preferred_element_type=F32)
                for kwj in kws]

        c_r = jnp.sum(jnp.where(eye, c_c, 0.0), axis=0, keepdims=True)
        w = jnp.exp(jnp.where(causal, c_r - big_m_c, NEG_BIG))
        s = qk * w
        den = jnp.sum(s, axis=1, keepdims=True)
        num = jnp.dot(s.astype(BF16), vh_p, preferred_element_type=F32)
        q_n = jnp.sum(qh.astype(F32) * n_rows, axis=1, keepdims=True)
        num = num + e_c * q_c
        den = den + e_c * q_n
        hh = num * (1.0 / jnp.maximum(jnp.abs(den), dinv_c))
        hh = hh * lax.rsqrt(jnp.mean(hh * hh, axis=1, keepdims=True) + EPS)
        outs.append(((hh * hg_ref[:, sl]) * jax.nn.sigmoid(oh)).astype(BF16))

        for j in range(NS):
            dec = decay4[j][:, h:h + 1]
            c_out[j, h] = dec * c_in[j, h] + upds[j]
            n_out[j, h:h + 1, :] = (dec * n_in[j, h:h + 1, :]
                                    + jnp.sum(kws[j], axis=0, keepdims=True))
    return outs


def _merge_out(x, ha, hb, ga, gb, wa_ref, wb_ref, wo_ref):
    pa = jnp.dot(ha, wa_ref[:, :D_MODEL], preferred_element_type=F32)
    pb = jnp.dot(hb, wb_ref[:, :D_MODEL], preferred_element_type=F32)
    merged = jax.nn.sigmoid(ga) * pa + jax.nn.sigmoid(gb) * pb
    return x + jnp.dot(merged.astype(BF16), wo_ref[:, :D_MODEL], preferred_element_type=F32)


class MixerIn(NamedTuple):
    x: Any
    meta: Any
    g: Any
    w5: Any
    w2: Any
    b: Any
    wg: Any
    bg: Any
    cw: Any
    cb: Any
    wr: Any
    br: Any
    wi: Any
    bi: Any
    lam: Any
    hg: Any
    wa: Any
    wb: Any
    wo: Any


class MixerOut(NamedTuple):
    x1: Any
    meta_x1: Any
    conv: Any
    h: Any
    c: Any
    n: Any
    m: Any


class MixerScratch(NamedTuple):
    ext: Any
    h: Any
    c: Any
    n: Any
    m: Any
    hist0: Any
    h0: Any
    c0: Any
    n0: Any
    m0: Any


class SideIn(NamedTuple):
    q: Any
    k: Any
    v: Any
    gt: Any
    o: Any
    c0: Any
    n0: Any
    m0: Any


class SideOut(NamedTuple):
    hb: Any
    c: Any
    n: Any
    m: Any


def _split_refs(refs, *kinds):
    out, pos = [], 0
    for kind in kinds:
        n = len(kind._fields)
        out.append(kind(*refs[pos:pos + n]))
        pos += n
    assert pos == len(refs)
    return out


def _mixer_tile(TS, x, i: MixerIn, o: MixerOut, s: MixerScratch):
    xn = _rms(x, i.g[...]).astype(BF16)
    proj = functools.partial(_in_proj, xn, i.w5, i.w2, i.b)

    def get_qkvo(h):
        off = h * D_HEAD
        q = (proj(COL_Q + off, D_HEAD) * (D_HEAD ** -0.5)).astype(BF16)
        return (q, proj(COL_K + off, D_HEAD).astype(BF16), proj(COL_V + off, D_HEAD).astype(BF16),
                proj(COL_O + off, D_HEAD))

    gt = jnp.dot(xn, i.wg[...], preferred_element_type=F32) + i.bg[...]
    gates = _mlstm_gates(1, TS, gt, s.m, s.m)
    hs = _lru_body(1, TS, proj(COL_U, D_LRU), i.cw, i.cb, i.wr, i.br[...],
                   i.wi, i.bi[...], i.lam[...], s.ext, s.h, o.conv, o.h)
    state = (s.c, s.n, s.m)
    hb = jnp.concatenate(_mlstm_heads(1, TS, get_qkvo, gates, i.hg, state, state), axis=1)
    return _merge_out(x, hs.astype(BF16), hb, proj(COL_GA, D_MODEL), proj(COL_GB, D_MODEL),
                      i.wa, i.wb, i.wo)


def _mixer_kernel(TS, SIDE_NS, SIDE_TS, *refs):
    i, si, o, so, s = _split_refs(refs, MixerIn, SideIn, MixerOut, SideOut, MixerScratch)
    seq, ti = pl.program_id(0), pl.program_id(1)
    hist_rows = slice(V7X_SUBLANES - (LRU_CONV - 1), V7X_SUBLANES)

    @pl.when((seq == 0) & (ti == 0))
    def _():
        s.ext[:, :, hist_rows, :] = jnp.zeros_like(s.hist0)
        for ref in (s.h, s.c, s.n, s.m):
            ref[...] = jnp.zeros_like(ref)
        o.meta_x1[...] = _mixer_tile(i.meta.shape[0], i.meta[...], i, o, s)
        s.hist0[...] = s.ext[:, :, hist_rows, :]
        for ref0, ref in ((s.h0, s.h), (s.c0, s.c), (s.n0, s.n), (s.m0, s.m)):
            ref0[...] = ref[...]

    @pl.when(ti == 0)
    def _():
        s.ext[:, :, hist_rows, :] = s.hist0[...]
        for ref0, ref in ((s.h0, s.h), (s.c0, s.c), (s.n0, s.n), (s.m0, s.m)):
            ref[...] = ref0[...]

    def get_qkvo(h):
        sl = slice(h * D_HEAD, (h + 1) * D_HEAD)
        return si.q[:, sl], si.k[:, sl], si.v[:, sl], si.o[:, sl]

    gates = _mlstm_gates(SIDE_NS, SIDE_TS, si.gt[...], si.m0, so.m)
    outs = _mlstm_heads(SIDE_NS, SIDE_TS, get_qkvo, gates, i.hg,
                        (si.c0, si.n0, si.m0), (so.c, so.n, so.m))
    for h, out in enumerate(outs):
        so.hb[:, h * D_HEAD:(h + 1) * D_HEAD] = out

    o.x1[...] = _mixer_tile(TS, i.x[...], i, o, s)

    @pl.when(ti == pl.num_programs(1) - 1)
    def _():
        o.c[...] = s.c[...]
        o.n[...] = s.n[...]
        o.m[...] = s.m[...]


def _lru_weight_specs():
    nb = D_LRU // V7X_MXU_DIM
    return [_resident((LRU_CONV, D_LRU)), _resident((1, D_LRU)),
            _resident((nb, V7X_MXU_DIM, V7X_MXU_DIM)), _resident((1, D_LRU)),
            _resident((nb, V7X_MXU_DIM, V7X_MXU_DIM)), _resident((1, D_LRU)),
            _resident((1, D_LRU))]


def _lru_weights(P):
    return (P["lru_conv_w"], P["lru_conv_b"], P["w_r"], P["lru_b_r"], P["w_i"], P["lru_b_i"],
            P["lru_lambda"])


def _state_specs(NS):
    def spec(*tail):
        zeros = (0,) * len(tail)
        return pl.BlockSpec((NS,) + tail, lambda s, t: (s,) + zeros)

    return [spec(LRU_CONV - 1, D_LRU), spec(1, D_LRU), spec(N_HEADS, D_HEAD, D_HEAD),
            spec(N_HEADS, D_HEAD), spec(1, V7X_LANES)]


def _state_shapes(NSEQ):
    return [jax.ShapeDtypeStruct((NSEQ, LRU_CONV - 1, D_LRU), F32),
            jax.ShapeDtypeStruct((NSEQ, 1, D_LRU), F32),
            jax.ShapeDtypeStruct((NSEQ, N_HEADS, D_HEAD, D_HEAD), F32),
            jax.ShapeDtypeStruct((NSEQ, N_HEADS, D_HEAD), F32),
            jax.ShapeDtypeStruct((NSEQ, 1, V7X_LANES), F32)]


def _mixer(x2, meta, side_in: SideIn, side_len, P, NSEQ, L, TS):
    NT = L // TS
    n_meta = meta.shape[0]
    rows = pl.BlockSpec((TS, D_MODEL), lambda s, t: (s * NT + t, 0))
    wsq = _resident((D_MODEL, D_MODEL + W_PITCH_PAD))
    n_side = side_in.c0.shape[0]
    side_ns = n_side // (NSEQ * NT)
    assert side_ns * NSEQ * NT == n_side
    step = lambda s, t: s * NT + t
    srows = lambda w: pl.BlockSpec((side_ns * side_len, w), lambda s, t: (step(s, t), 0))
    sstate = [pl.BlockSpec((side_ns, N_HEADS, D_HEAD, D_HEAD), lambda s, t: (step(s, t), 0, 0, 0)),
              pl.BlockSpec((side_ns, N_HEADS, D_HEAD), lambda s, t: (step(s, t), 0, 0)),
              pl.BlockSpec((side_ns, 1, V7X_LANES), lambda s, t: (step(s, t), 0, 0))]
    state_scratch = [pltpu.VMEM((1, 1, D_LRU), F32),
                     pltpu.VMEM((1, N_HEADS, D_HEAD, D_HEAD), F32),
                     pltpu.VMEM((1, N_HEADS, D_HEAD), F32),
                     pltpu.VMEM((1, 1, V7X_LANES), F32)]
    slabs = D_LRU // V7X_LANES
    return pl.pallas_call(
        functools.partial(_mixer_kernel, TS, side_ns, side_len),
        grid=(NSEQ, NT),
        in_specs=([rows, _resident((n_meta, D_MODEL))] + _in_proj_specs() + _lru_weight_specs()
                  + [_resident((1, D_MODEL)), wsq, wsq, wsq]
                  + [srows(D_MODEL)] * 3 + [srows(GATE_W), srows(D_MODEL)] + sstate),
        out_specs=([rows, pl.BlockSpec((n_meta, D_MODEL), lambda s, t: (0, 0))] + _state_specs(1)
                   + [srows(D_MODEL)] + sstate),
        out_shape=([jax.ShapeDtypeStruct((NSEQ * L, D_MODEL), F32),
                    jax.ShapeDtypeStruct((n_meta, D_MODEL), F32)] + _state_shapes(NSEQ)
                   + [jax.ShapeDtypeStruct((n_side * side_len, D_MODEL), BF16)]
                   + _state_shapes(n_side)[2:]),
        scratch_shapes=([pltpu.VMEM((1, slabs, V7X_SUBLANES + TS, V7X_LANES), F32)] + state_scratch
                        + [pltpu.VMEM((1, slabs, LRU_CONV - 1, V7X_LANES), F32)] + state_scratch),
        compiler_params=_params(2, VMEM_MB_RESIDENT_HALF),
        name="mixer",
    )(x2, meta, *_in_proj_weights(P), *_lru_weights(P), P["mlstm_head_g"],
      P["w_branch_a"], P["w_branch_b"], P["w_out"], *side_in)


def _front_kernel(NS, TS, x_ref, conv0_ref, h0_ref, g_ref, w5_ref, w2_ref, b_ref, wg_ref, bg_ref,
                  cw_ref, cb_ref, wr_ref, br_ref, wi_ref, bi_ref, lam_ref,
                  ha_ref, convo_ref, ho_ref, q_ref, k_ref, v_ref, o_ref, ga_ref, gb_ref, gt_ref,
                  ext_ref, h_s):
    _conv_init(ext_ref, conv0_ref, LRU_CONV, hist_major=True)
    h_s[...] = h0_ref[...]
    xn = _rms(x_ref[...], g_ref[...]).astype(BF16)
    proj = functools.partial(_in_proj, xn, w5_ref, w2_ref, b_ref)
    gt_ref[...] = jnp.dot(xn, wg_ref[...], preferred_element_type=F32) + bg_ref[...]
    hs = _lru_body(NS, TS, proj(COL_U, D_LRU), cw_ref, cb_ref, wr_ref, br_ref[...], wi_ref,
                   bi_ref[...], lam_ref[...], ext_ref, h_s, convo_ref, ho_ref, hist_major=True)
    ha_ref[...] = hs.astype(BF16)
    q_ref[...] = (proj(COL_Q, D_MODEL) * (D_HEAD ** -0.5)).astype(BF16)
    k_ref[...] = proj(COL_K, D_MODEL).astype(BF16)
    v_ref[...] = proj(COL_V, D_MODEL).astype(BF16)
    o_ref[...] = proj(COL_O, D_MODEL)
    ga_ref[...] = proj(COL_GA, D_MODEL)
    gb_ref[...] = proj(COL_GB, D_MODEL)


def _front(x2, conv0, h0, P, NSEQ, TS, NS):
    R = NS * TS
    M = NSEQ * TS
    row = lambda w: pl.BlockSpec((R, w), lambda s: (s, 0))
    st = [pl.BlockSpec((LRU_CONV - 1, NS, D_LRU), lambda s: (0, s, 0)),
          pl.BlockSpec((NS, 1, D_LRU), lambda s: (s, 0, 0))]
    f32o = jax.ShapeDtypeStruct((M, D_MODEL), F32)
    bf16o = jax.ShapeDtypeStruct((M, D_MODEL), BF16)
    return pl.pallas_call(
        functools.partial(_front_kernel, NS, TS),
        grid=(NSEQ // NS,),
        in_specs=[row(D_MODEL)] + st + _in_proj_specs() + _lru_weight_specs(),
        out_specs=[row(D_LRU)] + st + [row(D_MODEL)] * 6 + [row(GATE_W)],
        out_shape=([bf16o, jax.ShapeDtypeStruct((LRU_CONV - 1, NSEQ, D_LRU), F32),
                    _state_shapes(NSEQ)[1], bf16o, bf16o, bf16o, f32o, f32o, f32o]
                   + [jax.ShapeDtypeStruct((M, GATE_W), F32)]),
        scratch_shapes=[pltpu.VMEM((NS, D_LRU // V7X_LANES, V7X_SUBLANES + TS, V7X_LANES), F32),
                        pltpu.VMEM((NS, 1, D_LRU), F32)],
        compiler_params=_params(1, VMEM_MB_ROW_TILED),
        name="front",
    )(x2, conv0, h0, *_in_proj_weights(P), *_lru_weights(P))


def _back_kernel(NS, TS, x_ref, ha_ref, hb_ref, ga_ref, gb_ref, st0_ref, wa_ref, wb_ref, wo_ref,
                 g2_ref, wup_ref, cw_ref, cb_ref, wdn_ref, gf_ref, y_ref, sto_ref, ext_ref):
    _conv_init(ext_ref, st0_ref, FFN_CONV)
    x1 = _merge_out(x_ref[...], ha_ref[...], hb_ref[...], ga_ref[...], gb_ref[...],
                    wa_ref, wb_ref, wo_ref)
    y_ref[...] = _ffn_tile(NS, TS, x1, g2_ref, wup_ref, cw_ref, cb_ref, wdn_ref, gf_ref,
                           sto_ref, ext_ref)


def _ffn_meta_kernel(TS, x1_ref, meta_ref, g2_ref, wup_ref, cw_ref, cb_ref, wdn_ref, gf_ref,
                     y_ref, sto_ref, ext_ref, hist0_ref):
    seq, ti = pl.program_id(0), pl.program_id(1)
    hist = FFN_CONV - 1
    hist_rows = slice(V7X_SUBLANES - hist, V7X_SUBLANES)

    @pl.when((seq == 0) & (ti == 0))
    def _():
        xm = _rms(meta_ref[...], g2_ref[...]).astype(BF16)
        up = jnp.dot(xm, wup_ref[...], preferred_element_type=F32)
        for g in range(hist0_ref.shape[1]):
            hist0_ref[:, g, :, :] = up[None, up.shape[0] - hist:, g * V7X_LANES:(g + 1) * V7X_LANES]

    @pl.when(ti == 0)
    def _():
        ext_ref[:, :, hist_rows, :] = hist0_ref[...]

    y_ref[...] = _ffn_tile(1, TS, x1_ref[...], g2_ref, wup_ref, cw_ref, cb_ref, wdn_ref, gf_ref,
                           sto_ref, ext_ref)


def _ffn_tile(NS, TS, x1, g2_ref, wup_ref, cw_ref, cb_ref, wdn_ref, gf_ref, sto_ref, ext_ref):
    R = NS * TS
    xn = _rms(x1, g2_ref[...]).astype(BF16)

    def up_conv(col0):
        up = jnp.dot(xn, wup_ref[:, col0:col0 + FFN_CHUNK], preferred_element_type=F32)
        return _causal_conv(ext_ref, up.reshape(NS, TS, FFN_CHUNK), cw_ref, cb_ref, sto_ref,
                            col0).reshape(R, FFN_CHUNK)

    act = jnp.concatenate(
        [(jax.nn.gelu(up_conv(D_FF + c)) * up_conv(c)).astype(BF16)
         for c in range(0, D_FF, FFN_CHUNK)], axis=1)
    x2 = x1 + jnp.dot(act, wdn_ref[:, :D_MODEL], preferred_element_type=F32)
    return _rms(x2, gf_ref[...])


def _ffn_weight_specs():
    W = 2 * D_FF
    return [_resident((1, D_MODEL)), _resident((D_MODEL, W)), _resident((FFN_CONV, W)),
            _resident((1, W)), _resident((D_FF, D_MODEL + W_PITCH_PAD)), _resident((1, D_MODEL))]


def _ffn_weights(P):
    return (P["norm2_g"], P["w_up"], P["ffn_conv_w"], P["ffn_conv_b"], P["w_down"], P["final_g"])


def _ffn_long(x1, meta_x1, P, NSEQ, L, TS):
    NT = L // TS
    W = 2 * D_FF
    slabs = W // V7X_LANES
    rows = pl.BlockSpec((TS, D_MODEL), lambda s, t: (s * NT + t, 0))
    return pl.pallas_call(
        functools.partial(_ffn_meta_kernel, TS),
        grid=(NSEQ, NT),
        in_specs=[rows, _resident(meta_x1.shape)] + _ffn_weight_specs(),
        out_specs=[rows, pl.BlockSpec((1, FFN_CONV - 1, W), lambda s, t: (s, 0, 0))],
        out_shape=[jax.ShapeDtypeStruct((NSEQ * L, D_MODEL), F32),
                   jax.ShapeDtypeStruct((NSEQ, FFN_CONV - 1, W), F32)],
        scratch_shapes=[pltpu.VMEM((1, slabs, V7X_SUBLANES + TS, V7X_LANES), F32),
                        pltpu.VMEM((1, slabs, FFN_CONV - 1, V7X_LANES), F32)],
        compiler_params=_params(2, VMEM_MB_RESIDENT_HALF),
        name="ffn",
    )(x1, meta_x1, *_ffn_weights(P))


def _back(x2, ha, hb, ga, gb, st0, P, NSEQ, TS, NS):
    R = NS * TS
    W = 2 * D_FF
    row = pl.BlockSpec((R, D_MODEL), lambda s: (s, 0))
    stspec = pl.BlockSpec((NS, FFN_CONV - 1, W), lambda s: (s, 0, 0))
    wsq = _resident((D_MODEL, D_MODEL + W_PITCH_PAD))
    return pl.pallas_call(
        functools.partial(_back_kernel, NS, TS),
        grid=(NSEQ // NS,),
        in_specs=[row] * 5 + [stspec, wsq, wsq, wsq] + _ffn_weight_specs(),
        out_specs=[row, stspec],
        out_shape=[jax.ShapeDtypeStruct((NSEQ * TS, D_MODEL), F32),
                   jax.ShapeDtypeStruct((NSEQ, FFN_CONV - 1, W), F32)],
        scratch_shapes=[pltpu.VMEM((NS, W // V7X_LANES, V7X_SUBLANES + TS, V7X_LANES), F32)],
        compiler_params=_params(1, VMEM_MB_RESIDENT_HALF),
        name="back",
    )(x2, ha, hb, ga, gb, st0, P["w_branch_a"], P["w_branch_b"], P["w_out"], *_ffn_weights(P))


def _block_diag(w):
    bw = w.shape[1]
    per = V7X_MXU_DIM // bw
    nb = w.shape[0] // per
    w4 = w.reshape(nb, per, bw, 1, bw)
    on_diag = jnp.eye(per, dtype=w.dtype).reshape(1, per, 1, per, 1)
    return (w4 * on_diag).reshape(nb, V7X_MXU_DIM, V7X_MXU_DIM)


def _pitch_padded(w):
    pad = jnp.zeros((w.shape[0], W_PITCH_PAD), BF16)
    return jnp.concatenate([w.astype(BF16), pad], axis=1)


def _prep_weights_kernel(wt_ref, wup_ref, wdn_ref, wa_ref, wb_ref, wo_ref,
                         w5_ref, w2_ref, wg_ref, wup_o, wdn_o, wa_o, wb_o, wo_o):
    wt = wt_ref[...]
    n_gate = 2 * N_HEADS
    w5_ref[...] = _pitch_padded(wt[:N_W5].T)
    w2_ref[...] = _pitch_padded(wt[N_W5 + n_gate:].T)
    g = jnp.concatenate([wt[N_W5:N_W5 + n_gate],
                         jnp.zeros((V7X_LANES - n_gate, wt.shape[1]), F32)], axis=0).T
    head_lane = lax.broadcasted_iota(jnp.int32, g.shape, 1) < N_HEADS
    ig = jnp.where(head_lane, g, 0.0)
    fg = jnp.where(head_lane, pltpu.roll(g, V7X_LANES - N_HEADS, axis=1), 0.0)
    wg_ref[...] = jnp.concatenate([ig, fg], axis=1).astype(BF16)

    wup_o[...] = wup_ref[0].astype(BF16)
    wdn_o[...] = _pitch_padded(wdn_ref[0])
    wa_o[...] = _pitch_padded(wa_ref[0])
    wb_o[...] = _pitch_padded(wb_ref[0])
    wo_o[...] = _pitch_padded(wo_ref[0])


def _prep_weights(w_in, w_up, w_down, w_branch_a, w_branch_b, w_out):
    n_in = w_in.shape[2]
    assert n_in == N_W5 + 2 * N_HEADS + N_W2
    steps = D_MODEL // V7X_LANES
    rows_dn = D_FF // steps
    wt = jnp.transpose(w_in[0])
    slab3 = lambda r, w: pl.BlockSpec((1, r, w), lambda i: (0, i, 0))
    slab = lambda r, w: pl.BlockSpec((r, w), lambda i: (i, 0))
    sq_pad = D_MODEL + W_PITCH_PAD
    bf = lambda r, w: jax.ShapeDtypeStruct((r, w), BF16)
    return pl.pallas_call(
        _prep_weights_kernel,
        grid=(steps,),
        in_specs=[pl.BlockSpec((n_in, V7X_LANES), lambda i: (0, i)),
                  slab3(V7X_LANES, 2 * D_FF), slab3(rows_dn, D_MODEL),
                  slab3(V7X_LANES, D_MODEL), slab3(V7X_LANES, D_MODEL), slab3(V7X_LANES, D_MODEL)],
        out_specs=[slab(V7X_LANES, N_W5 + W_PITCH_PAD), slab(V7X_LANES, N_W2 + W_PITCH_PAD),
                   slab(V7X_LANES, GATE_W), slab(V7X_LANES, 2 * D_FF), slab(rows_dn, sq_pad),
                   slab(V7X_LANES, sq_pad), slab(V7X_LANES, sq_pad), slab(V7X_LANES, sq_pad)],
        out_shape=[bf(D_MODEL, N_W5 + W_PITCH_PAD), bf(D_MODEL, N_W2 + W_PITCH_PAD),
                   bf(D_MODEL, GATE_W), bf(D_MODEL, 2 * D_FF), bf(D_FF, sq_pad),
                   bf(D_MODEL, sq_pad), bf(D_MODEL, sq_pad), bf(D_MODEL, sq_pad)],
        compiler_params=_params(1, VMEM_MB_ROW_TILED),
        name="prep_weights",
    )(wt, w_up, w_down, w_branch_a, w_branch_b, w_out)


def _run_long_group(x3, meta, side_in, side_len, P, TS):
    NSEQ, L, _ = x3.shape
    x2 = x3.reshape(NSEQ * L, D_MODEL)
    x1, meta_x1, conv1, h1, c1, n1, m1, *side_out = _mixer(x2, meta, side_in, side_len, P,
                                                           NSEQ, L, TS)
    y, ffn1 = _ffn_long(x1, meta_x1, P, NSEQ, L, TS)
    return y.reshape(NSEQ, L, D_MODEL), (conv1, h1, c1, n1, m1, ffn1), side_out


def _short_group_front(x2, state, P, NSEQ, L, ns):
    conv0, h0, c0, n0, m0, _ = state
    hist_major = lambda a: jnp.transpose(a, (1, 0, 2))
    ha, conv1, h1, q, k, v, o, ga, gb, gt = _front(x2, hist_major(conv0), h0, P, NSEQ, L, ns)
    return SideIn(q, k, v, gt, o, c0, n0, m0), (ha, ga, gb, hist_major(conv1), h1)


def _short_group_back(x2, front, side_out, ffn0, P, NSEQ, L, ns):
    ha, ga, gb, conv1, h1 = front
    hb, c1, n1, m1 = side_out
    y, ffn1 = _back(x2, ha, hb, ga, gb, ffn0, P, NSEQ, L, ns)
    return y.reshape(NSEQ, L, D_MODEL), (conv1, h1, c1, n1, m1, ffn1)


def kernel(x_prompt, x_sample, state_lru_conv, state_lru_h, state_mlstm_C, state_mlstm_n,
           state_mlstm_m, state_ffn_conv, meta_tokens, norm1_g, w_in, b_in, lru_conv_w,
           lru_conv_b, lru_w_r, lru_b_r, lru_w_i, lru_b_i, lru_lambda, mlstm_head_g,
           w_branch_a, w_branch_b, w_out, norm2_g, w_up, ffn_conv_w, ffn_conv_b, w_down, final_g):
    assert w_in.shape[0] == 1, "single-layer trunk"
    b0 = b_in[0]
    gate_pad = jnp.zeros((V7X_LANES - N_HEADS,), b0.dtype)
    row = lambda a: a.reshape(1, -1).astype(F32)
    w5, w2, w_gate, w_up_b, w_down_b, w_a_b, w_b_b, w_o_b = _prep_weights(
        w_in, w_up, w_down, w_branch_a, w_branch_b, w_out)
    P = {
        "norm1_g": row(norm1_g[0]),
        "w5": w5,
        "w2": w2,
        "b_main": row(jnp.concatenate([b0[:N_W5], b0[N_W5 + 2 * N_HEADS:]])),
        "w_gate": w_gate,
        "b_gate": row(jnp.concatenate([b0[N_W5:N_W5 + N_HEADS], gate_pad,
                                       b0[N_W5 + N_HEADS:N_W5 + 2 * N_HEADS], gate_pad])),
        "lru_conv_w": lru_conv_w[0],
        "lru_conv_b": row(lru_conv_b[0]),
        "w_r": _block_diag(lru_w_r[0]).astype(BF16),
        "lru_b_r": row(lru_b_r[0]),
        "w_i": _block_diag(lru_w_i[0]).astype(BF16),
        "lru_b_i": row(lru_b_i[0]),
        "lru_lambda": row(lru_lambda[0]),
        "mlstm_head_g": row(mlstm_head_g[0]),
        "w_branch_a": w_a_b,
        "w_branch_b": w_b_b,
        "w_out": w_o_b,
        "norm2_g": row(norm2_g[0]),
        "w_up": w_up_b,
        "ffn_conv_w": ffn_conv_w[0],
        "ffn_conv_b": row(ffn_conv_b[0]),
        "w_down": w_down_b,
        "final_g": row(final_g),
    }

    def pack_state(conv, h, c, n, m, ffn):
        nseq = h.shape[0]
        m_pad = jnp.pad(m.astype(F32)[:, None, :], ((0, 0), (0, 0), (0, V7X_LANES - N_HEADS)))
        return (conv.astype(F32), h.astype(F32).reshape(nseq, 1, D_LRU), c.astype(F32),
                n.astype(F32), m_pad, ffn.astype(F32))

    def unpack_state(st):
        conv, h, c, n, m, ffn = st
        return (conv[None], h.reshape(1, -1, D_LRU), c[None], n[None],
                m[:, 0, :N_HEADS][None], ffn[None])

    sample_state0 = pack_state(state_lru_conv[0], state_lru_h[0], state_mlstm_C[0],
                               state_mlstm_n[0], state_mlstm_m[0], state_ffn_conv[0])
    n_sample, l_sample, _ = x_sample.shape
    xs2 = x_sample.reshape(n_sample * l_sample, D_MODEL)
    short = dict(NSEQ=n_sample, L=l_sample, ns=SHORT_ROWS // l_sample)
    side_in, front = _short_group_front(xs2, sample_state0, P, **short)
    y_prompt, prompt_state, side_out = _run_long_group(x_prompt, meta_tokens.astype(F32), side_in,
                                                       l_sample, P, LONG_TS)
    y_sample, sample_state = _short_group_back(xs2, front, side_out, sample_state0[5], P, **short)
    return (y_prompt, y_sample) + unpack_state(prompt_state) + unpack_state(sample_state)
```

```python
import functools
from typing import Any, NamedTuple

import jax
import jax.numpy as jnp
from jax import lax
from jax.experimental import pallas as pl
from jax.experimental.pallas import tpu as pltpu

F32 = jnp.float32
BF16 = jnp.bfloat16

D_MODEL = 1024
D_LRU = 1024
LRU_CONV = 4
LRU_C = 8.0
N_HEADS = 4
D_HEAD = 256
D_FF = 2816
FFN_CONV = 3
EPS = 1e-6

V7X_LANES = 128
V7X_SUBLANES = 8
V7X_MXU_DIM = 256
VMEM_MB_RESIDENT_HALF = 56
VMEM_MB_ROW_TILED = 48
NEG_BIG = -1e30

LONG_TS = 256
SHORT_ROWS = 256
FFN_CHUNK = 256

N_MAIN = 7 * D_MODEL
N_W5 = 5 * D_MODEL
N_W2 = 2 * D_MODEL
W_PITCH_PAD = V7X_LANES
GATE_W = 2 * V7X_LANES
COL_U, COL_Q, COL_K, COL_V, COL_O, COL_GA, COL_GB = (j * D_MODEL for j in range(7))


def _resident(shape):
    return pl.BlockSpec(shape, lambda *_: (0,) * len(shape), pipeline_mode=pl.Buffered(1))


def _params(n_grid, vmem_mb):
    return pltpu.CompilerParams(
        dimension_semantics=("arbitrary",) * n_grid,
        vmem_limit_bytes=vmem_mb * 1024 * 1024,
    )


def _rms(x, g):
    ms = jnp.mean(x * x, axis=-1, keepdims=True)
    return x * lax.rsqrt(ms + EPS) * g


def _in_proj(xn, w5_ref, w2_ref, b_ref, col, width):
    if col < N_W5:
        w = w5_ref[:, col:col + width]
    else:
        w = w2_ref[:, col - N_W5:col - N_W5 + width]
    return jnp.dot(xn, w, preferred_element_type=F32) + b_ref[:, col:col + width]


def _in_proj_specs():
    return [_resident((1, D_MODEL)), _resident((D_MODEL, N_W5 + W_PITCH_PAD)),
            _resident((D_MODEL, N_W2 + W_PITCH_PAD)),
            _resident((1, N_MAIN)), _resident((D_MODEL, GATE_W)), _resident((1, GATE_W))]


def _in_proj_weights(P):
    return (P["norm1_g"], P["w5"], P["w2"], P["b_main"], P["w_gate"], P["b_gate"])


def _conv_init(ext_ref, hist0_ref, taps, hist_major=False):
    pad, hist = V7X_SUBLANES, taps - 1
    for g in range(ext_ref.shape[1]):
        ls = slice(g * V7X_LANES, (g + 1) * V7X_LANES)
        if hist_major:
            for k in range(hist):
                ext_ref[:, g, pad - hist + k, :] = hist0_ref[k, :, ls]
        else:
            ext_ref[:, g, pad - hist:pad, :] = hist0_ref[:, :, ls]


def _causal_conv(ext_ref, x3, cw_ref, cb_ref, hist_out_ref, col0=0, hist_major=False):
    taps = cw_ref.shape[0]
    ts = x3.shape[1]
    pad, hist = V7X_SUBLANES, taps - 1
    outs = []
    for k in range(x3.shape[2] // V7X_LANES):
        g = col0 // V7X_LANES + k
        ls = slice(g * V7X_LANES, (g + 1) * V7X_LANES)
        xg = x3[:, :, k * V7X_LANES:(k + 1) * V7X_LANES]
        ext_ref[:, g, pad:pad + ts, :] = xg
        acc = cb_ref[:, ls] + cw_ref[taps - 1:taps, ls] * xg
        for j in range(hist):
            acc = acc + cw_ref[j:j + 1, ls] * ext_ref[:, g, pad - hist + j:pad - hist + j + ts, :]
        outs.append(acc)
        new_hist = ext_ref[:, g, pad + ts - hist:pad + ts, :]
        ext_ref[:, g, pad - hist:pad, :] = new_hist
        if hist_major:
            for j in range(hist):
                hist_out_ref[j, :, ls] = new_hist[:, j, :]
        else:
            hist_out_ref[:, :, ls] = new_hist
    return jnp.concatenate(outs, axis=-1)


def _lru_body(NS, TS, u2, cw_ref, cb_ref, wr_ref, br, wi_ref, bi, lam, ext_ref, h_ref, convo_ref,
              ho_ref, hist_major=False):
    R = NS * TS
    C = D_LRU
    SB = V7X_SUBLANES
    uc2 = _causal_conv(ext_ref, u2.reshape(NS, TS, C), cw_ref, cb_ref, convo_ref,
                       hist_major=hist_major).reshape(R, C)
    ucb = uc2.astype(BF16)

    def block_diag(w_ref):
        W = V7X_MXU_DIM
        return jnp.concatenate(
            [jnp.dot(ucb[:, g * W:(g + 1) * W], w_ref[g], preferred_element_type=F32)
             for g in range(C // W)], axis=1)

    r = jax.nn.sigmoid(block_diag(wr_ref) + br)
    i = jax.nn.sigmoid(block_diag(wi_ref) + bi)
    log_a = -LRU_C * r * jax.nn.softplus(-lam)
    a = jnp.exp(log_a)
    hh = jnp.sqrt(-jnp.tanh(log_a) * (a * a + 1.0)) * (i * uc2)

    a = a.reshape(R // SB, SB, C)
    hh = hh.reshape(R // SB, SB, C)
    sub = lax.broadcasted_iota(jnp.int32, (1, SB, C), 1)
    for d in (1 << p for p in range(SB.bit_length() - 1)):
        keep = sub >= d
        a_sh = pltpu.roll(a, d, axis=1)
        h_sh = pltpu.roll(hh, d, axis=1)
        hh = hh + a * jnp.where(keep, h_sh, 0.0)
        a = a * jnp.where(keep, a_sh, 1.0)

    nb = TS // SB
    a = a.reshape(NS, nb, SB, C)
    hh = hh.reshape(NS, nb, SB, C)
    h = jnp.broadcast_to(h_ref[...], (NS, SB, C))
    blocks = []
    for j in range(nb):
        hj = hh[:, j] + a[:, j] * h
        blocks.append(hj)
        h = jnp.broadcast_to(hj[:, SB - 1:, :], (NS, SB, C))
    h_ref[...] = h[:, 0:1, :]
    ho_ref[...] = h[:, 0:1, :]
    return jnp.concatenate(blocks, axis=1).reshape(R, C)


def _seg_scan(x, tpos, TS, op, ident):
    d = 1
    while d < TS:
        sh = pltpu.roll(x, d, axis=0)
        x = op(x, jnp.where(tpos >= d, sh, ident))
        d *= 2
    return x


def _pad_rows(x, rows):
    if x.shape[0] >= rows:
        return x
    return jnp.concatenate([x, jnp.zeros((rows - x.shape[0],) + x.shape[1:], x.dtype)], axis=0)


class MlstmGates(NamedTuple):
    c4: Any
    big_m4: Any
    e4: Any
    dinv4: Any
    wk4: Any
    decay4: Any


def _mlstm_gates(NS, TS, gt, m_in, m_out):
    R = NS * TS
    LN = V7X_LANES
    ig4 = gt[:, :LN]
    lf4 = jax.nn.log_sigmoid(gt[:, LN:])
    tpos = lax.broadcasted_iota(jnp.int32, (R, LN), 0) & (TS - 1)
    b4 = _seg_scan(lf4, tpos, TS, jnp.add, 0.0)
    c4 = ig4 - b4
    cmax4 = _seg_scan(c4, tpos, TS, jnp.maximum, -jnp.inf)
    m_prev = [m_in[j] for j in range(NS)]
    m_rows = jnp.concatenate([jnp.broadcast_to(m, (TS, LN)) for m in m_prev], axis=0)
    big_m4 = jnp.maximum(cmax4, m_rows)
    e4 = jnp.exp(m_rows - big_m4)
    dinv4 = jnp.exp(-(b4 + big_m4))

    decay4, wk_parts = [], []
    for j in range(NS):
        b_last = b4[(j + 1) * TS - 1:(j + 1) * TS, :]
        g4 = b_last + c4[j * TS:(j + 1) * TS, :]
        mn = jnp.maximum(b_last + m_prev[j], jnp.max(g4, axis=0, keepdims=True))
        decay4.append(jnp.exp(b_last + m_prev[j] - mn))
        wk_parts.append(jnp.exp(g4 - mn))
        m_out[j] = mn
    return MlstmGates(c4, big_m4, e4, dinv4, jnp.concatenate(wk_parts, axis=0), decay4)


def _mlstm_heads(NS, TS, get_qkvo, gates: MlstmGates, hg_ref, st_in, st_out):
    R = NS * TS
    RC = max(R, V7X_LANES)
    shift = TS.bit_length() - 1
    c4, big_m4, e4, dinv4, wk4, decay4 = gates
    c_in, n_in, _ = st_in
    c_out, n_out, _ = st_out

    ri = lax.broadcasted_iota(jnp.int32, (R, RC), 0)
    ci = lax.broadcasted_iota(jnp.int32, (R, RC), 1)
    eye = ri == ci
    if NS == 1:
        causal = ci <= ri
    else:
        causal = (ci <= ri) & ((ri >> shift) == (ci >> shift))
    seq_of_row = lax.broadcasted_iota(jnp.int32, (R, D_HEAD), 0) >> shift

    outs = []
    ahead = get_qkvo(0)
    for h in range(N_HEADS):
        sl = slice(h * D_HEAD, (h + 1) * D_HEAD)
        qh, kh, vh, oh = ahead
        if h + 1 < N_HEADS:
            ahead = get_qkvo(h + 1)
        kh_p = _pad_rows(kh, RC)
        vh_p = _pad_rows(vh, RC)
        c_c = c4[:, h:h + 1]
        big_m_c = big_m4[:, h:h + 1]
        e_c = e4[:, h:h + 1]
        dinv_c = dinv4[:, h:h + 1]
        wk_c = wk4[:, h:h + 1]

        qk = lax.dot_general(qh, kh_p, (((1,), (1,)), ((), ())), preferred_element_type=F32)
        if NS == 1:
            q_c = jnp.dot(qh, c_in[0, h].astype(BF16), preferred_element_type=F32)
            n_rows = n_in[0, h:h + 1, :]
        else:
            q_c = jnp.zeros((R, D_HEAD), F32)
            n_rows = jnp.zeros((R, D_HEAD), F32)
            for j in range(NS):
                mine = seq_of_row == j
                q_c = jnp.where(mine, jnp.dot(qh, c_in[j, h].astype(BF16),
                                              preferred_element_type=F32), q_c)
                n_rows = jnp.where(mine, n_in[j, h:h + 1, :], n_rows)
        kw = kh.astype(F32) * wk_c
        kws = [kw if NS == 1 else jnp.where(seq_of_row == j, kw, 0.0) for j in range(NS)]
        upds = [lax.dot_general(_pad_rows(kwj, RC).astype(BF16), vh_p,
                                (((0,), (0,)), ((), ())), preferred_element_type=F32)
                for kwj in kws]

        c_r = jnp.sum(jnp.where(eye, c_c, 0.0), axis=0, keepdims=True)
        w = jnp.exp(jnp.where(causal, c_r - big_m_c, NEG_BIG))
        s = qk * w
        den = jnp.sum(s, axis=1, keepdims=True)
        num = jnp.dot(s.astype(BF16), vh_p, preferred_element_type=F32)
        q_n = jnp.sum(qh.astype(F32) * n_rows, axis=1, keepdims=True)
        num = num + e_c * q_c
        den = den + e_c * q_n
        hh = num * (1.0 / jnp.maximum(jnp.abs(den), dinv_c))
        hh = hh * lax.rsqrt(jnp.mean(hh * hh, axis=1, keepdims=True) + EPS)
        outs.append(((hh * hg_ref[:, sl]) * jax.nn.sigmoid(oh)).astype(BF16))

        for j in range(NS):
            dec = decay4[j][:, h:h + 1]
            c_out[j, h] = dec * c_in[j, h] + upds[j]
            n_out[j, h:h + 1, :] = (dec * n_in[j, h:h + 1, :]
                                    + jnp.sum(kws[j], axis=0, keepdims=True))
    return outs


def _merge_out(x, ha, hb, ga, gb, wa_ref, wb_ref, wo_ref):
    pa = jnp.dot(ha, wa_ref[:, :D_MODEL], preferred_element_type=F32)
    pb = jnp.dot(hb, wb_ref[:, :D_MODEL], preferred_element_type=F32)
    merged = jax.nn.sigmoid(ga) * pa + jax.nn.sigmoid(gb) * pb
    return x + jnp.dot(merged.astype(BF16), wo_ref[:, :D_MODEL], preferred_element_type=F32)


class MixerIn(NamedTuple):
    x: Any
    meta: Any
    g: Any
    w5: Any
    w2: Any
    b: Any
    wg: Any
    bg: Any
    cw: Any
    cb: Any
    wr: Any
    br: Any
    wi: Any
    bi: Any
    lam: Any
    hg: Any
    wa: Any
    wb: Any
    wo: Any
    w_up_f32: Any


class MixerOut(NamedTuple):
    x1: Any
    meta_x1: Any
    conv: Any
    h: Any
    c: Any
    n: Any
    m: Any
    w_up: Any


class MixerScratch(NamedTuple):
    ext: Any
    h: Any
    c: Any
    n: Any
    m: Any
    hist0: Any
    h0: Any
    c0: Any
    n0: Any
    m0: Any


class SideIn(NamedTuple):
    q: Any
    k: Any
    v: Any
    gt: Any
    o: Any
    c0: Any
    n0: Any
    m0: Any


class SideOut(NamedTuple):
    hb: Any
    c: Any
    n: Any
    m: Any


def _split_refs(refs, *kinds):
    out, pos = [], 0
    for kind in kinds:
        n = len(kind._fields)
        out.append(kind(*refs[pos:pos + n]))
        pos += n
    assert pos == len(refs)
    return out


def _mixer_tile(TS, x, i: MixerIn, o: MixerOut, s: MixerScratch):
    xn = _rms(x, i.g[...]).astype(BF16)
    proj = functools.partial(_in_proj, xn, i.w5, i.w2, i.b)

    def get_qkvo(h):
        off = h * D_HEAD
        q = (proj(COL_Q + off, D_HEAD) * (D_HEAD ** -0.5)).astype(BF16)
        return (q, proj(COL_K + off, D_HEAD).astype(BF16), proj(COL_V + off, D_HEAD).astype(BF16),
                proj(COL_O + off, D_HEAD))

    gt = jnp.dot(xn, i.wg[...], preferred_element_type=F32) + i.bg[...]
    gates = _mlstm_gates(1, TS, gt, s.m, s.m)
    hs = _lru_body(1, TS, proj(COL_U, D_LRU), i.cw, i.cb, i.wr, i.br[...],
                   i.wi, i.bi[...], i.lam[...], s.ext, s.h, o.conv, o.h)
    state = (s.c, s.n, s.m)
    hb = jnp.concatenate(_mlstm_heads(1, TS, get_qkvo, gates, i.hg, state, state), axis=1)
    return _merge_out(x, hs.astype(BF16), hb, proj(COL_GA, D_MODEL), proj(COL_GB, D_MODEL),
                      i.wa, i.wb, i.wo)


def _mixer_kernel(TS, SIDE_NS, SIDE_TS, *refs):
    i, si, o, so, s = _split_refs(refs, MixerIn, SideIn, MixerOut, SideOut, MixerScratch)
    seq, ti = pl.program_id(0), pl.program_id(1)
    hist_rows = slice(V7X_SUBLANES - (LRU_CONV - 1), V7X_SUBLANES)

    @pl.when((seq == 0) & (ti == 0))
    def _():
        s.ext[:, :, hist_rows, :] = jnp.zeros_like(s.hist0)
        for ref in (s.h, s.c, s.n, s.m):
            ref[...] = jnp.zeros_like(ref)
        o.meta_x1[...] = _mixer_tile(i.meta.shape[0], i.meta[...], i, o, s)
        s.hist0[...] = s.ext[:, :, hist_rows, :]
        for ref0, ref in ((s.h0, s.h), (s.c0, s.c), (s.n0, s.n), (s.m0, s.m)):
            ref0[...] = ref[...]

    @pl.when(ti == 0)
    def _():
        s.ext[:, :, hist_rows, :] = s.hist0[...]
        for ref0, ref in ((s.h0, s.h), (s.c0, s.c), (s.n0, s.n), (s.m0, s.m)):
            ref[...] = ref0[...]

    def get_qkvo(h):
        sl = slice(h * D_HEAD, (h + 1) * D_HEAD)
        return si.q[:, sl], si.k[:, sl], si.v[:, sl], si.o[:, sl]

    gates = _mlstm_gates(SIDE_NS, SIDE_TS, si.gt[...], si.m0, so.m)
    outs = _mlstm_heads(SIDE_NS, SIDE_TS, get_qkvo, gates, i.hg,
                        (si.c0, si.n0, si.m0), (so.c, so.n, so.m))
    for h, out in enumerate(outs):
        so.hb[:, h * D_HEAD:(h + 1) * D_HEAD] = out

    o.x1[...] = _mixer_tile(TS, i.x[...], i, o, s)
    o.w_up[...] = i.w_up_f32[0].astype(BF16)

    @pl.when(ti == pl.num_programs(1) - 1)
    def _():
        o.c[...] = s.c[...]
        o.n[...] = s.n[...]
        o.m[...] = s.m[...]


def _lru_weight_specs():
    nb = D_LRU // V7X_MXU_DIM
    return [_resident((LRU_CONV, D_LRU)), _resident((1, D_LRU)),
            _resident((nb, V7X_MXU_DIM, V7X_MXU_DIM)), _resident((1, D_LRU)),
            _resident((nb, V7X_MXU_DIM, V7X_MXU_DIM)), _resident((1, D_LRU)),
            _resident((1, D_LRU))]


def _lru_weights(P):
    return (P["lru_conv_w"], P["lru_conv_b"], P["w_r"], P["lru_b_r"], P["w_i"], P["lru_b_i"],
            P["lru_lambda"])


def _state_specs(NS):
    def spec(*tail):
        zeros = (0,) * len(tail)
        return pl.BlockSpec((NS,) + tail, lambda s, t: (s,) + zeros)

    return [spec(LRU_CONV - 1, D_LRU), spec(1, D_LRU), spec(N_HEADS, D_HEAD, D_HEAD),
            spec(N_HEADS, D_HEAD), spec(1, V7X_LANES)]


def _state_shapes(NSEQ):
    return [jax.ShapeDtypeStruct((NSEQ, LRU_CONV - 1, D_LRU), F32),
            jax.ShapeDtypeStruct((NSEQ, 1, D_LRU), F32),
            jax.ShapeDtypeStruct((NSEQ, N_HEADS, D_HEAD, D_HEAD), F32),
            jax.ShapeDtypeStruct((NSEQ, N_HEADS, D_HEAD), F32),
            jax.ShapeDtypeStruct((NSEQ, 1, V7X_LANES), F32)]


def _mixer(x2, meta, w_up, side_in: SideIn, side_len, P, NSEQ, L, TS):
    NT = L // TS
    n_meta = meta.shape[0]
    rows = pl.BlockSpec((TS, D_MODEL), lambda s, t: (s * NT + t, 0))
    wsq = _resident((D_MODEL, D_MODEL + W_PITCH_PAD))
    n_side = side_in.c0.shape[0]
    side_ns = n_side // (NSEQ * NT)
    assert side_ns * NSEQ * NT == n_side
    step = lambda s, t: s * NT + t
    rows_up = D_MODEL // (NSEQ * NT)
    assert rows_up * NSEQ * NT == D_MODEL and rows_up % (2 * V7X_SUBLANES) == 0
    srows = lambda w: pl.BlockSpec((side_ns * side_len, w), lambda s, t: (step(s, t), 0))
    sstate = [pl.BlockSpec((side_ns, N_HEADS, D_HEAD, D_HEAD), lambda s, t: (step(s, t), 0, 0, 0)),
              pl.BlockSpec((side_ns, N_HEADS, D_HEAD), lambda s, t: (step(s, t), 0, 0)),
              pl.BlockSpec((side_ns, 1, V7X_LANES), lambda s, t: (step(s, t), 0, 0))]
    state_scratch = [pltpu.VMEM((1, 1, D_LRU), F32),
                     pltpu.VMEM((1, N_HEADS, D_HEAD, D_HEAD), F32),
                     pltpu.VMEM((1, N_HEADS, D_HEAD), F32),
                     pltpu.VMEM((1, 1, V7X_LANES), F32)]
    slabs = D_LRU // V7X_LANES
    return pl.pallas_call(
        functools.partial(_mixer_kernel, TS, side_ns, side_len),
        grid=(NSEQ, NT),
        in_specs=([rows, _resident((n_meta, D_MODEL))] + _in_proj_specs() + _lru_weight_specs()
                  + [_resident((1, D_MODEL)), wsq, wsq, wsq,
                     pl.BlockSpec((1, rows_up, 2 * D_FF), lambda s, t: (0, step(s, t), 0))]
                  + [srows(D_MODEL)] * 3 + [srows(GATE_W), srows(D_MODEL)] + sstate),
        out_specs=([rows, pl.BlockSpec((n_meta, D_MODEL), lambda s, t: (0, 0))] + _state_specs(1)
                   + [pl.BlockSpec((rows_up, 2 * D_FF), lambda s, t: (step(s, t), 0))]
                   + [srows(D_MODEL)] + sstate),
        out_shape=([jax.ShapeDtypeStruct((NSEQ * L, D_MODEL), F32),
                    jax.ShapeDtypeStruct((n_meta, D_MODEL), F32)] + _state_shapes(NSEQ)
                   + [jax.ShapeDtypeStruct((D_MODEL, 2 * D_FF), BF16)]
                   + [jax.ShapeDtypeStruct((n_side * side_len, D_MODEL), BF16)]
                   + _state_shapes(n_side)[2:]),
        scratch_shapes=([pltpu.VMEM((1, slabs, V7X_SUBLANES + TS, V7X_LANES), F32)] + state_scratch
                        + [pltpu.VMEM((1, slabs, LRU_CONV - 1, V7X_LANES), F32)] + state_scratch),
        compiler_params=_params(2, VMEM_MB_RESIDENT_HALF),
        name="mixer",
    )(x2, meta, *_in_proj_weights(P), *_lru_weights(P), P["mlstm_head_g"],
      P["w_branch_a"], P["w_branch_b"], P["w_out"], w_up, *side_in)


def _front_kernel(NS, TS, x_ref, conv0_ref, h0_ref, g_ref, w5_ref, w2_ref, b_ref, wg_ref, bg_ref,
                  cw_ref, cb_ref, wr_ref, br_ref, wi_ref, bi_ref, lam_ref,
                  wdn_f32_ref, wa_f32_ref, wb_f32_ref, wo_f32_ref,
                  ha_ref, convo_ref, ho_ref, q_ref, k_ref, v_ref, o_ref, ga_ref, gb_ref, gt_ref,
                  wdn_ref, wa_ref, wb_ref, wo_ref, ext_ref, h_s):
    for src, dst in ((wdn_f32_ref, wdn_ref), (wa_f32_ref, wa_ref), (wb_f32_ref, wb_ref),
                     (wo_f32_ref, wo_ref)):
        dst[...] = _pitch_padded(src[0])
    _conv_init(ext_ref, conv0_ref, LRU_CONV, hist_major=True)
    h_s[...] = h0_ref[...]
    xn = _rms(x_ref[...], g_ref[...]).astype(BF16)
    proj = functools.partial(_in_proj, xn, w5_ref, w2_ref, b_ref)
    gt_ref[...] = jnp.dot(xn, wg_ref[...], preferred_element_type=F32) + bg_ref[...]
    hs = _lru_body(NS, TS, proj(COL_U, D_LRU), cw_ref, cb_ref, wr_ref, br_ref[...], wi_ref,
                   bi_ref[...], lam_ref[...], ext_ref, h_s, convo_ref, ho_ref, hist_major=True)
    ha_ref[...] = hs.astype(BF16)
    q_ref[...] = (proj(COL_Q, D_MODEL) * (D_HEAD ** -0.5)).astype(BF16)
    k_ref[...] = proj(COL_K, D_MODEL).astype(BF16)
    v_ref[...] = proj(COL_V, D_MODEL).astype(BF16)
    o_ref[...] = proj(COL_O, D_MODEL)
    ga_ref[...] = proj(COL_GA, D_MODEL)
    gb_ref[...] = proj(COL_GB, D_MODEL)


def _front(x2, conv0, h0, later_weights, P, NSEQ, TS, NS):
    R = NS * TS
    M = NSEQ * TS
    steps = NSEQ // NS
    slab_rows = [w.shape[1] // steps for w in later_weights]
    for w, r in zip(later_weights, slab_rows):
        assert r * steps == w.shape[1] and r % (2 * V7X_SUBLANES) == 0 and w.shape[2] == D_MODEL
    row = lambda w: pl.BlockSpec((R, w), lambda s: (s, 0))
    st = [pl.BlockSpec((LRU_CONV - 1, NS, D_LRU), lambda s: (0, s, 0)),
          pl.BlockSpec((NS, 1, D_LRU), lambda s: (s, 0, 0))]
    f32o = jax.ShapeDtypeStruct((M, D_MODEL), F32)
    bf16o = jax.ShapeDtypeStruct((M, D_MODEL), BF16)
    return pl.pallas_call(
        functools.partial(_front_kernel, NS, TS),
        grid=(steps,),
        in_specs=([row(D_MODEL)] + st + _in_proj_specs() + _lru_weight_specs()
                  + [pl.BlockSpec((1, r, D_MODEL), lambda s: (0, s, 0)) for r in slab_rows]),
        out_specs=([row(D_LRU)] + st + [row(D_MODEL)] * 6 + [row(GATE_W)]
                   + [pl.BlockSpec((r, D_MODEL + W_PITCH_PAD), lambda s: (s, 0)) for r in slab_rows]),
        out_shape=([bf16o, jax.ShapeDtypeStruct((LRU_CONV - 1, NSEQ, D_LRU), F32),
                    _state_shapes(NSEQ)[1], bf16o, bf16o, bf16o, f32o, f32o, f32o]
                   + [jax.ShapeDtypeStruct((M, GATE_W), F32)]
                   + [jax.ShapeDtypeStruct((w.shape[1], D_MODEL + W_PITCH_PAD), BF16)
                      for w in later_weights]),
        scratch_shapes=[pltpu.VMEM((NS, D_LRU // V7X_LANES, V7X_SUBLANES + TS, V7X_LANES), F32),
                        pltpu.VMEM((NS, 1, D_LRU), F32)],
        compiler_params=_params(1, VMEM_MB_RESIDENT_HALF),
        name="front",
    )(x2, conv0, h0, *_in_proj_weights(P), *_lru_weights(P), *later_weights)


def _back_kernel(NS, TS, x_ref, ha_ref, hb_ref, ga_ref, gb_ref, st0_ref, wa_ref, wb_ref, wo_ref,
                 g2_ref, wup_ref, cw_ref, cb_ref, wdn_ref, gf_ref, y_ref, sto_ref, ext_ref):
    _conv_init(ext_ref, st0_ref, FFN_CONV)
    x1 = _merge_out(x_ref[...], ha_ref[...], hb_ref[...], ga_ref[...], gb_ref[...],
                    wa_ref, wb_ref, wo_ref)
    y_ref[...] = _ffn_tile(NS, TS, x1, g2_ref, wup_ref, cw_ref, cb_ref, wdn_ref, gf_ref,
                           sto_ref, ext_ref)


def _ffn_meta_kernel(TS, x1_ref, meta_ref, g2_ref, wup_ref, cw_ref, cb_ref, wdn_ref, gf_ref,
                     y_ref, sto_ref, ext_ref, hist0_ref):
    seq, ti = pl.program_id(0), pl.program_id(1)
    hist = FFN_CONV - 1
    hist_rows = slice(V7X_SUBLANES - hist, V7X_SUBLANES)

    @pl.when((seq == 0) & (ti == 0))
    def _():
        xm = _rms(meta_ref[...], g2_ref[...]).astype(BF16)
        up = jnp.dot(xm, wup_ref[...], preferred_element_type=F32)
        for g in range(hist0_ref.shape[1]):
            hist0_ref[:, g, :, :] = up[None, up.shape[0] - hist:, g * V7X_LANES:(g + 1) * V7X_LANES]

    @pl.when(ti == 0)
    def _():
        ext_ref[:, :, hist_rows, :] = hist0_ref[...]

    y_ref[...] = _ffn_tile(1, TS, x1_ref[...], g2_ref, wup_ref, cw_ref, cb_ref, wdn_ref, gf_ref,
                           sto_ref, ext_ref)


def _ffn_tile(NS, TS, x1, g2_ref, wup_ref, cw_ref, cb_ref, wdn_ref, gf_ref, sto_ref, ext_ref):
    R = NS * TS
    xn = _rms(x1, g2_ref[...]).astype(BF16)

    def up_conv(col0):
        up = jnp.dot(xn, wup_ref[:, col0:col0 + FFN_CHUNK], preferred_element_type=F32)
        return _causal_conv(ext_ref, up.reshape(NS, TS, FFN_CHUNK), cw_ref, cb_ref, sto_ref,
                            col0).reshape(R, FFN_CHUNK)

    act = jnp.concatenate(
        [(jax.nn.gelu(up_conv(D_FF + c)) * up_conv(c)).astype(BF16)
         for c in range(0, D_FF, FFN_CHUNK)], axis=1)
    x2 = x1 + jnp.dot(act, wdn_ref[:, :D_MODEL], preferred_element_type=F32)
    return _rms(x2, gf_ref[...])


def _ffn_weight_specs():
    W = 2 * D_FF
    return [_resident((1, D_MODEL)), _resident((D_MODEL, W)), _resident((FFN_CONV, W)),
            _resident((1, W)), _resident((D_FF, D_MODEL + W_PITCH_PAD)), _resident((1, D_MODEL))]


def _ffn_weights(P):
    return (P["norm2_g"], P["w_up"], P["ffn_conv_w"], P["ffn_conv_b"], P["w_down"], P["final_g"])


def _ffn_long(x1, meta_x1, P, NSEQ, L, TS):
    NT = L // TS
    W = 2 * D_FF
    slabs = W // V7X_LANES
    rows = pl.BlockSpec((TS, D_MODEL), lambda s, t: (s * NT + t, 0))
    return pl.pallas_call(
        functools.partial(_ffn_meta_kernel, TS),
        grid=(NSEQ, NT),
        in_specs=[rows, _resident(meta_x1.shape)] + _ffn_weight_specs(),
        out_specs=[rows, pl.BlockSpec((1, FFN_CONV - 1, W), lambda s, t: (s, 0, 0))],
        out_shape=[jax.ShapeDtypeStruct((NSEQ * L, D_MODEL), F32),
                   jax.ShapeDtypeStruct((NSEQ, FFN_CONV - 1, W), F32)],
        scratch_shapes=[pltpu.VMEM((1, slabs, V7X_SUBLANES + TS, V7X_LANES), F32),
                        pltpu.VMEM((1, slabs, FFN_CONV - 1, V7X_LANES), F32)],
        compiler_params=_params(2, VMEM_MB_RESIDENT_HALF),
        name="ffn",
    )(x1, meta_x1, *_ffn_weights(P))


def _back(x2, ha, hb, ga, gb, st0, P, NSEQ, TS, NS):
    R = NS * TS
    W = 2 * D_FF
    row = pl.BlockSpec((R, D_MODEL), lambda s: (s, 0))
    stspec = pl.BlockSpec((NS, FFN_CONV - 1, W), lambda s: (s, 0, 0))
    wsq = _resident((D_MODEL, D_MODEL + W_PITCH_PAD))
    return pl.pallas_call(
        functools.partial(_back_kernel, NS, TS),
        grid=(NSEQ // NS,),
        in_specs=[row] * 5 + [stspec, wsq, wsq, wsq] + _ffn_weight_specs(),
        out_specs=[row, stspec],
        out_shape=[jax.ShapeDtypeStruct((NSEQ * TS, D_MODEL), F32),
                   jax.ShapeDtypeStruct((NSEQ, FFN_CONV - 1, W), F32)],
        scratch_shapes=[pltpu.VMEM((NS, W // V7X_LANES, V7X_SUBLANES + TS, V7X_LANES), F32)],
        compiler_params=_params(1, VMEM_MB_RESIDENT_HALF),
        name="back",
    )(x2, ha, hb, ga, gb, st0, P["w_branch_a"], P["w_branch_b"], P["w_out"], *_ffn_weights(P))


def _block_diag(w):
    bw = w.shape[1]
    per = V7X_MXU_DIM // bw
    nb = w.shape[0] // per
    w4 = w.reshape(nb, per, bw, 1, bw)
    on_diag = jnp.eye(per, dtype=w.dtype).reshape(1, per, 1, per, 1)
    return (w4 * on_diag).reshape(nb, V7X_MXU_DIM, V7X_MXU_DIM)


def _pitch_padded(w):
    pad = jnp.zeros((w.shape[0], W_PITCH_PAD), BF16)
    return jnp.concatenate([w.astype(BF16), pad], axis=1)


def _prep_weights_kernel(wt_ref, w5_ref, w2_ref, wg_ref):
    wt = wt_ref[...]
    n_gate = 2 * N_HEADS
    w5_ref[...] = _pitch_padded(wt[:N_W5].T)
    w2_ref[...] = _pitch_padded(wt[N_W5 + n_gate:].T)
    g = jnp.concatenate([wt[N_W5:N_W5 + n_gate],
                         jnp.zeros((V7X_LANES - n_gate, wt.shape[1]), F32)], axis=0).T
    head_lane = lax.broadcasted_iota(jnp.int32, g.shape, 1) < N_HEADS
    ig = jnp.where(head_lane, g, 0.0)
    fg = jnp.where(head_lane, pltpu.roll(g, V7X_LANES - N_HEADS, axis=1), 0.0)
    wg_ref[...] = jnp.concatenate([ig, fg], axis=1).astype(BF16)


def _prep_weights(w_in):
    n_in = w_in.shape[2]
    assert n_in == N_W5 + 2 * N_HEADS + N_W2
    steps = D_MODEL // V7X_LANES
    wt = jnp.transpose(w_in[0])
    slab = lambda r, w: pl.BlockSpec((r, w), lambda i: (i, 0))
    bf = lambda r, w: jax.ShapeDtypeStruct((r, w), BF16)
    return pl.pallas_call(
        _prep_weights_kernel,
        grid=(steps,),
        in_specs=[pl.BlockSpec((n_in, V7X_LANES), lambda i: (0, i))],
        out_specs=[slab(V7X_LANES, N_W5 + W_PITCH_PAD), slab(V7X_LANES, N_W2 + W_PITCH_PAD),
                   slab(V7X_LANES, GATE_W)],
        out_shape=[bf(D_MODEL, N_W5 + W_PITCH_PAD), bf(D_MODEL, N_W2 + W_PITCH_PAD),
                   bf(D_MODEL, GATE_W)],
        compiler_params=_params(1, VMEM_MB_ROW_TILED),
        name="prep_weights",
    )(wt)


def _run_long_group(x3, meta, w_up, side_in, side_len, P, TS):
    NSEQ, L, _ = x3.shape
    x2 = x3.reshape(NSEQ * L, D_MODEL)
    x1, meta_x1, conv1, h1, c1, n1, m1, P["w_up"], *side_out = _mixer(
        x2, meta, w_up, side_in, side_len, P, NSEQ, L, TS)
    y, ffn1 = _ffn_long(x1, meta_x1, P, NSEQ, L, TS)
    return y.reshape(NSEQ, L, D_MODEL), (conv1, h1, c1, n1, m1, ffn1), side_out


def _short_group_front(x2, state, later_weights, P, NSEQ, L, ns):
    conv0, h0, c0, n0, m0, _ = state
    hist_major = lambda a: jnp.transpose(a, (1, 0, 2))
    (ha, conv1, h1, q, k, v, o, ga, gb, gt, P["w_down"], P["w_branch_a"], P["w_branch_b"],
     P["w_out"]) = _front(x2, hist_major(conv0), h0, later_weights, P, NSEQ, L, ns)
    return SideIn(q, k, v, gt, o, c0, n0, m0), (ha, ga, gb, hist_major(conv1), h1)


def _short_group_back(x2, front, side_out, ffn0, P, NSEQ, L, ns):
    ha, ga, gb, conv1, h1 = front
    hb, c1, n1, m1 = side_out
    y, ffn1 = _back(x2, ha, hb, ga, gb, ffn0, P, NSEQ, L, ns)
    return y.reshape(NSEQ, L, D_MODEL), (conv1, h1, c1, n1, m1, ffn1)


def kernel(x_prompt, x_sample, state_lru_conv, state_lru_h, state_mlstm_C, state_mlstm_n,
           state_mlstm_m, state_ffn_conv, meta_tokens, norm1_g, w_in, b_in, lru_conv_w,
           lru_conv_b, lru_w_r, lru_b_r, lru_w_i, lru_b_i, lru_lambda, mlstm_head_g,
           w_branch_a, w_branch_b, w_out, norm2_g, w_up, ffn_conv_w, ffn_conv_b, w_down, final_g):
    assert w_in.shape[0] == 1, "single-layer trunk"
    b0 = b_in[0]
    gate_pad = jnp.zeros((V7X_LANES - N_HEADS,), b0.dtype)
    row = lambda a: a.reshape(1, -1).astype(F32)
    w5, w2, w_gate = _prep_weights(w_in)
    P = {
        "norm1_g": row(norm1_g[0]),
        "w5": w5,
        "w2": w2,
        "b_main": row(jnp.concatenate([b0[:N_W5], b0[N_W5 + 2 * N_HEADS:]])),
        "w_gate": w_gate,
        "b_gate": row(jnp.concatenate([b0[N_W5:N_W5 + N_HEADS], gate_pad,
                                       b0[N_W5 + N_HEADS:N_W5 + 2 * N_HEADS], gate_pad])),
        "lru_conv_w": lru_conv_w[0],
        "lru_conv_b": row(lru_conv_b[0]),
        "w_r": _block_diag(lru_w_r[0]).astype(BF16),
        "lru_b_r": row(lru_b_r[0]),
        "w_i": _block_diag(lru_w_i[0]).astype(BF16),
        "lru_b_i": row(lru_b_i[0]),
        "lru_lambda": row(lru_lambda[0]),
        "mlstm_head_g": row(mlstm_head_g[0]),
        "norm2_g": row(norm2_g[0]),
        "ffn_conv_w": ffn_conv_w[0],
        "ffn_conv_b": row(ffn_conv_b[0]),
        "final_g": row(final_g),
    }

    def pack_state(conv, h, c, n, m, ffn):
        nseq = h.shape[0]
        m_pad = jnp.pad(m.astype(F32)[:, None, :], ((0, 0), (0, 0), (0, V7X_LANES - N_HEADS)))
        return (conv.astype(F32), h.astype(F32).reshape(nseq, 1, D_LRU), c.astype(F32),
                n.astype(F32), m_pad, ffn.astype(F32))

    def unpack_state(st):
        conv, h, c, n, m, ffn = st
        return (conv[None], h.reshape(1, -1, D_LRU), c[None], n[None],
                m[:, 0, :N_HEADS][None], ffn[None])

    sample_state0 = pack_state(state_lru_conv[0], state_lru_h[0], state_mlstm_C[0],
                               state_mlstm_n[0], state_mlstm_m[0], state_ffn_conv[0])
    n_sample, l_sample, _ = x_sample.shape
    xs2 = x_sample.reshape(n_sample * l_sample, D_MODEL)
    short = dict(NSEQ=n_sample, L=l_sample, ns=SHORT_ROWS // l_sample)
    side_in, front = _short_group_front(xs2, sample_state0, (w_down, w_branch_a, w_branch_b, w_out),
                                        P, **short)
    y_prompt, prompt_state, side_out = _run_long_group(x_prompt, meta_tokens.astype(F32), w_up,
                                                       side_in, l_sample, P, LONG_TS)
    y_sample, sample_state = _short_group_back(xs2, front, side_out, sample_state0[5], P, **short)
    return (y_prompt, y_sample) + unpack_state(prompt_state) + unpack_state(sample_state)
```

```python
import functools
from typing import Any, NamedTuple

import jax
import jax.numpy as jnp
from jax import lax
from jax.experimental import pallas as pl
from jax.experimental.pallas import tpu as pltpu

F32 = jnp.float32
BF16 = jnp.bfloat16

D_MODEL = 1024
D_LRU = 1024
LRU_CONV = 4
LRU_C = 8.0
N_HEADS = 4
D_HEAD = 256
D_FF = 2816
FFN_CONV = 3
EPS = 1e-6

V7X_LANES = 128
V7X_SUBLANES = 8
V7X_MXU_DIM = 256
VMEM_MB_RESIDENT_HALF = 56
VMEM_MB_ROW_TILED = 48
NEG_BIG = -1e30

LONG_TS = 256
SHORT_ROWS = 256
FFN_CHUNK = 256

N_MAIN = 7 * D_MODEL
N_W5 = 5 * D_MODEL
N_W2 = 2 * D_MODEL
W_PITCH_PAD = V7X_LANES
GATE_W = 2 * V7X_LANES
COL_U, COL_Q, COL_K, COL_V, COL_O, COL_GA, COL_GB = (j * D_MODEL for j in range(7))


def _resident(shape):
    return pl.BlockSpec(shape, lambda *_: (0,) * len(shape), pipeline_mode=pl.Buffered(1))


def _params(n_grid, vmem_mb):
    return pltpu.CompilerParams(
        dimension_semantics=("arbitrary",) * n_grid,
        vmem_limit_bytes=vmem_mb * 1024 * 1024,
    )


def _rms(x, g):
    ms = jnp.mean(x * x, axis=-1, keepdims=True)
    return x * lax.rsqrt(ms + EPS) * g


def _in_proj(xn, w5_ref, w2_ref, b_ref, col, width):
    if col < N_W5:
        w = w5_ref[:, col:col + width]
    else:
        w = w2_ref[:, col - N_W5:col - N_W5 + width]
    return jnp.dot(xn, w, preferred_element_type=F32) + b_ref[:, col:col + width]


def _in_proj_specs():
    return [_resident((1, D_MODEL)), _resident((D_MODEL, N_W5 + W_PITCH_PAD)),
            _resident((D_MODEL, N_W2 + W_PITCH_PAD)),
            _resident((1, N_MAIN)), _resident((D_MODEL, GATE_W)), _resident((1, GATE_W))]


def _in_proj_weights(P):
    return (P["norm1_g"], P["w5"], P["w2"], P["b_main"], P["w_gate"], P["b_gate"])


def _conv_init(ext_ref, hist0_ref, taps, hist_major=False):
    pad, hist = V7X_SUBLANES, taps - 1
    for g in range(ext_ref.shape[1]):
        ls = slice(g * V7X_LANES, (g + 1) * V7X_LANES)
        if hist_major:
            for k in range(hist):
                ext_ref[:, g, pad - hist + k, :] = hist0_ref[k, :, ls]
        else:
            ext_ref[:, g, pad - hist:pad, :] = hist0_ref[:, :, ls]


def _causal_conv(ext_ref, x3, cw_ref, cb_ref, hist_out_ref, col0=0, hist_major=False):
    taps = cw_ref.shape[0]
    ts = x3.shape[1]
    pad, hist = V7X_SUBLANES, taps - 1
    outs = []
    for k in range(x3.shape[2] // V7X_LANES):
        g = col0 // V7X_LANES + k
        ls = slice(g * V7X_LANES, (g + 1) * V7X_LANES)
        xg = x3[:, :, k * V7X_LANES:(k + 1) * V7X_LANES]
        ext_ref[:, g, pad:pad + ts, :] = xg
        acc = cb_ref[:, ls] + cw_ref[taps - 1:taps, ls] * xg
        for j in range(hist):
            acc = acc + cw_ref[j:j + 1, ls] * ext_ref[:, g, pad - hist + j:pad - hist + j + ts, :]
        outs.append(acc)
        new_hist = ext_ref[:, g, pad + ts - hist:pad + ts, :]
        ext_ref[:, g, pad - hist:pad, :] = new_hist
        if hist_major:
            for j in range(hist):
                hist_out_ref[j, :, ls] = new_hist[:, j, :]
        else:
            hist_out_ref[:, :, ls] = new_hist
    return jnp.concatenate(outs, axis=-1)


def _lru_body(NS, TS, u2, cw_ref, cb_ref, wr_ref, br, wi_ref, bi, lam, ext_ref, h_ref, convo_ref,
              ho_ref, hist_major=False):
    R = NS * TS
    C = D_LRU
    SB = V7X_SUBLANES
    uc2 = _causal_conv(ext_ref, u2.reshape(NS, TS, C), cw_ref, cb_ref, convo_ref,
                       hist_major=hist_major).reshape(R, C)
    ucb = uc2.astype(BF16)

    def block_diag(w_ref):
        W = V7X_MXU_DIM
        return jnp.concatenate(
            [jnp.dot(ucb[:, g * W:(g + 1) * W], w_ref[g], preferred_element_type=F32)
             for g in range(C // W)], axis=1)

    r = jax.nn.sigmoid(block_diag(wr_ref) + br)
    i = jax.nn.sigmoid(block_diag(wi_ref) + bi)
    log_a = -LRU_C * r * jax.nn.softplus(-lam)
    a = jnp.exp(log_a)
    hh = jnp.sqrt(-jnp.tanh(log_a) * (a * a + 1.0)) * (i * uc2)

    a = a.reshape(R // SB, SB, C)
    hh = hh.reshape(R // SB, SB, C)
    sub = lax.broadcasted_iota(jnp.int32, (1, SB, C), 1)
    for d in (1 << p for p in range(SB.bit_length() - 1)):
        keep = sub >= d
        a_sh = pltpu.roll(a, d, axis=1)
        h_sh = pltpu.roll(hh, d, axis=1)
        hh = hh + a * jnp.where(keep, h_sh, 0.0)
        a = a * jnp.where(keep, a_sh, 1.0)

    nb = TS // SB
    a = a.reshape(NS, nb, SB, C)
    hh = hh.reshape(NS, nb, SB, C)
    h = jnp.broadcast_to(h_ref[...], (NS, SB, C))
    blocks = []
    for j in range(nb):
        hj = hh[:, j] + a[:, j] * h
        blocks.append(hj)
        h = jnp.broadcast_to(hj[:, SB - 1:, :], (NS, SB, C))
    h_ref[...] = h[:, 0:1, :]
    ho_ref[...] = h[:, 0:1, :]
    return jnp.concatenate(blocks, axis=1).reshape(R, C)


def _seg_scan(x, tpos, TS, op, ident):
    d = 1
    while d < TS:
        sh = pltpu.roll(x, d, axis=0)
        x = op(x, jnp.where(tpos >= d, sh, ident))
        d *= 2
    return x


def _pad_rows(x, rows):
    if x.shape[0] >= rows:
        return x
    return jnp.concatenate([x, jnp.zeros((rows - x.shape[0],) + x.shape[1:], x.dtype)], axis=0)


class MlstmGates(NamedTuple):
    c4: Any
    big_m4: Any
    e4: Any
    dinv4: Any
    wk4: Any
    decay4: Any


def _mlstm_gates(NS, TS, gt, m_in, m_out):
    R = NS * TS
    LN = V7X_LANES
    ig4 = gt[:, :LN]
    lf4 = jax.nn.log_sigmoid(gt[:, LN:])
    tpos = lax.broadcasted_iota(jnp.int32, (R, LN), 0) & (TS - 1)
    b4 = _seg_scan(lf4, tpos, TS, jnp.add, 0.0)
    c4 = ig4 - b4
    cmax4 = _seg_scan(c4, tpos, TS, jnp.maximum, -jnp.inf)
    m_prev = [m_in[j] for j in range(NS)]
    m_rows = jnp.concatenate([jnp.broadcast_to(m, (TS, LN)) for m in m_prev], axis=0)
    big_m4 = jnp.maximum(cmax4, m_rows)
    e4 = jnp.exp(m_rows - big_m4)
    dinv4 = jnp.exp(-(b4 + big_m4))

    decay4, wk_parts = [], []
    for j in range(NS):
        b_last = b4[(j + 1) * TS - 1:(j + 1) * TS, :]
        g4 = b_last + c4[j * TS:(j + 1) * TS, :]
        mn = jnp.maximum(b_last + m_prev[j], jnp.max(g4, axis=0, keepdims=True))
        decay4.append(jnp.exp(b_last + m_prev[j] - mn))
        wk_parts.append(jnp.exp(g4 - mn))
        m_out[j] = mn
    return MlstmGates(c4, big_m4, e4, dinv4, jnp.concatenate(wk_parts, axis=0), decay4)


def _mlstm_heads(NS, TS, get_qkvo, gates: MlstmGates, hg_ref, st_in, st_out):
    R = NS * TS
    RC = max(R, V7X_LANES)
    shift = TS.bit_length() - 1
    c4, big_m4, e4, dinv4, wk4, decay4 = gates
    c_in, n_in, _ = st_in
    c_out, n_out, _ = st_out

    ri = lax.broadcasted_iota(jnp.int32, (R, RC), 0)
    ci = lax.broadcasted_iota(jnp.int32, (R, RC), 1)
    eye = ri == ci
    if NS == 1:
        causal = ci <= ri
    else:
        causal = (ci <= ri) & ((ri >> shift) == (ci >> shift))
    seq_of_row = lax.broadcasted_iota(jnp.int32, (R, D_HEAD), 0) >> shift

    outs = []
    ahead = get_qkvo(0)
    for h in range(N_HEADS):
        sl = slice(h * D_HEAD, (h + 1) * D_HEAD)
        qh, kh, vh, oh = ahead
        if h + 1 < N_HEADS:
            ahead = get_qkvo(h + 1)
        kh_p = _pad_rows(kh, RC)
        vh_p = _pad_rows(vh, RC)
        c_c = c4[:, h:h + 1]
        big_m_c = big_m4[:, h:h + 1]
        e_c = e4[:, h:h + 1]
        dinv_c = dinv4[:, h:h + 1]
        wk_c = wk4[:, h:h + 1]

        qk = lax.dot_general(qh, kh_p, (((1,), (1,)), ((), ())), preferred_element_type=F32)
        if NS == 1:
            q_c = jnp.dot(qh, c_in[0, h].astype(BF16), preferred_element_type=F32)
            n_rows = n_in[0, h:h + 1, :]
        else:
            q_c = jnp.zeros((R, D_HEAD), F32)
            n_rows = jnp.zeros((R, D_HEAD), F32)
            for j in range(NS):
                mine = seq_of_row == j
                q_c = jnp.where(mine, jnp.dot(qh, c_in[j, h].astype(BF16),
                                              preferred_element_type=F32), q_c)
                n_rows = jnp.where(mine, n_in[j, h:h + 1, :], n_rows)
        kw = kh.astype(F32) * wk_c
        kws = [kw if NS == 1 else jnp.where(seq_of_row == j, kw, 0.0) for j in range(NS)]
        upds = [lax.dot_general(_pad_rows(kwj, RC).astype(BF16), vh_p,
                                (((0,), (0,)), ((), ())), preferred_element_type=F32)
                for kwj in kws]

        c_r = jnp.sum(jnp.where(eye, c_c, 0.0), axis=0, keepdims=True)
        w = jnp.exp(jnp.where(causal, c_r - big_m_c, NEG_BIG))
        s = qk * w
        den = jnp.sum(s, axis=1, keepdims=True)
        num = jnp.dot(s.astype(BF16), vh_p, preferred_element_type=F32)
        q_n = jnp.sum(qh.astype(F32) * n_rows, axis=1, keepdims=True)
        num = num + e_c * q_c
        den = den + e_c * q_n
        hh = num * (1.0 / jnp.maximum(jnp.abs(den), dinv_c))
        hh = hh * lax.rsqrt(jnp.mean(hh * hh, axis=1, keepdims=True) + EPS)
        outs.append(((hh * hg_ref[:, sl]) * jax.nn.sigmoid(oh)).astype(BF16))

        for j in range(NS):
            dec = decay4[j][:, h:h + 1]
            c_out[j, h] = dec * c_in[j, h] + upds[j]
            n_out[j, h:h + 1, :] = (dec * n_in[j, h:h + 1, :]
                                    + jnp.sum(kws[j], axis=0, keepdims=True))
    return outs


def _merge_out(x, ha, hb, ga, gb, wa_ref, wb_ref, wo_ref):
    pa = jnp.dot(ha, wa_ref[:, :D_MODEL], preferred_element_type=F32)
    pb = jnp.dot(hb, wb_ref[:, :D_MODEL], preferred_element_type=F32)
    merged = jax.nn.sigmoid(ga) * pa + jax.nn.sigmoid(gb) * pb
    return x + jnp.dot(merged.astype(BF16), wo_ref[:, :D_MODEL], preferred_element_type=F32)


class MixerIn(NamedTuple):
    x: Any
    meta: Any
    g: Any
    w5: Any
    w2: Any
    b: Any
    wg: Any
    bg: Any
    cw: Any
    cb: Any
    wr: Any
    br: Any
    wi: Any
    bi: Any
    lam: Any
    hg: Any
    wa: Any
    wb: Any
    wo: Any
    w_up_f32: Any


class MixerOut(NamedTuple):
    x1: Any
    meta_x1: Any
    conv: Any
    h: Any
    c: Any
    n: Any
    m: Any
    w_up: Any


class MixerScratch(NamedTuple):
    ext: Any
    h: Any
    c: Any
    n: Any
    m: Any
    hist0: Any
    h0: Any
    c0: Any
    n0: Any
    m0: Any


class SideIn(NamedTuple):
    q: Any
    k: Any
    v: Any
    gt: Any
    o: Any
    c0: Any
    n0: Any
    m0: Any


class SideOut(NamedTuple):
    hb: Any
    c: Any
    n: Any
    m: Any


def _split_refs(refs, *kinds):
    out, pos = [], 0
    for kind in kinds:
        n = len(kind._fields)
        out.append(kind(*refs[pos:pos + n]))
        pos += n
    assert pos == len(refs)
    return out


def _mixer_tile(TS, x, i: MixerIn, o: MixerOut, s: MixerScratch):
    xn = _rms(x, i.g[...]).astype(BF16)
    proj = functools.partial(_in_proj, xn, i.w5, i.w2, i.b)

    def get_qkvo(h):
        off = h * D_HEAD
        q = (proj(COL_Q + off, D_HEAD) * (D_HEAD ** -0.5)).astype(BF16)
        return (q, proj(COL_K + off, D_HEAD).astype(BF16), proj(COL_V + off, D_HEAD).astype(BF16),
                proj(COL_O + off, D_HEAD))

    gt = jnp.dot(xn, i.wg[...], preferred_element_type=F32) + i.bg[...]
    gates = _mlstm_gates(1, TS, gt, s.m, s.m)
    hs = _lru_body(1, TS, proj(COL_U, D_LRU), i.cw, i.cb, i.wr, i.br[...],
                   i.wi, i.bi[...], i.lam[...], s.ext, s.h, o.conv, o.h)
    state = (s.c, s.n, s.m)
    hb = jnp.concatenate(_mlstm_heads(1, TS, get_qkvo, gates, i.hg, state, state), axis=1)
    return _merge_out(x, hs.astype(BF16), hb, proj(COL_GA, D_MODEL), proj(COL_GB, D_MODEL),
                      i.wa, i.wb, i.wo)


def _mixer_kernel(TS, SIDE_NS, SIDE_TS, *refs):
    i, si, o, so, s = _split_refs(refs, MixerIn, SideIn, MixerOut, SideOut, MixerScratch)
    seq, ti = pl.program_id(0), pl.program_id(1)
    hist_rows = slice(V7X_SUBLANES - (LRU_CONV - 1), V7X_SUBLANES)

    @pl.when((seq == 0) & (ti == 0))
    def _():
        s.ext[:, :, hist_rows, :] = jnp.zeros_like(s.hist0)
        for ref in (s.h, s.c, s.n, s.m):
            ref[...] = jnp.zeros_like(ref)
        o.meta_x1[...] = _mixer_tile(i.meta.shape[0], i.meta[...], i, o, s)
        s.hist0[...] = s.ext[:, :, hist_rows, :]
        for ref0, ref in ((s.h0, s.h), (s.c0, s.c), (s.n0, s.n), (s.m0, s.m)):
            ref0[...] = ref[...]

    @pl.when(ti == 0)
    def _():
        s.ext[:, :, hist_rows, :] = s.hist0[...]
        for ref0, ref in ((s.h0, s.h), (s.c0, s.c), (s.n0, s.n), (s.m0, s.m)):
            ref[...] = ref0[...]

    def get_qkvo(h):
        sl = slice(h * D_HEAD, (h + 1) * D_HEAD)
        return si.q[:, sl], si.k[:, sl], si.v[:, sl], si.o[:, sl]

    gates = _mlstm_gates(SIDE_NS, SIDE_TS, si.gt[...], si.m0, so.m)
    outs = _mlstm_heads(SIDE_NS, SIDE_TS, get_qkvo, gates, i.hg,
                        (si.c0, si.n0, si.m0), (so.c, so.n, so.m))
    for h, out in enumerate(outs):
        so.hb[:, h * D_HEAD:(h + 1) * D_HEAD] = out

    o.x1[...] = _mixer_tile(TS, i.x[...], i, o, s)
    o.w_up[...] = i.w_up_f32[0].astype(BF16)

    @pl.when(ti == pl.num_programs(1) - 1)
    def _():
        o.c[...] = s.c[...]
        o.n[...] = s.n[...]
        o.m[...] = s.m[...]


def _lru_weight_specs():
    nb = D_LRU // V7X_MXU_DIM
    return [_resident((LRU_CONV, D_LRU)), _resident((1, D_LRU)),
            _resident((nb, V7X_MXU_DIM, V7X_MXU_DIM)), _resident((1, D_LRU)),
            _resident((nb, V7X_MXU_DIM, V7X_MXU_DIM)), _resident((1, D_LRU)),
            _resident((1, D_LRU))]


def _lru_weights(P):
    return (P["lru_conv_w"], P["lru_conv_b"], P["w_r"], P["lru_b_r"], P["w_i"], P["lru_b_i"],
            P["lru_lambda"])


def _state_specs(NS):
    def spec(*tail):
        zeros = (0,) * len(tail)
        return pl.BlockSpec((NS,) + tail, lambda s, t: (s,) + zeros)

    return [spec(LRU_CONV - 1, D_LRU), spec(1, D_LRU), spec(N_HEADS, D_HEAD, D_HEAD),
            spec(N_HEADS, D_HEAD), spec(1, V7X_LANES)]


def _state_shapes(NSEQ):
    return [jax.ShapeDtypeStruct((NSEQ, LRU_CONV - 1, D_LRU), F32),
            jax.ShapeDtypeStruct((NSEQ, 1, D_LRU), F32),
            jax.ShapeDtypeStruct((NSEQ, N_HEADS, D_HEAD, D_HEAD), F32),
            jax.ShapeDtypeStruct((NSEQ, N_HEADS, D_HEAD), F32),
            jax.ShapeDtypeStruct((NSEQ, 1, V7X_LANES), F32)]


def _mixer(x2, meta, w_up, side_in: SideIn, side_len, P, NSEQ, L, TS):
    NT = L // TS
    n_meta = meta.shape[0]
    rows = pl.BlockSpec((TS, D_MODEL), lambda s, t: (s * NT + t, 0))
    wsq = _resident((D_MODEL, D_MODEL + W_PITCH_PAD))
    n_side = side_in.c0.shape[0]
    side_ns = n_side // (NSEQ * NT)
    assert side_ns * NSEQ * NT == n_side
    step = lambda s, t: s * NT + t
    rows_up = D_MODEL // (NSEQ * NT)
    assert rows_up * NSEQ * NT == D_MODEL and rows_up % (2 * V7X_SUBLANES) == 0
    srows = lambda w: pl.BlockSpec((side_ns * side_len, w), lambda s, t: (step(s, t), 0))
    sstate = [pl.BlockSpec((side_ns, N_HEADS, D_HEAD, D_HEAD), lambda s, t: (step(s, t), 0, 0, 0)),
              pl.BlockSpec((side_ns, N_HEADS, D_HEAD), lambda s, t: (step(s, t), 0, 0)),
              pl.BlockSpec((side_ns, 1, V7X_LANES), lambda s, t: (step(s, t), 0, 0))]
    state_scratch = [pltpu.VMEM((1, 1, D_LRU), F32),
                     pltpu.VMEM((1, N_HEADS, D_HEAD, D_HEAD), F32),
                     pltpu.VMEM((1, N_HEADS, D_HEAD), F32),
                     pltpu.VMEM((1, 1, V7X_LANES), F32)]
    slabs = D_LRU // V7X_LANES
    return pl.pallas_call(
        functools.partial(_mixer_kernel, TS, side_ns, side_len),
        grid=(NSEQ, NT),
        in_specs=([rows, _resident((n_meta, D_MODEL))] + _in_proj_specs() + _lru_weight_specs()
                  + [_resident((1, D_MODEL)), wsq, wsq, wsq,
                     pl.BlockSpec((1, rows_up, 2 * D_FF), lambda s, t: (0, step(s, t), 0))]
                  + [srows(D_MODEL)] * 3 + [srows(GATE_W), srows(D_MODEL)] + sstate),
        out_specs=([rows, pl.BlockSpec((n_meta, D_MODEL), lambda s, t: (0, 0))] + _state_specs(1)
                   + [pl.BlockSpec((rows_up, 2 * D_FF), lambda s, t: (step(s, t), 0))]
                   + [srows(D_MODEL)] + sstate),
        out_shape=([jax.ShapeDtypeStruct((NSEQ * L, D_MODEL), F32),
                    jax.ShapeDtypeStruct((n_meta, D_MODEL), F32)] + _state_shapes(NSEQ)
                   + [jax.ShapeDtypeStruct((D_MODEL, 2 * D_FF), BF16)]
                   + [jax.ShapeDtypeStruct((n_side * side_len, D_MODEL), BF16)]
                   + _state_shapes(n_side)[2:]),
        scratch_shapes=([pltpu.VMEM((1, slabs, V7X_SUBLANES + TS, V7X_LANES), F32)] + state_scratch
                        + [pltpu.VMEM((1, slabs, LRU_CONV - 1, V7X_LANES), F32)] + state_scratch),
        compiler_params=_params(2, VMEM_MB_RESIDENT_HALF),
        name="mixer",
    )(x2, meta, *_in_proj_weights(P), *_lru_weights(P), P["mlstm_head_g"],
      P["w_branch_a"], P["w_branch_b"], P["w_out"], w_up, *side_in)


def _front_kernel(NS, TS, x_ref, conv0_ref, h0_ref, g_ref, w5_ref, w2_ref, b_ref, wg_ref, bg_ref,
                  cw_ref, cb_ref, wr_ref, br_ref, wi_ref, bi_ref, lam_ref, wdn_f32_ref,
                  ha_ref, convo_ref, ho_ref, q_ref, k_ref, v_ref, o_ref, ga_ref, gb_ref, gt_ref,
                  wdn_ref, ext_ref, h_s):
    wdn_ref[...] = _pitch_padded(wdn_f32_ref[0])
    _conv_init(ext_ref, conv0_ref, LRU_CONV, hist_major=True)
    h_s[...] = h0_ref[...]
    xn = _rms(x_ref[...], g_ref[...]).astype(BF16)
    proj = functools.partial(_in_proj, xn, w5_ref, w2_ref, b_ref)
    gt_ref[...] = jnp.dot(xn, wg_ref[...], preferred_element_type=F32) + bg_ref[...]
    hs = _lru_body(NS, TS, proj(COL_U, D_LRU), cw_ref, cb_ref, wr_ref, br_ref[...], wi_ref,
                   bi_ref[...], lam_ref[...], ext_ref, h_s, convo_ref, ho_ref, hist_major=True)
    ha_ref[...] = hs.astype(BF16)
    q_ref[...] = (proj(COL_Q, D_MODEL) * (D_HEAD ** -0.5)).astype(BF16)
    k_ref[...] = proj(COL_K, D_MODEL).astype(BF16)
    v_ref[...] = proj(COL_V, D_MODEL).astype(BF16)
    o_ref[...] = proj(COL_O, D_MODEL)
    ga_ref[...] = proj(COL_GA, D_MODEL)
    gb_ref[...] = proj(COL_GB, D_MODEL)


def _front(x2, conv0, h0, w_down, P, NSEQ, TS, NS):
    R = NS * TS
    M = NSEQ * TS
    steps = NSEQ // NS
    rows_dn = D_FF // steps
    assert rows_dn * steps == D_FF and rows_dn % (2 * V7X_SUBLANES) == 0
    row = lambda w: pl.BlockSpec((R, w), lambda s: (s, 0))
    st = [pl.BlockSpec((LRU_CONV - 1, NS, D_LRU), lambda s: (0, s, 0)),
          pl.BlockSpec((NS, 1, D_LRU), lambda s: (s, 0, 0))]
    f32o = jax.ShapeDtypeStruct((M, D_MODEL), F32)
    bf16o = jax.ShapeDtypeStruct((M, D_MODEL), BF16)
    return pl.pallas_call(
        functools.partial(_front_kernel, NS, TS),
        grid=(steps,),
        in_specs=([row(D_MODEL)] + st + _in_proj_specs() + _lru_weight_specs()
                  + [pl.BlockSpec((1, rows_dn, D_MODEL), lambda s: (0, s, 0))]),
        out_specs=([row(D_LRU)] + st + [row(D_MODEL)] * 6 + [row(GATE_W)]
                   + [pl.BlockSpec((rows_dn, D_MODEL + W_PITCH_PAD), lambda s: (s, 0))]),
        out_shape=([bf16o, jax.ShapeDtypeStruct((LRU_CONV - 1, NSEQ, D_LRU), F32),
                    _state_shapes(NSEQ)[1], bf16o, bf16o, bf16o, f32o, f32o, f32o]
                   + [jax.ShapeDtypeStruct((M, GATE_W), F32),
                      jax.ShapeDtypeStruct((D_FF, D_MODEL + W_PITCH_PAD), BF16)]),
        scratch_shapes=[pltpu.VMEM((NS, D_LRU // V7X_LANES, V7X_SUBLANES + TS, V7X_LANES), F32),
                        pltpu.VMEM((NS, 1, D_LRU), F32)],
        compiler_params=_params(1, VMEM_MB_ROW_TILED),
        name="front",
    )(x2, conv0, h0, *_in_proj_weights(P), *_lru_weights(P), w_down)


def _back_kernel(NS, TS, x_ref, ha_ref, hb_ref, ga_ref, gb_ref, st0_ref, wa_ref, wb_ref, wo_ref,
                 g2_ref, wup_ref, cw_ref, cb_ref, wdn_ref, gf_ref, y_ref, sto_ref, ext_ref):
    _conv_init(ext_ref, st0_ref, FFN_CONV)
    x1 = _merge_out(x_ref[...], ha_ref[...], hb_ref[...], ga_ref[...], gb_ref[...],
                    wa_ref, wb_ref, wo_ref)
    y_ref[...] = _ffn_tile(NS, TS, x1, g2_ref, wup_ref, cw_ref, cb_ref, wdn_ref, gf_ref,
                           sto_ref, ext_ref)


def _ffn_meta_kernel(TS, x1_ref, meta_ref, g2_ref, wup_ref, cw_ref, cb_ref, wdn_ref, gf_ref,
                     y_ref, sto_ref, ext_ref, hist0_ref):
    seq, ti = pl.program_id(0), pl.program_id(1)
    hist = FFN_CONV - 1
    hist_rows = slice(V7X_SUBLANES - hist, V7X_SUBLANES)

    @pl.when((seq == 0) & (ti == 0))
    def _():
        xm = _rms(meta_ref[...], g2_ref[...]).astype(BF16)
        up = jnp.dot(xm, wup_ref[...], preferred_element_type=F32)
        for g in range(hist0_ref.shape[1]):
            hist0_ref[:, g, :, :] = up[None, up.shape[0] - hist:, g * V7X_LANES:(g + 1) * V7X_LANES]

    @pl.when(ti == 0)
    def _():
        ext_ref[:, :, hist_rows, :] = hist0_ref[...]

    y_ref[...] = _ffn_tile(1, TS, x1_ref[...], g2_ref, wup_ref, cw_ref, cb_ref, wdn_ref, gf_ref,
                           sto_ref, ext_ref)


def _ffn_tile(NS, TS, x1, g2_ref, wup_ref, cw_ref, cb_ref, wdn_ref, gf_ref, sto_ref, ext_ref):
    R = NS * TS
    xn = _rms(x1, g2_ref[...]).astype(BF16)

    def up_conv(col0):
        up = jnp.dot(xn, wup_ref[:, col0:col0 + FFN_CHUNK], preferred_element_type=F32)
        return _causal_conv(ext_ref, up.reshape(NS, TS, FFN_CHUNK), cw_ref, cb_ref, sto_ref,
                            col0).reshape(R, FFN_CHUNK)

    act = jnp.concatenate(
        [(jax.nn.gelu(up_conv(D_FF + c)) * up_conv(c)).astype(BF16)
         for c in range(0, D_FF, FFN_CHUNK)], axis=1)
    x2 = x1 + jnp.dot(act, wdn_ref[:, :D_MODEL], preferred_element_type=F32)
    return _rms(x2, gf_ref[...])


def _ffn_weight_specs():
    W = 2 * D_FF
    return [_resident((1, D_MODEL)), _resident((D_MODEL, W)), _resident((FFN_CONV, W)),
            _resident((1, W)), _resident((D_FF, D_MODEL + W_PITCH_PAD)), _resident((1, D_MODEL))]


def _ffn_weights(P):
    return (P["norm2_g"], P["w_up"], P["ffn_conv_w"], P["ffn_conv_b"], P["w_down"], P["final_g"])


def _ffn_long(x1, meta_x1, P, NSEQ, L, TS):
    NT = L // TS
    W = 2 * D_FF
    slabs = W // V7X_LANES
    rows = pl.BlockSpec((TS, D_MODEL), lambda s, t: (s * NT + t, 0))
    return pl.pallas_call(
        functools.partial(_ffn_meta_kernel, TS),
        grid=(NSEQ, NT),
        in_specs=[rows, _resident(meta_x1.shape)] + _ffn_weight_specs(),
        out_specs=[rows, pl.BlockSpec((1, FFN_CONV - 1, W), lambda s, t: (s, 0, 0))],
        out_shape=[jax.ShapeDtypeStruct((NSEQ * L, D_MODEL), F32),
                   jax.ShapeDtypeStruct((NSEQ, FFN_CONV - 1, W), F32)],
        scratch_shapes=[pltpu.VMEM((1, slabs, V7X_SUBLANES + TS, V7X_LANES), F32),
                        pltpu.VMEM((1, slabs, FFN_CONV - 1, V7X_LANES), F32)],
        compiler_params=_params(2, VMEM_MB_RESIDENT_HALF),
        name="ffn",
    )(x1, meta_x1, *_ffn_weights(P))


def _back(x2, ha, hb, ga, gb, st0, P, NSEQ, TS, NS):
    R = NS * TS
    W = 2 * D_FF
    row = pl.BlockSpec((R, D_MODEL), lambda s: (s, 0))
    stspec = pl.BlockSpec((NS, FFN_CONV - 1, W), lambda s: (s, 0, 0))
    wsq = _resident((D_MODEL, D_MODEL + W_PITCH_PAD))
    return pl.pallas_call(
        functools.partial(_back_kernel, NS, TS),
        grid=(NSEQ // NS,),
        in_specs=[row] * 5 + [stspec, wsq, wsq, wsq] + _ffn_weight_specs(),
        out_specs=[row, stspec],
        out_shape=[jax.ShapeDtypeStruct((NSEQ * TS, D_MODEL), F32),
                   jax.ShapeDtypeStruct((NSEQ, FFN_CONV - 1, W), F32)],
        scratch_shapes=[pltpu.VMEM((NS, W // V7X_LANES, V7X_SUBLANES + TS, V7X_LANES), F32)],
        compiler_params=_params(1, VMEM_MB_RESIDENT_HALF),
        name="back",
    )(x2, ha, hb, ga, gb, st0, P["w_branch_a"], P["w_branch_b"], P["w_out"], *_ffn_weights(P))


def _block_diag(w):
    bw = w.shape[1]
    per = V7X_MXU_DIM // bw
    nb = w.shape[0] // per
    w4 = w.reshape(nb, per, bw, 1, bw)
    on_diag = jnp.eye(per, dtype=w.dtype).reshape(1, per, 1, per, 1)
    return (w4 * on_diag).reshape(nb, V7X_MXU_DIM, V7X_MXU_DIM)


def _pitch_padded(w):
    pad = jnp.zeros((w.shape[0], W_PITCH_PAD), BF16)
    return jnp.concatenate([w.astype(BF16), pad], axis=1)


def _prep_weights_kernel(wt_ref, wa_ref, wb_ref, wo_ref, w5_ref, w2_ref, wg_ref, wa_o, wb_o, wo_o):
    wt = wt_ref[...]
    n_gate = 2 * N_HEADS
    w5_ref[...] = _pitch_padded(wt[:N_W5].T)
    w2_ref[...] = _pitch_padded(wt[N_W5 + n_gate:].T)
    g = jnp.concatenate([wt[N_W5:N_W5 + n_gate],
                         jnp.zeros((V7X_LANES - n_gate, wt.shape[1]), F32)], axis=0).T
    head_lane = lax.broadcasted_iota(jnp.int32, g.shape, 1) < N_HEADS
    ig = jnp.where(head_lane, g, 0.0)
    fg = jnp.where(head_lane, pltpu.roll(g, V7X_LANES - N_HEADS, axis=1), 0.0)
    wg_ref[...] = jnp.concatenate([ig, fg], axis=1).astype(BF16)

    wa_o[...] = _pitch_padded(wa_ref[0])
    wb_o[...] = _pitch_padded(wb_ref[0])
    wo_o[...] = _pitch_padded(wo_ref[0])


def _prep_weights(w_in, w_branch_a, w_branch_b, w_out):
    n_in = w_in.shape[2]
    assert n_in == N_W5 + 2 * N_HEADS + N_W2
    steps = D_MODEL // V7X_LANES
    wt = jnp.transpose(w_in[0])
    slab3 = lambda r, w: pl.BlockSpec((1, r, w), lambda i: (0, i, 0))
    slab = lambda r, w: pl.BlockSpec((r, w), lambda i: (i, 0))
    sq_pad = D_MODEL + W_PITCH_PAD
    bf = lambda r, w: jax.ShapeDtypeStruct((r, w), BF16)
    return pl.pallas_call(
        _prep_weights_kernel,
        grid=(steps,),
        in_specs=[pl.BlockSpec((n_in, V7X_LANES), lambda i: (0, i)),
                  slab3(V7X_LANES, D_MODEL), slab3(V7X_LANES, D_MODEL), slab3(V7X_LANES, D_MODEL)],
        out_specs=[slab(V7X_LANES, N_W5 + W_PITCH_PAD), slab(V7X_LANES, N_W2 + W_PITCH_PAD),
                   slab(V7X_LANES, GATE_W),
                   slab(V7X_LANES, sq_pad), slab(V7X_LANES, sq_pad), slab(V7X_LANES, sq_pad)],
        out_shape=[bf(D_MODEL, N_W5 + W_PITCH_PAD), bf(D_MODEL, N_W2 + W_PITCH_PAD),
                   bf(D_MODEL, GATE_W),
                   bf(D_MODEL, sq_pad), bf(D_MODEL, sq_pad), bf(D_MODEL, sq_pad)],
        compiler_params=_params(1, VMEM_MB_ROW_TILED),
        name="prep_weights",
    )(wt, w_branch_a, w_branch_b, w_out)


def _run_long_group(x3, meta, w_up, side_in, side_len, P, TS):
    NSEQ, L, _ = x3.shape
    x2 = x3.reshape(NSEQ * L, D_MODEL)
    x1, meta_x1, conv1, h1, c1, n1, m1, P["w_up"], *side_out = _mixer(
        x2, meta, w_up, side_in, side_len, P, NSEQ, L, TS)
    y, ffn1 = _ffn_long(x1, meta_x1, P, NSEQ, L, TS)
    return y.reshape(NSEQ, L, D_MODEL), (conv1, h1, c1, n1, m1, ffn1), side_out


def _short_group_front(x2, state, w_down, P, NSEQ, L, ns):
    conv0, h0, c0, n0, m0, _ = state
    hist_major = lambda a: jnp.transpose(a, (1, 0, 2))
    ha, conv1, h1, q, k, v, o, ga, gb, gt, P["w_down"] = _front(x2, hist_major(conv0), h0, w_down, P,
                                                                 NSEQ, L, ns)
    return SideIn(q, k, v, gt, o, c0, n0, m0), (ha, ga, gb, hist_major(conv1), h1)


def _short_group_back(x2, front, side_out, ffn0, P, NSEQ, L, ns):
    ha, ga, gb, conv1, h1 = front
    hb, c1, n1, m1 = side_out
    y, ffn1 = _back(x2, ha, hb, ga, gb, ffn0, P, NSEQ, L, ns)
    return y.reshape(NSEQ, L, D_MODEL), (conv1, h1, c1, n1, m1, ffn1)


def kernel(x_prompt, x_sample, state_lru_conv, state_lru_h, state_mlstm_C, state_mlstm_n,
           state_mlstm_m, state_ffn_conv, meta_tokens, norm1_g, w_in, b_in, lru_conv_w,
           lru_conv_b, lru_w_r, lru_b_r, lru_w_i, lru_b_i, lru_lambda, mlstm_head_g,
           w_branch_a, w_branch_b, w_out, norm2_g, w_up, ffn_conv_w, ffn_conv_b, w_down, final_g):
    assert w_in.shape[0] == 1, "single-layer trunk"
    b0 = b_in[0]
    gate_pad = jnp.zeros((V7X_LANES - N_HEADS,), b0.dtype)
    row = lambda a: a.reshape(1, -1).astype(F32)
    w5, w2, w_gate, w_a_b, w_b_b, w_o_b = _prep_weights(w_in, w_branch_a, w_branch_b, w_out)
    P = {
        "norm1_g": row(norm1_g[0]),
        "w5": w5,
        "w2": w2,
        "b_main": row(jnp.concatenate([b0[:N_W5], b0[N_W5 + 2 * N_HEADS:]])),
        "w_gate": w_gate,
        "b_gate": row(jnp.concatenate([b0[N_W5:N_W5 + N_HEADS], gate_pad,
                                       b0[N_W5 + N_HEADS:N_W5 + 2 * N_HEADS], gate_pad])),
        "lru_conv_w": lru_conv_w[0],
        "lru_conv_b": row(lru_conv_b[0]),
        "w_r": _block_diag(lru_w_r[0]).astype(BF16),
        "lru_b_r": row(lru_b_r[0]),
        "w_i": _block_diag(lru_w_i[0]).astype(BF16),
        "lru_b_i": row(lru_b_i[0]),
        "lru_lambda": row(lru_lambda[0]),
        "mlstm_head_g": row(mlstm_head_g[0]),
        "w_branch_a": w_a_b,
        "w_branch_b": w_b_b,
        "w_out": w_o_b,
        "norm2_g": row(norm2_g[0]),
        "ffn_conv_w": ffn_conv_w[0],
        "ffn_conv_b": row(ffn_conv_b[0]),
        "final_g": row(final_g),
    }

    def pack_state(conv, h, c, n, m, ffn):
        nseq = h.shape[0]
        m_pad = jnp.pad(m.astype(F32)[:, None, :], ((0, 0), (0, 0), (0, V7X_LANES - N_HEADS)))
        return (conv.astype(F32), h.astype(F32).reshape(nseq, 1, D_LRU), c.astype(F32),
                n.astype(F32), m_pad, ffn.astype(F32))

    def unpack_state(st):
        conv, h, c, n, m, ffn = st
        return (conv[None], h.reshape(1, -1, D_LRU), c[None], n[None],
                m[:, 0, :N_HEADS][None], ffn[None])

    sample_state0 = pack_state(state_lru_conv[0], state_lru_h[0], state_mlstm_C[0],
                               state_mlstm_n[0], state_mlstm_m[0], state_ffn_conv[0])
    n_sample, l_sample, _ = x_sample.shape
    xs2 = x_sample.reshape(n_sample * l_sample, D_MODEL)
    short = dict(NSEQ=n_sample, L=l_sample, ns=SHORT_ROWS // l_sample)
    side_in, front = _short_group_front(xs2, sample_state0, w_down, P, **short)
    y_prompt, prompt_state, side_out = _run_long_group(x_prompt, meta_tokens.astype(F32), w_up,
                                                       side_in, l_sample, P, LONG_TS)
    y_sample, sample_state = _short_group_back(xs2, front, side_out, sample_state0[5], P, **short)
    return (y_prompt, y_sample) + unpack_state(prompt_state) + unpack_state(sample_state)
```

```python
import functools
from typing import Any, NamedTuple

import jax
import jax.numpy as jnp
from jax import lax
from jax.experimental import pallas as pl
from jax.experimental.pallas import tpu as pltpu

F32 = jnp.float32
BF16 = jnp.bfloat16

D_MODEL = 1024
D_LRU = 1024
LRU_CONV = 4
LRU_C = 8.0
N_HEADS = 4
D_HEAD = 256
D_FF = 2816
FFN_CONV = 3
EPS = 1e-6

V7X_LANES = 128
V7X_SUBLANES = 8
V7X_MXU_DIM = 256
VMEM_MB_RESIDENT_HALF = 56
VMEM_MB_ROW_TILED = 48
VMEM_MB_WEIGHT_HANDOFF = 62
NEG_BIG = -1e30

LONG_TS = 256
SHORT_ROWS = 256
FFN_CHUNK = 256
PREP_STEPS = D_MODEL // V7X_LANES

N_MAIN = 7 * D_MODEL
N_W5 = 5 * D_MODEL
N_W2 = 2 * D_MODEL
W_PITCH_PAD = V7X_LANES
GATE_W = 2 * V7X_LANES
COL_U, COL_Q, COL_K, COL_V, COL_O, COL_GA, COL_GB = (j * D_MODEL for j in range(7))


def _resident(shape):
    return pl.BlockSpec(shape, lambda *_: (0,) * len(shape), pipeline_mode=pl.Buffered(1))


def _params(n_grid, vmem_mb):
    return pltpu.CompilerParams(
        dimension_semantics=("arbitrary",) * n_grid,
        vmem_limit_bytes=vmem_mb * 1024 * 1024,
    )


def _rms(x, g):
    ms = jnp.mean(x * x, axis=-1, keepdims=True)
    return x * lax.rsqrt(ms + EPS) * g


def _in_proj(xn, w5_ref, w2_ref, b_ref, col, width):
    if col < N_W5:
        w = w5_ref[:, col:col + width]
    else:
        w = w2_ref[:, col - N_W5:col - N_W5 + width]
    return jnp.dot(xn, w, preferred_element_type=F32) + b_ref[:, col:col + width]


def _in_proj_specs():
    return [_resident((1, D_MODEL)), _resident((D_MODEL, N_W5 + W_PITCH_PAD)),
            _resident((D_MODEL, N_W2 + W_PITCH_PAD)),
            _resident((1, N_MAIN)), _resident((D_MODEL, GATE_W)), _resident((1, GATE_W))]


def _in_proj_weights(P):
    return (P["norm1_g"], P["w5"], P["w2"], P["b_main"], P["w_gate"], P["b_gate"])


def _conv_init(ext_ref, hist0_ref, taps, hist_major=False):
    pad, hist = V7X_SUBLANES, taps - 1
    for g in range(ext_ref.shape[1]):
        ls = slice(g * V7X_LANES, (g + 1) * V7X_LANES)
        if hist_major:
            for k in range(hist):
                ext_ref[:, g, pad - hist + k, :] = hist0_ref[k, :, ls]
        else:
            ext_ref[:, g, pad - hist:pad, :] = hist0_ref[:, :, ls]


def _causal_conv(ext_ref, x3, cw_ref, cb_ref, hist_out_ref, col0=0, hist_major=False):
    taps = cw_ref.shape[0]
    ts = x3.shape[1]
    pad, hist = V7X_SUBLANES, taps - 1
    outs = []
    for k in range(x3.shape[2] // V7X_LANES):
        g = col0 // V7X_LANES + k
        ls = slice(g * V7X_LANES, (g + 1) * V7X_LANES)
        xg = x3[:, :, k * V7X_LANES:(k + 1) * V7X_LANES]
        ext_ref[:, g, pad:pad + ts, :] = xg
        acc = cb_ref[:, ls] + cw_ref[taps - 1:taps, ls] * xg
        for j in range(hist):
            acc = acc + cw_ref[j:j + 1, ls] * ext_ref[:, g, pad - hist + j:pad - hist + j + ts, :]
        outs.append(acc)
        new_hist = ext_ref[:, g, pad + ts - hist:pad + ts, :]
        ext_ref[:, g, pad - hist:pad, :] = new_hist
        if hist_major:
            for j in range(hist):
                hist_out_ref[j, :, ls] = new_hist[:, j, :]
        else:
            hist_out_ref[:, :, ls] = new_hist
    return jnp.concatenate(outs, axis=-1)


def _lru_body(NS, TS, u2, cw_ref, cb_ref, wr_ref, br, wi_ref, bi, lam, ext_ref, h_ref, convo_ref,
              ho_ref, hist_major=False):
    R = NS * TS
    C = D_LRU
    SB = V7X_SUBLANES
    uc2 = _causal_conv(ext_ref, u2.reshape(NS, TS, C), cw_ref, cb_ref, convo_ref,
                       hist_major=hist_major).reshape(R, C)
    ucb = uc2.astype(BF16)

    def block_diag(w_ref):
        W = V7X_MXU_DIM
        return jnp.concatenate(
            [jnp.dot(ucb[:, g * W:(g + 1) * W], w_ref[g], preferred_element_type=F32)
             for g in range(C // W)], axis=1)

    r = jax.nn.sigmoid(block_diag(wr_ref) + br)
    i = jax.nn.sigmoid(block_diag(wi_ref) + bi)
    log_a = -LRU_C * r * jax.nn.softplus(-lam)
    a = jnp.exp(log_a)
    hh = jnp.sqrt(-jnp.tanh(log_a) * (a * a + 1.0)) * (i * uc2)

    a = a.reshape(R // SB, SB, C)
    hh = hh.reshape(R // SB, SB, C)
    sub = lax.broadcasted_iota(jnp.int32, (1, SB, C), 1)
    for d in (1 << p for p in range(SB.bit_length() - 1)):
        keep = sub >= d
        a_sh = pltpu.roll(a, d, axis=1)
        h_sh = pltpu.roll(hh, d, axis=1)
        hh = hh + a * jnp.where(keep, h_sh, 0.0)
        a = a * jnp.where(keep, a_sh, 1.0)

    nb = TS // SB
    a = a.reshape(NS, nb, SB, C)
    hh = hh.reshape(NS, nb, SB, C)
    h = jnp.broadcast_to(h_ref[...], (NS, SB, C))
    blocks = []
    for j in range(nb):
        hj = hh[:, j] + a[:, j] * h
        blocks.append(hj)
        h = jnp.broadcast_to(hj[:, SB - 1:, :], (NS, SB, C))
    h_ref[...] = h[:, 0:1, :]
    ho_ref[...] = h[:, 0:1, :]
    return jnp.concatenate(blocks, axis=1).reshape(R, C)


def _seg_scan(x, tpos, TS, op, ident):
    d = 1
    while d < TS:
        sh = pltpu.roll(x, d, axis=0)
        x = op(x, jnp.where(tpos >= d, sh, ident))
        d *= 2
    return x


def _pad_rows(x, rows):
    if x.shape[0] >= rows:
        return x
    return jnp.concatenate([x, jnp.zeros((rows - x.shape[0],) + x.shape[1:], x.dtype)], axis=0)


class MlstmGates(NamedTuple):
    c4: Any
    big_m4: Any
    e4: Any
    dinv4: Any
    wk4: Any
    decay4: Any


def _mlstm_gates(NS, TS, gt, m_in, m_out):
    R = NS * TS
    LN = V7X_LANES
    ig4 = gt[:, :LN]
    lf4 = jax.nn.log_sigmoid(gt[:, LN:])
    tpos = lax.broadcasted_iota(jnp.int32, (R, LN), 0) & (TS - 1)
    b4 = _seg_scan(lf4, tpos, TS, jnp.add, 0.0)
    c4 = ig4 - b4
    cmax4 = _seg_scan(c4, tpos, TS, jnp.maximum, -jnp.inf)
    m_prev = [m_in[j] for j in range(NS)]
    m_rows = jnp.concatenate([jnp.broadcast_to(m, (TS, LN)) for m in m_prev], axis=0)
    big_m4 = jnp.maximum(cmax4, m_rows)
    e4 = jnp.exp(m_rows - big_m4)
    dinv4 = jnp.exp(-(b4 + big_m4))

    decay4, wk_parts = [], []
    for j in range(NS):
        b_last = b4[(j + 1) * TS - 1:(j + 1) * TS, :]
        g4 = b_last + c4[j * TS:(j + 1) * TS, :]
        mn = jnp.maximum(b_last + m_prev[j], jnp.max(g4, axis=0, keepdims=True))
        decay4.append(jnp.exp(b_last + m_prev[j] - mn))
        wk_parts.append(jnp.exp(g4 - mn))
        m_out[j] = mn
    return MlstmGates(c4, big_m4, e4, dinv4, jnp.concatenate(wk_parts, axis=0), decay4)


def _mlstm_heads(NS, TS, get_qkvo, gates: MlstmGates, hg_ref, st_in, st_out):
    R = NS * TS
    RC = max(R, V7X_LANES)
    shift = TS.bit_length() - 1
    c4, big_m4, e4, dinv4, wk4, decay4 = gates
    c_in, n_in, _ = st_in
    c_out, n_out, _ = st_out

    ri = lax.broadcasted_iota(jnp.int32, (R, RC), 0)
    ci = lax.broadcasted_iota(jnp.int32, (R, RC), 1)
    eye = ri == ci
    if NS == 1:
        causal = ci <= ri
    else:
        causal = (ci <= ri) & ((ri >> shift) == (ci >> shift))
    seq_of_row = lax.broadcasted_iota(jnp.int32, (R, D_HEAD), 0) >> shift

    outs = []
    ahead = get_qkvo(0)
    for h in range(N_HEADS):
        sl = slice(h * D_HEAD, (h + 1) * D_HEAD)
        qh, kh, vh, oh = ahead
        if h + 1 < N_HEADS:
            ahead = get_qkvo(h + 1)
        kh_p = _pad_rows(kh, RC)
        vh_p = _pad_rows(vh, RC)
        c_c = c4[:, h:h + 1]
        big_m_c = big_m4[:, h:h + 1]
        e_c = e4[:, h:h + 1]
        dinv_c = dinv4[:, h:h + 1]
        wk_c = wk4[:, h:h + 1]

        qk = lax.dot_general(qh, kh_p, (((1,), (1,)), ((), ())), preferred_element_type=F32)
        if NS == 1:
            q_c = jnp.dot(qh, c_in[0, h].astype(BF16), preferred_element_type=F32)
            n_rows = n_in[0, h:h + 1, :]
        else:
            q_c = jnp.zeros((R, D_HEAD), F32)
            n_rows = jnp.zeros((R, D_HEAD), F32)
            for j in range(NS):
                mine = seq_of_row == j
                q_c = jnp.where(mine, jnp.dot(qh, c_in[j, h].astype(BF16),
                                              preferred_element_type=F32), q_c)
                n_rows = jnp.where(mine, n_in[j, h:h + 1, :], n_rows)
        kw = kh.astype(F32) * wk_c
        kws = [kw if NS == 1 else jnp.where(seq_of_row == j, kw, 0.0) for j in range(NS)]
        upds = [lax.dot_general(_pad_rows(kwj, RC).astype(BF16), vh_p,
                                (((0,), (0,)), ((), ())), preferred_element_type=F32)
                for kwj in kws]

        c_r = jnp.sum(jnp.where(eye, c_c, 0.0), axis=0, keepdims=True)
        w = jnp.exp(jnp.where(causal, c_r - big_m_c, NEG_BIG))
        s = qk * w
        den = jnp.sum(s, axis=1, keepdims=True)
        num = jnp.dot(s.astype(BF16), vh_p, preferred_element_type=F32)
        q_n = jnp.sum(qh.astype(F32) * n_rows, axis=1, keepdims=True)
        num = num + e_c * q_c
        den = den + e_c * q_n
        hh = num * (1.0 / jnp.maximum(jnp.abs(den), dinv_c))
        hh = hh * lax.rsqrt(jnp.mean(hh * hh, axis=1, keepdims=True) + EPS)
        outs.append(((hh * hg_ref[:, sl]) * jax.nn.sigmoid(oh)).astype(BF16))

        for j in range(NS):
            dec = decay4[j][:, h:h + 1]
            c_out[j, h] = dec * c_in[j, h] + upds[j]
            n_out[j, h:h + 1, :] = (dec * n_in[j, h:h + 1, :]
                                    + jnp.sum(kws[j], axis=0, keepdims=True))
    return outs


def _merge_out(x, ha, hb, ga, gb, wa_ref, wb_ref, wo_ref):
    pa = jnp.dot(ha, wa_ref[:, :D_MODEL], preferred_element_type=F32)
    pb = jnp.dot(hb, wb_ref[:, :D_MODEL], preferred_element_type=F32)
    merged = jax.nn.sigmoid(ga) * pa + jax.nn.sigmoid(gb) * pb
    return x + jnp.dot(merged.astype(BF16), wo_ref[:, :D_MODEL], preferred_element_type=F32)


class MixerIn(NamedTuple):
    x: Any
    meta: Any
    g: Any
    w5: Any
    w2: Any
    b: Any
    wg: Any
    bg: Any
    cw: Any
    cb: Any
    wr: Any
    br: Any
    wi: Any
    bi: Any
    lam: Any
    hg: Any
    wa: Any
    wb: Any
    wo: Any
    w_up_f32: Any


class MixerOut(NamedTuple):
    x1: Any
    meta_x1: Any
    conv: Any
    h: Any
    c: Any
    n: Any
    m: Any
    w_up: Any


class MixerScratch(NamedTuple):
    ext: Any
    h: Any
    c: Any
    n: Any
    m: Any
    hist0: Any
    h0: Any
    c0: Any
    n0: Any
    m0: Any


class SideIn(NamedTuple):
    q: Any
    k: Any
    v: Any
    gt: Any
    o: Any
    c0: Any
    n0: Any
    m0: Any


class SideOut(NamedTuple):
    hb: Any
    c: Any
    n: Any
    m: Any


def _split_refs(refs, *kinds):
    out, pos = [], 0
    for kind in kinds:
        n = len(kind._fields)
        out.append(kind(*refs[pos:pos + n]))
        pos += n
    assert pos == len(refs)
    return out


def _mixer_tile(TS, x, i: MixerIn, o: MixerOut, s: MixerScratch):
    xn = _rms(x, i.g[...]).astype(BF16)
    proj = functools.partial(_in_proj, xn, i.w5, i.w2, i.b)

    def get_qkvo(h):
        off = h * D_HEAD
        q = (proj(COL_Q + off, D_HEAD) * (D_HEAD ** -0.5)).astype(BF16)
        return (q, proj(COL_K + off, D_HEAD).astype(BF16), proj(COL_V + off, D_HEAD).astype(BF16),
                proj(COL_O + off, D_HEAD))

    gt = jnp.dot(xn, i.wg[...], preferred_element_type=F32) + i.bg[...]
    gates = _mlstm_gates(1, TS, gt, s.m, s.m)
    hs = _lru_body(1, TS, proj(COL_U, D_LRU), i.cw, i.cb, i.wr, i.br[...],
                   i.wi, i.bi[...], i.lam[...], s.ext, s.h, o.conv, o.h)
    state = (s.c, s.n, s.m)
    hb = jnp.concatenate(_mlstm_heads(1, TS, get_qkvo, gates, i.hg, state, state), axis=1)
    return _merge_out(x, hs.astype(BF16), hb, proj(COL_GA, D_MODEL), proj(COL_GB, D_MODEL),
                      i.wa, i.wb, i.wo)


def _mixer_kernel(TS, SIDE_NS, SIDE_TS, *refs):
    i, si, o, so, s = _split_refs(refs, MixerIn, SideIn, MixerOut, SideOut, MixerScratch)
    seq, ti = pl.program_id(0), pl.program_id(1)
    hist_rows = slice(V7X_SUBLANES - (LRU_CONV - 1), V7X_SUBLANES)

    @pl.when((seq == 0) & (ti == 0))
    def _():
        s.ext[:, :, hist_rows, :] = jnp.zeros_like(s.hist0)
        for ref in (s.h, s.c, s.n, s.m):
            ref[...] = jnp.zeros_like(ref)
        o.meta_x1[...] = _mixer_tile(i.meta.shape[0], i.meta[...], i, o, s)
        s.hist0[...] = s.ext[:, :, hist_rows, :]
        for ref0, ref in ((s.h0, s.h), (s.c0, s.c), (s.n0, s.n), (s.m0, s.m)):
            ref0[...] = ref[...]

    @pl.when(ti == 0)
    def _():
        s.ext[:, :, hist_rows, :] = s.hist0[...]
        for ref0, ref in ((s.h0, s.h), (s.c0, s.c), (s.n0, s.n), (s.m0, s.m)):
            ref[...] = ref0[...]

    def get_qkvo(h):
        sl = slice(h * D_HEAD, (h + 1) * D_HEAD)
        return si.q[:, sl], si.k[:, sl], si.v[:, sl], si.o[:, sl]

    gates = _mlstm_gates(SIDE_NS, SIDE_TS, si.gt[...], si.m0, so.m)
    outs = _mlstm_heads(SIDE_NS, SIDE_TS, get_qkvo, gates, i.hg,
                        (si.c0, si.n0, si.m0), (so.c, so.n, so.m))
    for h, out in enumerate(outs):
        so.hb[:, h * D_HEAD:(h + 1) * D_HEAD] = out

    o.x1[...] = _mixer_tile(TS, i.x[...], i, o, s)
    o.w_up[...] = i.w_up_f32[0].astype(BF16)

    @pl.when(ti == pl.num_programs(1) - 1)
    def _():
        o.c[...] = s.c[...]
        o.n[...] = s.n[...]
        o.m[...] = s.m[...]


def _lru_weight_specs():
    nb = D_LRU // V7X_MXU_DIM
    return [_resident((LRU_CONV, D_LRU)), _resident((1, D_LRU)),
            _resident((nb, V7X_MXU_DIM, V7X_MXU_DIM)), _resident((1, D_LRU)),
            _resident((nb, V7X_MXU_DIM, V7X_MXU_DIM)), _resident((1, D_LRU)),
            _resident((1, D_LRU))]


def _lru_weights(P):
    return (P["lru_conv_w"], P["lru_conv_b"], P["w_r"], P["lru_b_r"], P["w_i"], P["lru_b_i"],
            P["lru_lambda"])


def _state_specs(NS):
    def spec(*tail):
        zeros = (0,) * len(tail)
        return pl.BlockSpec((NS,) + tail, lambda s, t: (s,) + zeros)

    return [spec(LRU_CONV - 1, D_LRU), spec(1, D_LRU), spec(N_HEADS, D_HEAD, D_HEAD),
            spec(N_HEADS, D_HEAD), spec(1, V7X_LANES)]


def _state_shapes(NSEQ):
    return [jax.ShapeDtypeStruct((NSEQ, LRU_CONV - 1, D_LRU), F32),
            jax.ShapeDtypeStruct((NSEQ, 1, D_LRU), F32),
            jax.ShapeDtypeStruct((NSEQ, N_HEADS, D_HEAD, D_HEAD), F32),
            jax.ShapeDtypeStruct((NSEQ, N_HEADS, D_HEAD), F32),
            jax.ShapeDtypeStruct((NSEQ, 1, V7X_LANES), F32)]


def _mixer(x2, meta, w_up, side_in: SideIn, side_len, P, NSEQ, L, TS):
    NT = L // TS
    n_meta = meta.shape[0]
    rows = pl.BlockSpec((TS, D_MODEL), lambda s, t: (s * NT + t, 0))
    wsq = _resident((D_MODEL, D_MODEL + W_PITCH_PAD))
    n_side = side_in.c0.shape[0]
    side_ns = n_side // (NSEQ * NT)
    assert side_ns * NSEQ * NT == n_side
    step = lambda s, t: s * NT + t
    rows_up = D_MODEL // (NSEQ * NT)
    assert rows_up * NSEQ * NT == D_MODEL and rows_up % (2 * V7X_SUBLANES) == 0
    srows = lambda w: pl.BlockSpec((side_ns * side_len, w), lambda s, t: (step(s, t), 0))
    sstate = [pl.BlockSpec((side_ns, N_HEADS, D_HEAD, D_HEAD), lambda s, t: (step(s, t), 0, 0, 0)),
              pl.BlockSpec((side_ns, N_HEADS, D_HEAD), lambda s, t: (step(s, t), 0, 0)),
              pl.BlockSpec((side_ns, 1, V7X_LANES), lambda s, t: (step(s, t), 0, 0))]
    state_scratch = [pltpu.VMEM((1, 1, D_LRU), F32),
                     pltpu.VMEM((1, N_HEADS, D_HEAD, D_HEAD), F32),
                     pltpu.VMEM((1, N_HEADS, D_HEAD), F32),
                     pltpu.VMEM((1, 1, V7X_LANES), F32)]
    slabs = D_LRU // V7X_LANES
    return pl.pallas_call(
        functools.partial(_mixer_kernel, TS, side_ns, side_len),
        grid=(NSEQ, NT),
        in_specs=([rows, _resident((n_meta, D_MODEL))] + _in_proj_specs() + _lru_weight_specs()
                  + [_resident((1, D_MODEL)), wsq, wsq, wsq,
                     pl.BlockSpec((1, rows_up, 2 * D_FF), lambda s, t: (0, step(s, t), 0))]
                  + [srows(D_MODEL)] * 3 + [srows(GATE_W), srows(D_MODEL)] + sstate),
        out_specs=([rows, pl.BlockSpec((n_meta, D_MODEL), lambda s, t: (0, 0))] + _state_specs(1)
                   + [pl.BlockSpec((rows_up, 2 * D_FF), lambda s, t: (step(s, t), 0))]
                   + [srows(D_MODEL)] + sstate),
        out_shape=([jax.ShapeDtypeStruct((NSEQ * L, D_MODEL), F32),
                    jax.ShapeDtypeStruct((n_meta, D_MODEL), F32)] + _state_shapes(NSEQ)
                   + [jax.ShapeDtypeStruct((D_MODEL, 2 * D_FF), BF16)]
                   + [jax.ShapeDtypeStruct((n_side * side_len, D_MODEL), BF16)]
                   + _state_shapes(n_side)[2:]),
        scratch_shapes=([pltpu.VMEM((1, slabs, V7X_SUBLANES + TS, V7X_LANES), F32)] + state_scratch
                        + [pltpu.VMEM((1, slabs, LRU_CONV - 1, V7X_LANES), F32)] + state_scratch),
        compiler_params=_params(2, VMEM_MB_RESIDENT_HALF),
        name="mixer",
    )(x2, meta, *_in_proj_weights(P), *_lru_weights(P), P["mlstm_head_g"],
      P["w_branch_a"], P["w_branch_b"], P["w_out"], w_up, *side_in)


def _front_kernel(NS, TS, wt_ref, wa_ref, wb_ref, wo_ref, x_ref, conv0_ref, h0_ref, g_ref, b_ref,
                  bg_ref, cw_ref, cb_ref, wr_ref, br_ref, wi_ref, bi_ref, lam_ref, wdn_f32_ref,
                  w5_o, w2_o, wg_o, wa_o, wb_o, wo_o,
                  ha_ref, convo_ref, ho_ref, q_ref, k_ref, v_ref, o_ref, ga_ref, gb_ref, gt_ref,
                  wdn_ref, w5_ref, w2_ref, wg_ref, ext_ref, h_s):
    step = pl.program_id(0)

    @pl.when(step < PREP_STEPS)
    def _():
        _prep_weights_kernel(wt_ref, wa_ref, wb_ref, wo_ref, w5_o, w2_o, wg_o, wa_o, wb_o, wo_o)
        rows = pl.ds(pl.multiple_of(step * V7X_LANES, V7X_LANES), V7X_LANES)
        w5_ref[rows, :] = w5_o[...]
        w2_ref[rows, :] = w2_o[...]
        wg_ref[rows, :] = wg_o[...]

    @pl.when(step >= PREP_STEPS)
    def _():
        _front_tile(NS, TS, x_ref, conv0_ref, h0_ref, g_ref, w5_ref, w2_ref, b_ref, wg_ref, bg_ref,
                    cw_ref, cb_ref, wr_ref, br_ref, wi_ref, bi_ref, lam_ref, wdn_f32_ref,
                    ha_ref, convo_ref, ho_ref, q_ref, k_ref, v_ref, o_ref, ga_ref, gb_ref, gt_ref,
                    wdn_ref, ext_ref, h_s)


def _front_tile(NS, TS, x_ref, conv0_ref, h0_ref, g_ref, w5_ref, w2_ref, b_ref, wg_ref, bg_ref,
                cw_ref, cb_ref, wr_ref, br_ref, wi_ref, bi_ref, lam_ref, wdn_f32_ref,
                ha_ref, convo_ref, ho_ref, q_ref, k_ref, v_ref, o_ref, ga_ref, gb_ref, gt_ref,
                wdn_ref, ext_ref, h_s):
    wdn_ref[...] = _pitch_padded(wdn_f32_ref[0])
    _conv_init(ext_ref, conv0_ref, LRU_CONV, hist_major=True)
    h_s[...] = h0_ref[...]
    xn = _rms(x_ref[...], g_ref[...]).astype(BF16)
    proj = functools.partial(_in_proj, xn, w5_ref, w2_ref, b_ref)
    gt_ref[...] = jnp.dot(xn, wg_ref[...], preferred_element_type=F32) + bg_ref[...]
    hs = _lru_body(NS, TS, proj(COL_U, D_LRU), cw_ref, cb_ref, wr_ref, br_ref[...], wi_ref,
                   bi_ref[...], lam_ref[...], ext_ref, h_s, convo_ref, ho_ref, hist_major=True)
    ha_ref[...] = hs.astype(BF16)
    q_ref[...] = (proj(COL_Q, D_MODEL) * (D_HEAD ** -0.5)).astype(BF16)
    k_ref[...] = proj(COL_K, D_MODEL).astype(BF16)
    v_ref[...] = proj(COL_V, D_MODEL).astype(BF16)
    o_ref[...] = proj(COL_O, D_MODEL)
    ga_ref[...] = proj(COL_GA, D_MODEL)
    gb_ref[...] = proj(COL_GB, D_MODEL)


def _front(x2, conv0, h0, w_in, w_branch_a, w_branch_b, w_out, w_down, P, NSEQ, TS, NS):
    R = NS * TS
    M = NSEQ * TS
    steps = NSEQ // NS
    rows_dn = D_FF // steps
    assert rows_dn * steps == D_FF and rows_dn % (2 * V7X_SUBLANES) == 0
    n_in = w_in.shape[2]
    assert n_in == N_W5 + 2 * N_HEADS + N_W2
    wt = jnp.transpose(w_in[0])
    slab_of = lambda s: jnp.minimum(s, PREP_STEPS - 1)
    tile_of = lambda s: jnp.maximum(s - PREP_STEPS, 0)
    slab3 = lambda r, w: pl.BlockSpec((1, r, w), lambda s: (0, slab_of(s), 0))
    slab = lambda r, w: pl.BlockSpec((r, w), lambda s: (slab_of(s), 0))
    row = lambda w: pl.BlockSpec((R, w), lambda s: (tile_of(s), 0))
    st = [pl.BlockSpec((LRU_CONV - 1, NS, D_LRU), lambda s: (0, tile_of(s), 0)),
          pl.BlockSpec((NS, 1, D_LRU), lambda s: (tile_of(s), 0, 0))]
    sq_pad = D_MODEL + W_PITCH_PAD
    bf = lambda r, w: jax.ShapeDtypeStruct((r, w), BF16)
    f32o = jax.ShapeDtypeStruct((M, D_MODEL), F32)
    bf16o = bf(M, D_MODEL)
    return pl.pallas_call(
        functools.partial(_front_kernel, NS, TS),
        grid=(PREP_STEPS + steps,),
        in_specs=([pl.BlockSpec((n_in, V7X_LANES), lambda s: (0, slab_of(s)))]
                  + [slab3(V7X_LANES, D_MODEL)] * 3
                  + [row(D_MODEL)] + st
                  + [_resident((1, D_MODEL)), _resident((1, N_MAIN)), _resident((1, GATE_W))]
                  + _lru_weight_specs()
                  + [pl.BlockSpec((1, rows_dn, D_MODEL), lambda s: (0, tile_of(s), 0))]),
        out_specs=([slab(V7X_LANES, N_W5 + W_PITCH_PAD), slab(V7X_LANES, N_W2 + W_PITCH_PAD),
                    slab(V7X_LANES, GATE_W)] + [slab(V7X_LANES, sq_pad)] * 3
                   + [row(D_LRU)] + st + [row(D_MODEL)] * 6 + [row(GATE_W)]
                   + [pl.BlockSpec((rows_dn, sq_pad), lambda s: (tile_of(s), 0))]),
        out_shape=([bf(D_MODEL, N_W5 + W_PITCH_PAD), bf(D_MODEL, N_W2 + W_PITCH_PAD),
                    bf(D_MODEL, GATE_W), bf(D_MODEL, sq_pad), bf(D_MODEL, sq_pad),
                    bf(D_MODEL, sq_pad)]
                   + [bf16o, jax.ShapeDtypeStruct((LRU_CONV - 1, NSEQ, D_LRU), F32),
                      _state_shapes(NSEQ)[1], bf16o, bf16o, bf16o, f32o, f32o, f32o]
                   + [jax.ShapeDtypeStruct((M, GATE_W), F32), bf(D_FF, sq_pad)]),
        scratch_shapes=[pltpu.VMEM((D_MODEL, N_W5 + W_PITCH_PAD), BF16),
                        pltpu.VMEM((D_MODEL, N_W2 + W_PITCH_PAD), BF16),
                        pltpu.VMEM((D_MODEL, GATE_W), BF16),
                        pltpu.VMEM((NS, D_LRU // V7X_LANES, V7X_SUBLANES + TS, V7X_LANES), F32),
                        pltpu.VMEM((NS, 1, D_LRU), F32)],
        compiler_params=_params(1, VMEM_MB_WEIGHT_HANDOFF),
        name="front",
    )(wt, w_branch_a, w_branch_b, w_out, x2, conv0, h0, P["norm1_g"], P["b_main"], P["b_gate"],
      *_lru_weights(P), w_down)


def _back_kernel(NS, TS, x_ref, ha_ref, hb_ref, ga_ref, gb_ref, st0_ref, wa_ref, wb_ref, wo_ref,
                 g2_ref, wup_ref, cw_ref, cb_ref, wdn_ref, gf_ref, y_ref, sto_ref, ext_ref):
    _conv_init(ext_ref, st0_ref, FFN_CONV)
    x1 = _merge_out(x_ref[...], ha_ref[...], hb_ref[...], ga_ref[...], gb_ref[...],
                    wa_ref, wb_ref, wo_ref)
    y_ref[...] = _ffn_tile(NS, TS, x1, g2_ref, wup_ref, cw_ref, cb_ref, wdn_ref, gf_ref,
                           sto_ref, ext_ref)


def _ffn_meta_kernel(TS, x1_ref, meta_ref, g2_ref, wup_ref, cw_ref, cb_ref, wdn_ref, gf_ref,
                     y_ref, sto_ref, ext_ref, hist0_ref):
    seq, ti = pl.program_id(0), pl.program_id(1)
    hist = FFN_CONV - 1
    hist_rows = slice(V7X_SUBLANES - hist, V7X_SUBLANES)

    @pl.when((seq == 0) & (ti == 0))
    def _():
        xm = _rms(meta_ref[...], g2_ref[...]).astype(BF16)
        up = jnp.dot(xm, wup_ref[...], preferred_element_type=F32)
        for g in range(hist0_ref.shape[1]):
            hist0_ref[:, g, :, :] = up[None, up.shape[0] - hist:, g * V7X_LANES:(g + 1) * V7X_LANES]

    @pl.when(ti == 0)
    def _():
        ext_ref[:, :, hist_rows, :] = hist0_ref[...]

    y_ref[...] = _ffn_tile(1, TS, x1_ref[...], g2_ref, wup_ref, cw_ref, cb_ref, wdn_ref, gf_ref,
                           sto_ref, ext_ref)


def _ffn_tile(NS, TS, x1, g2_ref, wup_ref, cw_ref, cb_ref, wdn_ref, gf_ref, sto_ref, ext_ref):
    R = NS * TS
    xn = _rms(x1, g2_ref[...]).astype(BF16)

    def up_conv(col0):
        up = jnp.dot(xn, wup_ref[:, col0:col0 + FFN_CHUNK], preferred_element_type=F32)
        return _causal_conv(ext_ref, up.reshape(NS, TS, FFN_CHUNK), cw_ref, cb_ref, sto_ref,
                            col0).reshape(R, FFN_CHUNK)

    act = jnp.concatenate(
        [(jax.nn.gelu(up_conv(D_FF + c)) * up_conv(c)).astype(BF16)
         for c in range(0, D_FF, FFN_CHUNK)], axis=1)
    x2 = x1 + jnp.dot(act, wdn_ref[:, :D_MODEL], preferred_element_type=F32)
    return _rms(x2, gf_ref[...])


def _ffn_weight_specs():
    W = 2 * D_FF
    return [_resident((1, D_MODEL)), _resident((D_MODEL, W)), _resident((FFN_CONV, W)),
            _resident((1, W)), _resident((D_FF, D_MODEL + W_PITCH_PAD)), _resident((1, D_MODEL))]


def _ffn_weights(P):
    return (P["norm2_g"], P["w_up"], P["ffn_conv_w"], P["ffn_conv_b"], P["w_down"], P["final_g"])


def _ffn_long(x1, meta_x1, P, NSEQ, L, TS):
    NT = L // TS
    W = 2 * D_FF
    slabs = W // V7X_LANES
    rows = pl.BlockSpec((TS, D_MODEL), lambda s, t: (s * NT + t, 0))
    return pl.pallas_call(
        functools.partial(_ffn_meta_kernel, TS),
        grid=(NSEQ, NT),
        in_specs=[rows, _resident(meta_x1.shape)] + _ffn_weight_specs(),
        out_specs=[rows, pl.BlockSpec((1, FFN_CONV - 1, W), lambda s, t: (s, 0, 0))],
        out_shape=[jax.ShapeDtypeStruct((NSEQ * L, D_MODEL), F32),
                   jax.ShapeDtypeStruct((NSEQ, FFN_CONV - 1, W), F32)],
        scratch_shapes=[pltpu.VMEM((1, slabs, V7X_SUBLANES + TS, V7X_LANES), F32),
                        pltpu.VMEM((1, slabs, FFN_CONV - 1, V7X_LANES), F32)],
        compiler_params=_params(2, VMEM_MB_RESIDENT_HALF),
        name="ffn",
    )(x1, meta_x1, *_ffn_weights(P))


def _back(x2, ha, hb, ga, gb, st0, P, NSEQ, TS, NS):
    R = NS * TS
    W = 2 * D_FF
    row = pl.BlockSpec((R, D_MODEL), lambda s: (s, 0))
    stspec = pl.BlockSpec((NS, FFN_CONV - 1, W), lambda s: (s, 0, 0))
    wsq = _resident((D_MODEL, D_MODEL + W_PITCH_PAD))
    return pl.pallas_call(
        functools.partial(_back_kernel, NS, TS),
        grid=(NSEQ // NS,),
        in_specs=[row] * 5 + [stspec, wsq, wsq, wsq] + _ffn_weight_specs(),
        out_specs=[row, stspec],
        out_shape=[jax.ShapeDtypeStruct((NSEQ * TS, D_MODEL), F32),
                   jax.ShapeDtypeStruct((NSEQ, FFN_CONV - 1, W), F32)],
        scratch_shapes=[pltpu.VMEM((NS, W // V7X_LANES, V7X_SUBLANES + TS, V7X_LANES), F32)],
        compiler_params=_params(1, VMEM_MB_RESIDENT_HALF),
        name="back",
    )(x2, ha, hb, ga, gb, st0, P["w_branch_a"], P["w_branch_b"], P["w_out"], *_ffn_weights(P))


def _block_diag(w):
    bw = w.shape[1]
    per = V7X_MXU_DIM // bw
    nb = w.shape[0] // per
    w4 = w.reshape(nb, per, bw, 1, bw)
    on_diag = jnp.eye(per, dtype=w.dtype).reshape(1, per, 1, per, 1)
    return (w4 * on_diag).reshape(nb, V7X_MXU_DIM, V7X_MXU_DIM)


def _pitch_padded(w):
    pad = jnp.zeros((w.shape[0], W_PITCH_PAD), BF16)
    return jnp.concatenate([w.astype(BF16), pad], axis=1)


def _prep_weights_kernel(wt_ref, wa_ref, wb_ref, wo_ref, w5_ref, w2_ref, wg_ref, wa_o, wb_o, wo_o):
    wt = wt_ref[...]
    n_gate = 2 * N_HEADS
    w5_ref[...] = _pitch_padded(wt[:N_W5].T)
    w2_ref[...] = _pitch_padded(wt[N_W5 + n_gate:].T)
    g = jnp.concatenate([wt[N_W5:N_W5 + n_gate],
                         jnp.zeros((V7X_LANES - n_gate, wt.shape[1]), F32)], axis=0).T
    head_lane = lax.broadcasted_iota(jnp.int32, g.shape, 1) < N_HEADS
    ig = jnp.where(head_lane, g, 0.0)
    fg = jnp.where(head_lane, pltpu.roll(g, V7X_LANES - N_HEADS, axis=1), 0.0)
    wg_ref[...] = jnp.concatenate([ig, fg], axis=1).astype(BF16)

    wa_o[...] = _pitch_padded(wa_ref[0])
    wb_o[...] = _pitch_padded(wb_ref[0])
    wo_o[...] = _pitch_padded(wo_ref[0])


def _run_long_group(x3, meta, w_up, side_in, side_len, P, TS):
    NSEQ, L, _ = x3.shape
    x2 = x3.reshape(NSEQ * L, D_MODEL)
    x1, meta_x1, conv1, h1, c1, n1, m1, P["w_up"], *side_out = _mixer(
        x2, meta, w_up, side_in, side_len, P, NSEQ, L, TS)
    y, ffn1 = _ffn_long(x1, meta_x1, P, NSEQ, L, TS)
    return y.reshape(NSEQ, L, D_MODEL), (conv1, h1, c1, n1, m1, ffn1), side_out


def _short_group_front(x2, state, mixer_w, w_down, P, NSEQ, L, ns):
    conv0, h0, c0, n0, m0, _ = state
    hist_major = lambda a: jnp.transpose(a, (1, 0, 2))
    (P["w5"], P["w2"], P["w_gate"], P["w_branch_a"], P["w_branch_b"], P["w_out"],
     ha, conv1, h1, q, k, v, o, ga, gb, gt, P["w_down"]) = _front(
        x2, hist_major(conv0), h0, *mixer_w, w_down, P, NSEQ, L, ns)
    return SideIn(q, k, v, gt, o, c0, n0, m0), (ha, ga, gb, hist_major(conv1), h1)


def _short_group_back(x2, front, side_out, ffn0, P, NSEQ, L, ns):
    ha, ga, gb, conv1, h1 = front
    hb, c1, n1, m1 = side_out
    y, ffn1 = _back(x2, ha, hb, ga, gb, ffn0, P, NSEQ, L, ns)
    return y.reshape(NSEQ, L, D_MODEL), (conv1, h1, c1, n1, m1, ffn1)


def kernel(x_prompt, x_sample, state_lru_conv, state_lru_h, state_mlstm_C, state_mlstm_n,
           state_mlstm_m, state_ffn_conv, meta_tokens, norm1_g, w_in, b_in, lru_conv_w,
           lru_conv_b, lru_w_r, lru_b_r, lru_w_i, lru_b_i, lru_lambda, mlstm_head_g,
           w_branch_a, w_branch_b, w_out, norm2_g, w_up, ffn_conv_w, ffn_conv_b, w_down, final_g):
    assert w_in.shape[0] == 1, "single-layer trunk"
    b0 = b_in[0]
    gate_pad = jnp.zeros((V7X_LANES - N_HEADS,), b0.dtype)
    row = lambda a: a.reshape(1, -1).astype(F32)
    P = {
        "norm1_g": row(norm1_g[0]),
        "b_main": row(jnp.concatenate([b0[:N_W5], b0[N_W5 + 2 * N_HEADS:]])),
        "b_gate": row(jnp.concatenate([b0[N_W5:N_W5 + N_HEADS], gate_pad,
                                       b0[N_W5 + N_HEADS:N_W5 + 2 * N_HEADS], gate_pad])),
        "lru_conv_w": lru_conv_w[0],
        "lru_conv_b": row(lru_conv_b[0]),
        "w_r": _block_diag(lru_w_r[0]).astype(BF16),
        "lru_b_r": row(lru_b_r[0]),
        "w_i": _block_diag(lru_w_i[0]).astype(BF16),
        "lru_b_i": row(lru_b_i[0]),
        "lru_lambda": row(lru_lambda[0]),
        "mlstm_head_g": row(mlstm_head_g[0]),
        "norm2_g": row(norm2_g[0]),
        "ffn_conv_w": ffn_conv_w[0],
        "ffn_conv_b": row(ffn_conv_b[0]),
        "final_g": row(final_g),
    }

    def pack_state(conv, h, c, n, m, ffn):
        nseq = h.shape[0]
        m_pad = jnp.pad(m.astype(F32)[:, None, :], ((0, 0), (0, 0), (0, V7X_LANES - N_HEADS)))
        return (conv.astype(F32), h.astype(F32).reshape(nseq, 1, D_LRU), c.astype(F32),
                n.astype(F32), m_pad, ffn.astype(F32))

    def unpack_state(st):
        conv, h, c, n, m, ffn = st
        return (conv[None], h.reshape(1, -1, D_LRU), c[None], n[None],
                m[:, 0, :N_HEADS][None], ffn[None])

    sample_state0 = pack_state(state_lru_conv[0], state_lru_h[0], state_mlstm_C[0],
                               state_mlstm_n[0], state_mlstm_m[0], state_ffn_conv[0])
    n_sample, l_sample, _ = x_sample.shape
    xs2 = x_sample.reshape(n_sample * l_sample, D_MODEL)
    short = dict(NSEQ=n_sample, L=l_sample, ns=SHORT_ROWS // l_sample)
    side_in, front = _short_group_front(xs2, sample_state0,
                                        (w_in, w_branch_a, w_branch_b, w_out), w_down, P, **short)
    y_prompt, prompt_state, side_out = _run_long_group(x_prompt, meta_tokens.astype(F32), w_up,
                                                       side_in, l_sample, P, LONG_TS)
    y_sample, sample_state = _short_group_back(xs2, front, side_out, sample_state0[5], P, **short)
    return (y_prompt, y_sample) + unpack_state(prompt_state) + unpack_state(sample_state)
```
